```python
import math
import jax, jax.numpy as jnp
from jax import lax
import numpy as np

D_MODEL = 1024
BATCH = 8
SEQ = 4096
DEPTH = 1
DEC_BATCH = 128
DEC_SEQ = 1
PAST_LEN = 16384
PAGE_SIZE = 128

HEAD_DIM = 64
A_HEADS = D_MODEL // HEAD_DIM // 2
B_HEADS = D_MODEL // HEAD_DIM // 2
B_KV_HEADS = 2
B_GROUP = B_HEADS // B_KV_HEADS
A_W = A_HEADS * HEAD_DIM
B_W = B_HEADS * HEAD_DIM
B_KV_W = B_KV_HEADS * HEAD_DIM
IN_SPLITS = (A_W, A_W, A_W, B_W, B_KV_W, B_KV_W)
IN_DIM = sum(IN_SPLITS)
DIL_GROUPS = ((128, 1), (512, 4), (2048, 16))
A_WINDOW = 2048
B_WINDOW = 128
BLOCK = 128
ROPE_THETA = 10000.0
MEM_TOKENS = 256
X_HEADS = 4
X_HEAD_DIM = D_MODEL // X_HEADS
D_FF = ((8 * D_MODEL // 3 + 127) // 128) * 128
CONV_W = 3
EPS = 1e-6
NEG = -1e30

kernel_name = "hybrid_dilated_swa_sink_memory_convffn_step"


def rms_norm(x, g):
    xf = x.astype(jnp.float32)
    y = xf * lax.rsqrt(jnp.mean(xf * xf, axis=-1, keepdims=True) + EPS)
    return (y * g.astype(jnp.float32)).astype(x.dtype)


def rope(x, pos):
    half = HEAD_DIM // 2
    inv = jnp.power(ROPE_THETA, -jnp.arange(half, dtype=jnp.float32) / half)
    ang = pos.astype(jnp.float32)[:, None] * inv[None, :]
    cos = jnp.cos(ang)[:, None, :]
    sin = jnp.sin(ang)[:, None, :]
    xf = x.astype(jnp.float32)
    x1, x2 = xf[..., :half], xf[..., half:]
    return jnp.concatenate([x1 * cos - x2 * sin, x2 * cos + x1 * sin], axis=-1).astype(x.dtype)


def banded_attention(q, k, v, window):
    n, L = q.shape[:2]
    nb = -(-L // BLOCK)
    pad = nb * BLOCK - L
    padl = lambda a: jnp.pad(a, ((0, 0), (0, pad)) + ((0, 0),) * (a.ndim - 2))
    q, k, v = padl(q), padl(k), padl(v)
    qb = q.reshape((n, nb, BLOCK) + q.shape[2:])
    kb = k.reshape((n, nb, BLOCK) + k.shape[2:])
    vb = v.reshape((n, nb, BLOCK) + v.shape[2:])

    def with_prev(a):
        prev = jnp.pad(a, ((0, 0), (1, 0)) + ((0, 0),) * (a.ndim - 2))[:, :-1]
        return jnp.concatenate([prev, a], axis=2)

    kk, vv = with_prev(kb), with_prev(vb)
    scale = q.shape[-1] ** -0.5
    s = jnp.einsum('nbqhgd,nbkhd->nbhgqk', qb, kk, preferred_element_type=jnp.float32) * scale
    qpos = jnp.arange(BLOCK)[:, None] + BLOCK
    kpos = jnp.arange(2 * BLOCK)[None, :]
    dist = qpos - kpos
    band = (dist >= 0) & (dist <= window)
    first = (jnp.arange(nb) == 0)[:, None, None]
    valid = band[None] & ~(first & (kpos < BLOCK)[None])
    s = jnp.where(valid[None, :, None, None], s, NEG)
    lse = jax.nn.logsumexp(s, axis=-1)
    p = jnp.exp(s - lse[..., None])
    o = jnp.einsum('nbhgqk,nbkhd->nbqhgd', p.astype(vv.dtype), vv)
    o = o.reshape((n, nb * BLOCK) + o.shape[3:])[:, :L]
    lse = jnp.moveaxis(lse, 4, 2).reshape((n, nb * BLOCK) + lse.shape[2:4])[:, :L]
    return o, lse


def combine_by_denominator(o, lse, axis):
    w = jax.nn.softmax(lse, axis=axis)
    return jnp.sum(o.astype(jnp.float32) * w[..., None], axis=axis).astype(o.dtype)


def dilated_prompt(q, k, v):
    B, S, H, D = q.shape
    outs, lses = [], []
    for w, d in DIL_GROUPS:
        to_res = lambda a: a.reshape(B, S // d, d, H, D).transpose(0, 2, 1, 3, 4).reshape(B * d, S // d, H, D)
        o, lse = banded_attention(to_res(q)[:, :, :, None], to_res(k), to_res(v), w // d)
        o = o[:, :, :, 0].reshape(B, d, S // d, H, D).transpose(0, 2, 1, 3, 4).reshape(B, S, H, D)
        lse = lse[..., 0].reshape(B, d, S // d, H).transpose(0, 2, 1, 3).reshape(B, S, H)
        outs.append(o)
        lses.append(lse)
    return combine_by_denominator(jnp.stack(outs), jnp.stack(lses), axis=0)


def dilated_sample(q, kc, vc):
    T = q.shape[1]
    L = kc.shape[1] - T
    idx = np.stack([L + np.arange(T)[:, None] - (np.arange(w // d + 1) * d)[None, :] for w, d in DIL_GROUPS])
    valid = jnp.asarray(idx >= 0)
    idx_c = np.maximum(idx, 0)
    kg = kc[:, idx_c]
    vg = vc[:, idx_c]
    s = jnp.einsum('nthd,ngtkhd->ngthk', q, kg, preferred_element_type=jnp.float32) * (HEAD_DIM ** -0.5)
    s = jnp.where(valid[None, :, :, None, :], s, NEG)
    lse = jax.nn.logsumexp(s, axis=-1)
    p = jnp.exp(s - lse[..., None])
    o = jnp.einsum('ngthk,ngtkhd->ngthd', p.astype(vg.dtype), vg)
    return combine_by_denominator(o, lse, axis=1)


def apply_sinks(o, lse, sinks):
    f = jax.nn.sigmoid(lse - sinks.reshape(B_KV_HEADS, B_GROUP).astype(jnp.float32))
    return o * f[..., None].astype(o.dtype)


def swa_sample(q, kc, vc, sinks):
    T = q.shape[1]
    L = kc.shape[1] - T
    dist = jnp.arange(T)[:, None] - (jnp.arange(L + T) - L)[None, :]
    valid = (dist >= 0) & (dist <= B_WINDOW)
    s = jnp.einsum('nthgd,nkhd->nhgtk', q, kc, preferred_element_type=jnp.float32) * (HEAD_DIM ** -0.5)
    s = jnp.where(valid[None, None, None], s, NEG)
    lse = jax.nn.logsumexp(s, axis=-1)
    p = jnp.exp(s - lse[..., None])
    o = jnp.einsum('nhgtk,nkhd->nthgd', p.astype(vc.dtype), vc)
    return apply_sinks(o, jnp.moveaxis(lse, 3, 1), sinks)


def project_mixers(h, w_in, pos):
    N, T, _ = h.shape
    z = h @ w_in
    points = [int(c) for c in np.cumsum(IN_SPLITS)[:-1]]
    qa, ka, va, qb, kb, vb = jnp.split(z, points, axis=-1)
    qa = rope(qa.reshape(N, T, A_HEADS, HEAD_DIM), pos)
    ka = rope(ka.reshape(N, T, A_HEADS, HEAD_DIM), pos)
    va = va.reshape(N, T, A_HEADS, HEAD_DIM)
    qb = rope(qb.reshape(N, T, B_HEADS, HEAD_DIM), pos).reshape(N, T, B_KV_HEADS, B_GROUP, HEAD_DIM)
    kb = rope(kb.reshape(N, T, B_KV_HEADS, HEAD_DIM), pos)
    vb = vb.reshape(N, T, B_KV_HEADS, HEAD_DIM)
    return qa, ka, va, qb, kb, vb


def mixers_out(oa, ob, g_out_a, g_out_b, w_out):
    N, T = oa.shape[:2]
    oa = rms_norm(oa.reshape(N, T, A_W), g_out_a)
    ob = rms_norm(ob.reshape(N, T, B_W), g_out_b)
    return jnp.concatenate([oa, ob], axis=-1) @ w_out


def memory_kv(mem, g_mem, w_mem_kv):
    N, M, _ = mem.shape
    kv = rms_norm(mem, g_mem) @ w_mem_kv
    k, v = jnp.split(kv, 2, axis=-1)
    return k.reshape(N, M, X_HEADS, X_HEAD_DIM), v.reshape(N, M, X_HEADS, X_HEAD_DIM)


def cross_attention(h, mk, mv, w_xq, w_xo):
    N, T, _ = h.shape
    q = (h @ w_xq).reshape(N, T, X_HEADS, X_HEAD_DIM)
    s = jnp.einsum('nthd,nmhd->nhtm', q, mk, preferred_element_type=jnp.float32) * (X_HEAD_DIM ** -0.5)
    p = jax.nn.softmax(s, axis=-1)
    o = jnp.einsum('nhtm,nmhd->nthd', p.astype(mv.dtype), mv).reshape(N, T, D_MODEL)
    return o @ w_xo


def conv_ffn(h, conv_state, w_up, conv_w, conv_b, w_down):
    T = h.shape[1]
    gate, val = jnp.split(h @ w_up, 2, axis=-1)
    gx = jnp.concatenate([conv_state, gate], axis=1)
    conv = conv_b + sum(gx[:, i:i + T] * conv_w[i] for i in range(CONV_W))
    y = (jax.nn.silu(conv) * val) @ w_down
    return y, gx[:, -(CONV_W - 1):]


def prompt_layer(x, mem, g_mix, w_in, g_out_a, g_out_b, sinks, w_out, g_cross, g_mem, w_xq, w_mem_kv, w_xo,
                 g_ffn, w_up, conv_w, conv_b, w_down):
    N, S, _ = x.shape
    pos = jnp.arange(S, dtype=jnp.int32)
    qa, ka, va, qb, kb, vb = project_mixers(rms_norm(x, g_mix), w_in, pos)
    oa = dilated_prompt(qa, ka, va)
    ob, lse_b = banded_attention(qb, kb, vb, B_WINDOW)
    ob = apply_sinks(ob, lse_b, sinks)
    x = x + mixers_out(oa, ob, g_out_a, g_out_b, w_out)
    mk, mv = memory_kv(mem, g_mem, w_mem_kv)
    x = x + cross_attention(rms_norm(x, g_cross), mk, mv, w_xq, w_xo)
    y, conv_state = conv_ffn(rms_norm(x, g_ffn), jnp.zeros((N, CONV_W - 1, D_FF), x.dtype), w_up, conv_w, conv_b, w_down)
    x = x + y
    la, lb = min(A_WINDOW, S), min(B_WINDOW, S)
    return x, (ka[:, S - la:], va[:, S - la:], kb[:, S - lb:], vb[:, S - lb:], mk, mv, conv_state)


def sample_layer(x, a_k, a_v, b_k, b_v, mem_k, mem_v, conv_state, g_mix, w_in, g_out_a, g_out_b, sinks, w_out,
                 g_cross, w_xq, w_xo, g_ffn, w_up, conv_w, conv_b, w_down):
    N, T, _ = x.shape
    pos = PAST_LEN + jnp.arange(T, dtype=jnp.int32)
    qa, ka, va, qb, kb, vb = project_mixers(rms_norm(x, g_mix), w_in, pos)
    kca, vca = jnp.concatenate([a_k, ka], axis=1), jnp.concatenate([a_v, va], axis=1)
    kcb, vcb = jnp.concatenate([b_k, kb], axis=1), jnp.concatenate([b_v, vb], axis=1)
    oa = dilated_sample(qa, kca, vca)
    ob = swa_sample(qb, kcb, vcb, sinks)
    x = x + mixers_out(oa, ob, g_out_a, g_out_b, w_out)
    x = x + cross_attention(rms_norm(x, g_cross), mem_k, mem_v, w_xq, w_xo)
    y, new_conv = conv_ffn(rms_norm(x, g_ffn), conv_state, w_up, conv_w, conv_b, w_down)
    x = x + y
    la, lb = a_k.shape[1], b_k.shape[1]
    return x, (kca[:, -la:], vca[:, -la:], kcb[:, -lb:], vcb[:, -lb:], new_conv)


def setup_inputs(seed: int = 0) -> dict:
    key = jax.random.key(seed)
    ks = iter(jax.random.split(key, 40))
    nrm = lambda shape, scale: jax.random.normal(next(ks), shape, jnp.float32) * scale
    gain = lambda shape: 1.0 + nrm(shape, 0.02)
    la, lb = min(A_WINDOW, PAST_LEN), min(B_WINDOW, PAST_LEN)
    return {
        "x_prompt": nrm((BATCH, SEQ, D_MODEL), 1.0),
        "x_sample": nrm((DEC_BATCH, DEC_SEQ, D_MODEL), 1.0),
        "cache_a_k": nrm((DEPTH, DEC_BATCH, la, A_HEADS, HEAD_DIM), 1.0),
        "cache_a_v": nrm((DEPTH, DEC_BATCH, la, A_HEADS, HEAD_DIM), 1.0),
        "cache_b_k": nrm((DEPTH, DEC_BATCH, lb, B_KV_HEADS, HEAD_DIM), 1.0),
        "cache_b_v": nrm((DEPTH, DEC_BATCH, lb, B_KV_HEADS, HEAD_DIM), 1.0),
        "cache_mem_k": nrm((DEPTH, DEC_BATCH, MEM_TOKENS, X_HEADS, X_HEAD_DIM), 1.0),
        "cache_mem_v": nrm((DEPTH, DEC_BATCH, MEM_TOKENS, X_HEADS, X_HEAD_DIM), 1.0),
        "state_conv": nrm((DEPTH, DEC_BATCH, CONV_W - 1, D_FF), 1.0),
        "mem_prompt": nrm((BATCH, MEM_TOKENS, D_MODEL), 1.0),
        "g_mix": gain((DEPTH, D_MODEL)),
        "w_in": nrm((DEPTH, D_MODEL, IN_DIM), D_MODEL ** -0.5),
        "g_out_a": gain((DEPTH, A_W)),
        "g_out_b": gain((DEPTH, B_W)),
        "sinks": nrm((DEPTH, B_HEADS), 0.5),
        "w_out": nrm((DEPTH, A_W + B_W, D_MODEL), (A_W + B_W) ** -0.5),
        "g_cross": gain((DEPTH, D_MODEL)),
        "g_mem": gain((DEPTH, D_MODEL)),
        "w_xq": nrm((DEPTH, D_MODEL, X_HEADS * X_HEAD_DIM), D_MODEL ** -0.5),
        "w_mem_kv": nrm((DEPTH, D_MODEL, 2 * X_HEADS * X_HEAD_DIM), D_MODEL ** -0.5),
        "w_xo": nrm((DEPTH, X_HEADS * X_HEAD_DIM, D_MODEL), (X_HEADS * X_HEAD_DIM) ** -0.5),
        "g_ffn": gain((DEPTH, D_MODEL)),
        "w_up": nrm((DEPTH, D_MODEL, 2 * D_FF), D_MODEL ** -0.5),
        "conv_w": nrm((DEPTH, CONV_W, D_FF), CONV_W ** -0.5),
        "conv_b": nrm((DEPTH, D_FF), 0.01),
        "w_down": nrm((DEPTH, D_FF, D_MODEL), D_FF ** -0.5),
        "g_final": gain((D_MODEL,)),
    }


def reference(x_prompt, x_sample, cache_a_k, cache_a_v, cache_b_k, cache_b_v, cache_mem_k, cache_mem_v, state_conv,
              mem_prompt, g_mix, w_in, g_out_a, g_out_b, sinks, w_out, g_cross, g_mem, w_xq, w_mem_kv, w_xo,
              g_ffn, w_up, conv_w, conv_b, w_down, g_final):
    hp, hs = x_prompt, x_sample
    p_states, s_states = [], []
    for l in range(DEPTH):
        hp, ps = prompt_layer(hp, mem_prompt, g_mix[l], w_in[l], g_out_a[l], g_out_b[l], sinks[l], w_out[l],
                              g_cross[l], g_mem[l], w_xq[l], w_mem_kv[l], w_xo[l], g_ffn[l], w_up[l], conv_w[l],
                              conv_b[l], w_down[l])
        hs, ss = sample_layer(hs, cache_a_k[l], cache_a_v[l], cache_b_k[l], cache_b_v[l], cache_mem_k[l],
                              cache_mem_v[l], state_conv[l], g_mix[l], w_in[l], g_out_a[l], g_out_b[l], sinks[l],
                              w_out[l], g_cross[l], w_xq[l], w_xo[l], g_ffn[l], w_up[l], conv_w[l], conv_b[l],
                              w_down[l])
        p_states.append(ps)
        s_states.append(ss)
    stack = lambda states, i: jnp.stack([st[i] for st in states])
    y_prompt = rms_norm(hp, g_final)
    y_sample = rms_norm(hs, g_final)
    p_a_k, p_a_v, p_b_k, p_b_v = stack(p_states, 0), stack(p_states, 1), stack(p_states, 2), stack(p_states, 3)
    p_mem_k, p_mem_v, p_conv = stack(p_states, 4), stack(p_states, 5), stack(p_states, 6)
    s_a_k, s_a_v, s_b_k, s_b_v = stack(s_states, 0), stack(s_states, 1), stack(s_states, 2), stack(s_states, 3)
    s_conv = stack(s_states, 4)
    return (y_prompt, y_sample, p_a_k, p_a_v, p_b_k, p_b_v, p_mem_k, p_mem_v, p_conv, s_a_k, s_a_v, s_b_k, s_b_v, s_conv)
```

```python
import functools

import jax
import jax.numpy as jnp
import numpy as np
from jax import lax
from jax.experimental import pallas as pl
from jax.experimental.pallas import tpu as pltpu

F32 = jnp.float32
BF16 = jnp.bfloat16

HEAD_DIM = 64
A_HEADS = 8
B_HEADS = 8
B_KV_HEADS = 2
B_GROUP = B_HEADS // B_KV_HEADS
A_W = A_HEADS * HEAD_DIM
B_W = B_HEADS * HEAD_DIM
B_KV_W = B_KV_HEADS * HEAD_DIM
DILATIONS = (1, 4, 16)
A_WINDOW = 2048
B_WINDOW = 128
BLOCK = 128
ROPE_THETA = 10000.0
PAST_LEN = 16384
X_HEADS = 4
CONV_W = 3
EPS = 1e-6
NEG = -1e30

V7X_LANES = 128
V7X_MXU_DIM = 256
V7X_VMEM_BYTES = 64 * 1024 * 1024
VMEM_LIMIT = V7X_VMEM_BYTES - 8 * 1024 * 1024


def _params(*sem):
    return pltpu.CompilerParams(dimension_semantics=sem, vmem_limit_bytes=VMEM_LIMIT)


def _rms(x, g):
    return (x * lax.rsqrt(jnp.mean(x * x, axis=-1, keepdims=True) + EPS)) * g


def _dot(a, b):
    return jnp.dot(a, b, preferred_element_type=F32)


def _dot_t(a, b):
    return lax.dot_general(a, b, (((1,), (1,)), ((), ())), preferred_element_type=F32)


def _rope_table_kernel(inv_ref, cos_ref, sin_ref, *, pos0, pos_step):
    rows = cos_ref.shape[0]
    row = lax.broadcasted_iota(jnp.int32, (rows, V7X_LANES), 0)
    lane = lax.broadcasted_iota(jnp.int32, (rows, V7X_LANES), 1)
    ang = (pos0 + pos_step * row).astype(F32) * inv_ref[...]
    first_half = (lane % HEAD_DIM) < (HEAD_DIM // 2)
    cos_ref[...] = jnp.cos(ang)
    sin_ref[...] = jnp.where(first_half, -jnp.sin(ang), jnp.sin(ang))


def _rope_tables(rows, pos0, pos_step):
    half = HEAD_DIM // 2
    inv = jnp.power(ROPE_THETA, -jnp.arange(half, dtype=F32) / half)
    inv = jnp.tile(inv, V7X_LANES // half).reshape(1, V7X_LANES)
    return pl.pallas_call(
        functools.partial(_rope_table_kernel, pos0=pos0, pos_step=pos_step),
        out_shape=[jax.ShapeDtypeStruct((rows, V7X_LANES), F32)] * 2,
        name="rope_tables",
    )(inv)


def _rope(slab, cos, sin, first_half):
    partner = jnp.where(first_half, pltpu.roll(slab, 96, 1), pltpu.roll(slab, 32, 1))
    return slab * cos + partner * sin


def _inproj_kernel(x_ref, g_ref, w_ref, cos_ref, sin_ref,
                   qa_ref, ka_ref, va_ref, qb_ref, kb_ref, vb_ref, *tails,
                   tail_skip, n_tiles):
    tm = x_ref.shape[0]
    hn = _rms(x_ref[...], g_ref[...]).astype(BF16)
    cos = cos_ref[...]
    sin = sin_ref[...]
    lane = lax.broadcasted_iota(jnp.int32, (tm, V7X_LANES), 1)
    first_half = (lane % HEAD_DIM) < (HEAD_DIM // 2)

    def seg(c0, width):
        return _dot(hn, w_ref[:, c0:c0 + width])

    def roped(z):
        return jnp.concatenate(
            [_rope(z[:, c:c + V7X_LANES], cos, sin, first_half)
             for c in range(0, z.shape[1], V7X_LANES)], axis=1)

    qa_ref[...] = roped(seg(0, A_W))
    ka = roped(seg(A_W, A_W))
    ka_ref[...] = ka
    va = seg(2 * A_W, A_W)
    va_ref[...] = va
    qb_ref[...] = roped(seg(3 * A_W, B_W))
    kvb = seg(3 * A_W + B_W, 2 * B_KV_W)
    kb = roped(kvb[:, :B_KV_W])
    vb = kvb[:, B_KV_W:]
    kb_ref[...] = kb
    vb_ref[...] = vb

    if tails:
        kat_ref, vat_ref, kbt_ref, vbt_ref = tails
        i = pl.program_id(1)

        @pl.when(i >= tail_skip)
        def _():
            kat_ref[...] = ka
            vat_ref[...] = va

        @pl.when(i == n_tiles - 1)
        def _():
            kbt_ref[...] = kb[tm - B_WINDOW:, :]
            vbt_ref[...] = vb[tm - B_WINDOW:, :]


def _inproj(x, g, w, cos, sin, *, tm, with_tails):
    B, S, D = x.shape
    n_tiles = S // tm
    la = min(A_WINDOW, S)
    lb = min(B_WINDOW, S)
    tail_skip = (S - la) // tm
    tok = lambda w_: pl.BlockSpec((None, tm, w_), lambda b, i: (b, i, 0))
    const = lambda shape: pl.BlockSpec(shape, lambda b, i: (0,) * len(shape))
    out_shape = [jax.ShapeDtypeStruct((B, S, A_W), F32)] * 3 + [
        jax.ShapeDtypeStruct((B, S, B_W), F32),
        jax.ShapeDtypeStruct((B, S, B_KV_W), F32),
        jax.ShapeDtypeStruct((B, S, B_KV_W), F32)]
    out_specs = [tok(A_W)] * 3 + [tok(B_W), tok(B_KV_W), tok(B_KV_W)]
    if with_tails:
        out_shape += [jax.ShapeDtypeStruct((B, la, A_W), F32)] * 2
        out_shape += [jax.ShapeDtypeStruct((B, lb, B_KV_W), F32)] * 2
        a_tail = pl.BlockSpec((None, tm, A_W), lambda b, i: (b, jnp.maximum(i - tail_skip, 0), 0))
        b_tail = pl.BlockSpec((None, lb, B_KV_W), lambda b, i: (b, 0, 0))
        out_specs += [a_tail, a_tail, b_tail, b_tail]
    return pl.pallas_call(
        functools.partial(_inproj_kernel, tail_skip=tail_skip, n_tiles=n_tiles),
        grid=(B, n_tiles),
        in_specs=[tok(D), const((1, D)), const(w.shape),
                  pl.BlockSpec((tm, V7X_LANES), lambda b, i: (i, 0)),
                  pl.BlockSpec((tm, V7X_LANES), lambda b, i: (i, 0))],
        out_specs=out_specs,
        out_shape=out_shape,
        compiler_params=_params("arbitrary", "arbitrary"),
        name="inproj_rope",
    )(x, g, w, cos, sin)


def _band_bias(first):
    a = lax.broadcasted_iota(jnp.int32, (2 * BLOCK, 2 * BLOCK), 0) % BLOCK
    b = lax.broadcasted_iota(jnp.int32, (2 * BLOCK, 2 * BLOCK), 1)
    valid = (b >= a) & (b <= a + BLOCK)
    if first:
        valid = valid & (b >= BLOCK)
    return jnp.where(valid, 0.0, NEG).astype(F32)


def _two_head_block(q, kk, vv, bias):
    lane = lax.broadcasted_iota(jnp.int32, (BLOCK, V7X_LANES), 1)
    lo = lane < HEAD_DIM
    q = q * (HEAD_DIM ** -0.5)
    qs = jnp.concatenate([jnp.where(lo, q, 0.0), jnp.where(lo, 0.0, q)], axis=0).astype(BF16)
    s = _dot_t(qs, kk) + bias
    m = jnp.max(s, axis=-1, keepdims=True)
    p = jnp.exp(s - m)
    l = jnp.sum(p, axis=-1, keepdims=True)
    pv = _dot(p.astype(BF16), vv)
    acc = jnp.where(lo, pv[:BLOCK], pv[BLOCK:])
    m2 = jnp.where(lo, m[:BLOCK], m[BLOCK:])
    l2 = jnp.where(lo, l[:BLOCK], l[BLOCK:])
    return acc, m2, l2


def _rows(start, d):
    if d == 1:
        return pl.ds(pl.multiple_of(start, BLOCK), BLOCK)
    return pl.ds(start, BLOCK, stride=d)


def _branch_blocks(q_ref, k_ref, v_ref, bias_ref, d, visit):
    S = q_ref.shape[0]
    per_residue = S // (BLOCK * d)

    def body(i, carry):
        r = i // per_residue
        jb = i % per_residue
        start = r + jb * (BLOCK * d)
        first = jb == 0
        pstart = jnp.where(first, start, start - BLOCK * d)
        rows = _rows(start, d)
        prows = _rows(pstart, d)
        kk = jnp.concatenate([k_ref[prows, :], k_ref[rows, :]], axis=0).astype(BF16)
        vv = jnp.concatenate([v_ref[prows, :], v_ref[rows, :]], axis=0).astype(BF16)
        bias = bias_ref[jnp.where(first, 1, 0)]
        acc, m, l = _two_head_block(q_ref[rows, :], kk, vv, bias)
        visit(rows, acc, m, l)
        return carry

    lax.fori_loop(0, S // BLOCK, body, 0)


def _attn_a_kernel(q_ref, k_ref, v_ref, o_ref, m_ref, l_ref, bias_ref):
    bias_ref[0] = _band_bias(False)
    bias_ref[1] = _band_bias(True)

    def first_visit(rows, acc, m, l):
        o_ref[rows, :] = acc
        m_ref[rows, :] = m
        l_ref[rows, :] = l

    def merged(rows, acc, m, l):
        m_old = m_ref[rows, :]
        m_new = jnp.maximum(m_old, m)
        a_old = jnp.exp(m_old - m_new)
        a_new = jnp.exp(m - m_new)
        return (a_old * o_ref[rows, :] + a_new * acc, m_new, a_old * l_ref[rows, :] + a_new * l)

    def mid_visit(rows, acc, m, l):
        acc, m, l = merged(rows, acc, m, l)
        o_ref[rows, :] = acc
        m_ref[rows, :] = m
        l_ref[rows, :] = l

    def last_visit(rows, acc, m, l):
        acc, m, l = merged(rows, acc, m, l)
        o_ref[rows, :] = acc / l

    visits = [first_visit] + [mid_visit] * (len(DILATIONS) - 2) + [last_visit]
    for d, visit in zip(DILATIONS, visits):
        _branch_blocks(q_ref, k_ref, v_ref, bias_ref, d, visit)


def _attn_b_kernel(q_ref, k_ref, v_ref, sink_ref, o_ref, bias_ref):
    bias_ref[0] = _band_bias(False)
    bias_ref[1] = _band_bias(True)
    sink = sink_ref[0:1, :]

    def visit(rows, acc, m, l):
        o_ref[rows, :] = acc / (l + jnp.exp(sink - m))

    _branch_blocks(q_ref, k_ref, v_ref, bias_ref, 1, visit)


def _attn_a(q, k, v):
    B, S, W = q.shape
    slab = pl.BlockSpec((None, S, V7X_LANES), lambda b, j: (b, 0, j))
    return pl.pallas_call(
        _attn_a_kernel,
        grid=(B, W // V7X_LANES),
        in_specs=[slab, slab, slab],
        out_specs=slab,
        out_shape=jax.ShapeDtypeStruct((B, S, W), F32),
        scratch_shapes=[pltpu.VMEM((S, V7X_LANES), F32), pltpu.VMEM((S, V7X_LANES), F32),
                        pltpu.VMEM((2, 2 * BLOCK, 2 * BLOCK), F32)],
        compiler_params=_params("arbitrary", "arbitrary"),
        name="mixer_a_attention",
    )(q, k, v)


def _attn_b(q, k, v, sink_lanes):
    B, S, W = q.shape
    slab = pl.BlockSpec((None, S, V7X_LANES), lambda b, j: (b, 0, j))
    kv = pl.BlockSpec((None, S, V7X_LANES), lambda b, j: (b, 0, 0))
    return pl.pallas_call(
        _attn_b_kernel,
        grid=(B, W // V7X_LANES),
        in_specs=[slab, kv, kv, pl.BlockSpec((None, 8, V7X_LANES), lambda b, j: (j, 0, 0))],
        out_specs=slab,
        out_shape=jax.ShapeDtypeStruct((B, S, W), F32),
        scratch_shapes=[pltpu.VMEM((2, 2 * BLOCK, 2 * BLOCK), F32)],
        compiler_params=_params("arbitrary", "arbitrary"),
        name="mixer_b_attention",
    )(q, k, v, sink_lanes)


def _head_rows(vec, n_heads, width):
    total = n_heads * width
    row = lax.broadcasted_iota(jnp.int32, (8, total), 0)
    lane = lax.broadcasted_iota(jnp.int32, (8, total), 1)
    return jnp.where(lane // width == row, jnp.broadcast_to(vec, (8, total)), 0.0)


def _pick_head_lanes(mat, n_heads, width):
    total = n_heads * width
    row = lax.broadcasted_iota(jnp.int32, (8, total), 0)
    lane = lax.broadcasted_iota(jnp.int32, (8, total), 1)
    return jnp.sum(jnp.where(lane // width == row, mat, 0.0), axis=0, keepdims=True)


def _as_column(row):
    return jnp.broadcast_to(row, (V7X_LANES, row.shape[1])).T


def _roll_in(dst_ref, cache, new_col):
    L = cache.shape[1]
    rolled = pltpu.roll(cache, L - 1, 1)
    lane = lax.broadcasted_iota(jnp.int32, (cache.shape[0], V7X_LANES), 1)
    if L > V7X_LANES:
        dst_ref[:, :L - V7X_LANES] = rolled[:, :L - V7X_LANES]
    dst_ref[:, L - V7X_LANES:] = jnp.where(lane == V7X_LANES - 1, new_col, rolled[:, L - V7X_LANES:])


def _sample_a_kernel(q_ref, kn_ref, vn_ref, kt_ref, vt_ref, w_ref, o_ref, okt_ref, ovt_ref):
    R, L = kt_ref.shape
    H = R // HEAD_DIM
    n_br = len(DILATIONS)
    qc = _as_column(q_ref[...] * (HEAD_DIM ** -0.5))
    knc = _as_column(kn_ref[...])
    vnc = _as_column(vn_ref[...])
    kt = kt_ref[...]
    vt = vt_ref[...]
    w = w_ref[...]
    prod = jnp.concatenate([kt[:, j:j + V7X_LANES] * qc for j in range(0, L, V7X_LANES)], axis=1)
    s = jnp.sum(prod.reshape(H, HEAD_DIM, L), axis=1)
    s_new = jnp.sum((qc * knc).reshape(H, HEAD_DIM, V7X_LANES), axis=1)[:, 0:1]
    s = jnp.where(w > 0, s, NEG)
    m = jnp.maximum(jnp.max(s, axis=-1, keepdims=True), s_new)
    p = w * jnp.exp(s - m)
    p_new = n_br * jnp.exp(s_new - m)
    l = jnp.sum(p, axis=-1, keepdims=True) + p_new

    def per_row(a):
        return jnp.broadcast_to(a.reshape(H, 1, a.shape[1]), (H, HEAD_DIM, a.shape[1])).reshape(R, a.shape[1])

    o_col = (jnp.sum(vt * per_row(p), axis=-1, keepdims=True) + per_row(p_new) * vnc[:, 0:1]) / per_row(l)
    o_ref[...] = jnp.broadcast_to(o_col, (R, V7X_LANES)).T[0:1, :]
    _roll_in(okt_ref, kt, knc)
    _roll_in(ovt_ref, vt, vnc)


def _branch_multiplicity(L):
    dist = L - np.arange(L)
    w = sum(((dist % d == 0) & (dist <= BLOCK * d)).astype(np.float32) for d in DILATIONS)
    return jnp.asarray(w.reshape(1, L))


def _sample_a(q, k_new, v_new, kt, vt):
    N, W, L = kt.shape
    assert L == A_WINDOW, "a cache shorter than the largest dilated window is unsupported"
    half = W // 2
    row = pl.BlockSpec((None, 1, half), lambda n, h: (n, 0, h))
    big = pl.BlockSpec((None, half, L), lambda n, h: (n, h, 0))
    return pl.pallas_call(
        _sample_a_kernel,
        grid=(N, 2),
        in_specs=[row, row, row, big, big, pl.BlockSpec((1, L), lambda n, h: (0, 0))],
        out_specs=[row, big, big],
        out_shape=[jax.ShapeDtypeStruct((N, 1, W), F32), jax.ShapeDtypeStruct((N, W, L), F32),
                   jax.ShapeDtypeStruct((N, W, L), F32)],
        compiler_params=_params("arbitrary", "arbitrary"),
        name="sample_mixer_a",
    )(q, k_new, v_new, kt, vt, _branch_multiplicity(L))


def _sample_b_kernel(q_ref, kn_ref, vn_ref, kt_ref, vt_ref, sink_ref, o_ref, okt_ref, ovt_ref):
    nb = q_ref.shape[0]
    lane = lax.broadcasted_iota(jnp.int32, (1, V7X_LANES), 1)
    lo = lane < HEAD_DIM
    for n in range(nb):
        q = q_ref[n:n + 1, :] * (HEAD_DIM ** -0.5)
        rows = []
        for j in range(B_W // V7X_LANES):
            pair = q[:, j * V7X_LANES:(j + 1) * V7X_LANES]
            rows += [jnp.where(lo, pair, 0.0), jnp.where(lo, 0.0, pair)]
        qh = jnp.concatenate(rows, axis=0)
        kt = kt_ref[n]
        vt = vt_ref[n]
        s_new = jnp.sum(qh * kn_ref[n:n + 1, :], axis=-1, keepdims=True)
        s = _dot(qh.astype(BF16), kt.astype(BF16))
        m = jnp.maximum(s_new, jnp.max(s, axis=-1, keepdims=True))
        p_new = jnp.exp(s_new - m)
        p = jnp.exp(s - m)
        l = p_new + jnp.sum(p, axis=-1, keepdims=True)
        o = p_new * vn_ref[n:n + 1, :] + _dot_t(p.astype(BF16), vt.astype(BF16))
        o = o / (l + jnp.exp(sink_ref[...] - m))
        o_ref[n:n + 1, :] = jnp.concatenate(
            [jnp.where(lo, o[2 * j:2 * j + 1], o[2 * j + 1:2 * j + 2]) for j in range(B_W // V7X_LANES)], axis=1)
        _roll_in(okt_ref.at[n], kt, _as_column(kn_ref[n:n + 1, :]))
        _roll_in(ovt_ref.at[n], vt, _as_column(vn_ref[n:n + 1, :]))


def _sample_b(q, k_new, v_new, kt, vt, sink_rows, *, nb):
    N, W, L = kt.shape
    assert L == B_WINDOW and W == V7X_LANES, "a cache shorter than the sliding window is unsupported"
    row = lambda w_: pl.BlockSpec((nb, w_), lambda i: (i, 0))
    cache = pl.BlockSpec((nb, W, L), lambda i: (i, 0, 0))
    return pl.pallas_call(
        _sample_b_kernel,
        grid=(N // nb,),
        in_specs=[row(B_W), row(W), row(W), cache, cache, pl.BlockSpec((8, 1), lambda i: (0, 0))],
        out_specs=[row(B_W), cache, cache],
        out_shape=[jax.ShapeDtypeStruct((N, B_W), F32), jax.ShapeDtypeStruct((N, W, L), F32),
                   jax.ShapeDtypeStruct((N, W, L), F32)],
        compiler_params=_params("arbitrary"),
        name="sample_mixer_b",
    )(q, k_new, v_new, kt, vt, sink_rows)


def _outproj_kernel(oa_ref, ob_ref, x_ref, ga_ref, gb_ref, w_ref, y_ref):
    ha = _rms(oa_ref[...], ga_ref[...]).astype(BF16)
    hb = _rms(ob_ref[...], gb_ref[...]).astype(BF16)
    y_ref[...] = x_ref[...] + _dot(ha, w_ref[:A_W, :]) + _dot(hb, w_ref[A_W:, :])


def _outproj(oa, ob, x, ga, gb, w, *, tm):
    T, D = x.shape
    tok = lambda w_: pl.BlockSpec((tm, w_), lambda i: (i, 0))
    const = lambda shape: pl.BlockSpec(shape, lambda i: (0,) * len(shape))
    return pl.pallas_call(
        _outproj_kernel,
        grid=(T // tm,),
        in_specs=[tok(A_W), tok(B_W), tok(D), const((1, A_W)), const((1, B_W)), const(w.shape)],
        out_specs=tok(D),
        out_shape=jax.ShapeDtypeStruct((T, D), F32),
        compiler_params=_params("arbitrary"),
        name="mixers_outproj",
    )(oa, ob, x, ga, gb, w)


def _norm_matmul_kernel(x_ref, g_ref, w_ref, *out_refs):
    h = _rms(x_ref[...], g_ref[...]).astype(BF16)
    width = out_refs[0].shape[1]
    for j, o_ref in enumerate(out_refs):
        o_ref[...] = _dot(h, w_ref[:, j * width:(j + 1) * width])


def _norm_matmul(x, g, w, *, n_out, tm):
    T, D = x.shape
    width = w.shape[1] // n_out
    return pl.pallas_call(
        _norm_matmul_kernel,
        grid=(T // tm,),
        in_specs=[pl.BlockSpec((tm, D), lambda i: (i, 0)), pl.BlockSpec((1, D), lambda i: (0, 0)),
                  pl.BlockSpec(w.shape, lambda i: (0, 0))],
        out_specs=[pl.BlockSpec((tm, width), lambda i: (i, 0))] * n_out,
        out_shape=[jax.ShapeDtypeStruct((T, width), F32)] * n_out,
        compiler_params=_params("arbitrary"),
        name="norm_matmul",
    )(x, g, w)


def _matmul_residual_kernel(a_ref, w_ref, r_ref, y_ref):
    y_ref[...] = r_ref[...] + _dot(a_ref[...].astype(BF16), w_ref[...])


def _matmul_residual(a, w, r, *, tm):
    T, K = a.shape
    D = w.shape[1]
    return pl.pallas_call(
        _matmul_residual_kernel,
        grid=(T // tm,),
        in_specs=[pl.BlockSpec((tm, K), lambda i: (i, 0)), pl.BlockSpec(w.shape, lambda i: (0, 0)),
                  pl.BlockSpec((tm, D), lambda i: (i, 0))],
        out_specs=pl.BlockSpec((tm, D), lambda i: (i, 0)),
        out_shape=jax.ShapeDtypeStruct((T, D), F32),
        compiler_params=_params("arbitrary"),
        name="matmul_residual",
    )(a, w, r)


def _cross_kernel(x_ref, g_ref, wq_ref, mk_ref, mv_ref, wo_ref, y_ref):
    x = x_ref[...]
    D = x.shape[1]
    hd = D // X_HEADS
    q = _dot(_rms(x, g_ref[...]).astype(BF16), wq_ref[...]) * (hd ** -0.5)
    y = x
    for h in range(X_HEADS):
        cols = slice(h * hd, (h + 1) * hd)
        s = _dot_t(q[:, cols].astype(BF16), mk_ref[:, cols].astype(BF16))
        p = jnp.exp(s - jnp.max(s, axis=-1, keepdims=True))
        o = _dot(p.astype(BF16), mv_ref[:, cols].astype(BF16)) / jnp.sum(p, axis=-1, keepdims=True)
        y = y + _dot(o.astype(BF16), wo_ref[cols, :])
    y_ref[...] = y


def _cross(x, g, wq, mk, mv, wo, *, tm):
    B, S, D = x.shape
    M = mk.shape[1]
    tok = pl.BlockSpec((None, tm, D), lambda b, i: (b, i, 0))
    mem = pl.BlockSpec((None, M, D), lambda b, i: (b, 0, 0))
    const = lambda shape: pl.BlockSpec(shape, lambda b, i: (0,) * len(shape))
    return pl.pallas_call(
        _cross_kernel,
        grid=(B, S // tm),
        in_specs=[tok, const((1, D)), const(wq.shape), mem, mem, const(wo.shape)],
        out_specs=tok,
        out_shape=jax.ShapeDtypeStruct((B, S, D), F32),
        compiler_params=_params("arbitrary", "arbitrary"),
        name="cross_attention",
    )(x, g, wq, mk, mv, wo)


def _sample_cross_kernel(q_ref, mk_ref, mv_ref, o_ref):
    nb, D = q_ref.shape
    hd = D // X_HEADS
    for n in range(nb):
        qh = _head_rows(q_ref[n:n + 1, :] * (hd ** -0.5), X_HEADS, hd).astype(BF16)
        s = _dot_t(qh, mk_ref[n].astype(BF16))
        p = jnp.exp(s - jnp.max(s, axis=-1, keepdims=True))
        o = _dot(p.astype(BF16), mv_ref[n].astype(BF16)) / jnp.sum(p, axis=-1, keepdims=True)
        o_ref[n:n + 1, :] = _pick_head_lanes(o, X_HEADS, hd)


def _sample_cross(q, mk, mv, *, nb):
    N, D = q.shape
    M = mk.shape[1]
    mem = pl.BlockSpec((nb, M, D), lambda i: (i, 0, 0))
    row = pl.BlockSpec((nb, D), lambda i: (i, 0))
    return pl.pallas_call(
        _sample_cross_kernel,
        grid=(N // nb,),
        in_specs=[row, mem, mem],
        out_specs=row,
        out_shape=jax.ShapeDtypeStruct((N, D), F32),
        compiler_params=_params("arbitrary"),
        name="sample_cross_attention",
    )(q, mk, mv)


FF_CHUNKS = ((0, 1024), (1024, 1024), (2048, 768))


def _ffn_chunk(h, wup_ref, wdn_ref, cw_ref, cb_ref, d_ff, c0, cw, prev_rows):
    gate = _dot(h, wup_ref[:, c0:c0 + cw])
    val = _dot(h, wup_ref[:, d_ff + c0:d_ff + c0 + cw])
    g2, g1 = prev_rows(gate)
    conv = cb_ref[:, c0:c0 + cw] + (g2 * cw_ref[0:1, c0:c0 + cw] + g1 * cw_ref[1:2, c0:c0 + cw]
                                     + gate * cw_ref[2:3, c0:c0 + cw])
    act = (conv * (1.0 / (1.0 + jnp.exp(-conv)))) * val
    return gate, _dot(act.astype(BF16), wdn_ref[c0:c0 + cw, :])


def _ffn_prompt_kernel(x_ref, g_ref, wup_ref, cw_ref, cb_ref, wdn_ref, gf_ref,
                       y_ref, state_ref, halo_ref, *, n_tiles):
    i = pl.program_id(1)
    tm = x_ref.shape[0]
    d_ff = wdn_ref.shape[0]

    @pl.when(i == 0)
    def _():
        halo_ref[...] = jnp.zeros_like(halo_ref)

    x = x_ref[...]
    h = _rms(x, g_ref[...]).astype(BF16)
    y = x
    for c0, cw in FF_CHUNKS:
        row = lax.broadcasted_iota(jnp.int32, (tm, cw), 0)
        halo = halo_ref[:, c0:c0 + cw]

        def prev_rows(gate):
            g1 = jnp.where(row == 0, halo[1:2], pltpu.roll(gate, 1, 0))
            g2 = jnp.where(row == 0, halo[0:1], jnp.where(row == 1, halo[1:2], pltpu.roll(gate, 2, 0)))
            return g2, g1

        gate, contrib = _ffn_chunk(h, wup_ref, wdn_ref, cw_ref, cb_ref, d_ff, c0, cw, prev_rows)
        y = y + contrib
        halo_ref[0:2, c0:c0 + cw] = gate[tm - 2:, :]

        @pl.when(i == n_tiles - 1)
        def _():
            state_ref[:, c0:c0 + cw] = gate[tm - 2:, :]

    y_ref[...] = _rms(y, gf_ref[...])


def _ffn_prompt(x, g, wup, cw, cb, wdn, gf, *, tm):
    B, S, D = x.shape
    d_ff = wdn.shape[0]
    assert FF_CHUNKS[-1][0] + FF_CHUNKS[-1][1] == d_ff
    n_tiles = S // tm
    tok = pl.BlockSpec((None, tm, D), lambda b, i: (b, i, 0))
    const = lambda shape: pl.BlockSpec(shape, lambda b, i: (0,) * len(shape))
    return pl.pallas_call(
        functools.partial(_ffn_prompt_kernel, n_tiles=n_tiles),
        grid=(B, n_tiles),
        in_specs=[tok, const((1, D)), const(wup.shape), const(cw.shape), const(cb.shape),
                  const(wdn.shape), const((1, D))],
        out_specs=[tok, pl.BlockSpec((None, CONV_W - 1, d_ff), lambda b, i: (b, 0, 0))],
        out_shape=[jax.ShapeDtypeStruct((B, S, D), F32),
                   jax.ShapeDtypeStruct((B, CONV_W - 1, d_ff), F32)],
        scratch_shapes=[pltpu.VMEM((8, d_ff), F32)],
        compiler_params=_params("arbitrary", "arbitrary"),
        name="conv_ffn_prompt",
    )(x, g, wup, cw, cb, wdn, gf)


def _ffn_sample_kernel(x_ref, s0_ref, s1_ref, g_ref, wup_ref, cw_ref, cb_ref, wdn_ref, gf_ref,
                       y_ref, gate_ref):
    d_ff = wdn_ref.shape[0]
    x = x_ref[...]
    h = _rms(x, g_ref[...]).astype(BF16)
    y = x
    for c0, cw in FF_CHUNKS:
        prev_rows = lambda gate: (s0_ref[:, c0:c0 + cw], s1_ref[:, c0:c0 + cw])
        gate, contrib = _ffn_chunk(h, wup_ref, wdn_ref, cw_ref, cb_ref, d_ff, c0, cw, prev_rows)
        y = y + contrib
        gate_ref[:, c0:c0 + cw] = gate
    y_ref[...] = _rms(y, gf_ref[...])


def _ffn_sample(x, s0, s1, g, wup, cw, cb, wdn, gf):
    N, D = x.shape
    d_ff = wdn.shape[0]
    return pl.pallas_call(
        _ffn_sample_kernel,
        out_shape=[jax.ShapeDtypeStruct((N, D), F32), jax.ShapeDtypeStruct((N, d_ff), F32)],
        compiler_params=pltpu.CompilerParams(vmem_limit_bytes=VMEM_LIMIT),
        name="conv_ffn_sample",
    )(x, s0, s1, g, wup, cw, cb, wdn, gf)


def _qb_pair_perm():
    cols = []
    for j in range(B_GROUP):
        for hk in range(B_KV_HEADS):
            h = hk * B_GROUP + j
            cols.extend(range(h * HEAD_DIM, (h + 1) * HEAD_DIM))
    return np.asarray(cols, dtype=np.int32)


def _layer_weights(l, g_mix, w_in, g_out_a, g_out_b, sinks, w_out, g_cross, g_mem, w_xq, w_mem_kv, w_xo,
                   g_ffn, w_up, conv_w, conv_b, w_down):
    perm = _qb_pair_perm()
    qb0 = 3 * A_W
    w_in_l = w_in[l]
    w_in_l = jnp.concatenate([w_in_l[:, :qb0], w_in_l[:, qb0:qb0 + B_W][:, perm], w_in_l[:, qb0 + B_W:]], axis=1)
    w_out_l = w_out[l]
    w_out_l = jnp.concatenate([w_out_l[:A_W], w_out_l[A_W:][perm]], axis=0)
    sink = sinks[l].astype(F32)
    pair = jnp.stack([sink[:B_GROUP], sink[B_GROUP:]], axis=1)
    sink_lanes = jnp.broadcast_to(jnp.repeat(pair, HEAD_DIM, axis=1)[:, None, :], (B_GROUP, 8, V7X_LANES))
    sink_rows = pair.reshape(2 * B_GROUP, 1)
    row = lambda v: v.reshape(1, -1).astype(F32)
    return dict(
        g_mix=row(g_mix[l]), w_in=w_in_l.astype(BF16),
        g_out_a=row(g_out_a[l]), g_out_b=row(g_out_b[l][perm]), w_out=w_out_l.astype(BF16),
        sink_lanes=sink_lanes, sink_rows=sink_rows,
        g_cross=row(g_cross[l]), g_mem=row(g_mem[l]), w_xq=w_xq[l].astype(BF16),
        w_mem_kv=w_mem_kv[l].astype(BF16), w_xo=w_xo[l].astype(BF16),
        g_ffn=row(g_ffn[l]), w_up=w_up[l].astype(BF16), conv_w=conv_w[l].astype(F32),
        conv_b=row(conv_b[l]), w_down=w_down[l].astype(BF16))


PROMPT_TM = 512
SAMPLE_NB = 8


def _prompt_layer(x, mem, W, g_final, cos, sin):
    B, S, D = x.shape
    M = mem.shape[1]
    tm = min(PROMPT_TM, S)
    qa, ka, va, qb, kb, vb, ka_t, va_t, kb_t, vb_t = _inproj(
        x, W["g_mix"], W["w_in"], cos, sin, tm=tm, with_tails=True)
    oa = _attn_a(qa, ka, va)
    ob = _attn_b(qb, kb, vb, W["sink_lanes"])
    x1 = _outproj(oa.reshape(B * S, A_W), ob.reshape(B * S, B_W), x.reshape(B * S, D),
                  W["g_out_a"], W["g_out_b"], W["w_out"], tm=tm)
    mk, mv = _norm_matmul(mem.reshape(B * M, D), W["g_mem"], W["w_mem_kv"], n_out=2, tm=min(512, B * M))
    mk = mk.reshape(B, M, D)
    mv = mv.reshape(B, M, D)
    x2 = _cross(x1.reshape(B, S, D), W["g_cross"], W["w_xq"], mk, mv, W["w_xo"], tm=tm)
    y, conv_state = _ffn_prompt(x2, W["g_ffn"], W["w_up"], W["conv_w"], W["conv_b"], W["w_down"], g_final, tm=tm)
    return y, (ka_t, va_t, kb_t, vb_t, mk, mv, conv_state)


def _sample_layer(x, a_kt, a_vt, b_kt, b_vt, mem_k, mem_v, conv_state, W, g_final, cos, sin):
    N, D = x.shape
    qa, ka, va, qb, kb, vb = [t[0] for t in _inproj(
        x[None], W["g_mix"], W["w_in"], cos, sin, tm=N, with_tails=False)]
    oa, s_akt, s_avt = _sample_a(qa[:, None, :], ka[:, None, :], va[:, None, :], a_kt, a_vt)
    ob, s_bkt, s_bvt = _sample_b(qb, kb, vb, b_kt, b_vt, W["sink_rows"], nb=SAMPLE_NB)
    x1 = _outproj(oa.reshape(N, A_W), ob, x, W["g_out_a"], W["g_out_b"], W["w_out"], tm=N)
    (q,) = _norm_matmul(x1, W["g_cross"], W["w_xq"], n_out=1, tm=N)
    o = _sample_cross(q, mem_k, mem_v, nb=SAMPLE_NB)
    x2 = _matmul_residual(o, W["w_xo"], x1, tm=N)
    y, gate = _ffn_sample(x2, conv_state[:, 0], conv_state[:, 1], W["g_ffn"], W["w_up"], W["conv_w"],
                          W["conv_b"], W["w_down"], g_final)
    new_conv = jnp.stack([conv_state[:, 1], gate], axis=1)
    return y, (s_akt, s_avt, s_bkt, s_bvt, new_conv)


def _time_minor(cache):
    N, L, H, Dh = cache.shape
    return cache.transpose(0, 2, 3, 1).reshape(N, H * Dh, L)


def _time_major(cache_t, H):
    N, W, L = cache_t.shape
    return cache_t.reshape(N, H, W // H, L).transpose(0, 3, 1, 2)


def kernel(x_prompt, x_sample, cache_a_k, cache_a_v, cache_b_k, cache_b_v, cache_mem_k, cache_mem_v, state_conv,
           mem_prompt, g_mix, w_in, g_out_a, g_out_b, sinks, w_out, g_cross, g_mem, w_xq, w_mem_kv, w_xo,
           g_ffn, w_up, conv_w, conv_b, w_down, g_final):
    depth = w_in.shape[0]
    assert depth == 1, "layer stacking is not wired up: the problem has a single layer"
    B, S, D = x_prompt.shape
    N, T, _ = x_sample.shape
    assert T == 1, "the sample group decodes one token per sequence"
    gf = g_final.reshape(1, D).astype(F32)
    cos_p, sin_p = _rope_tables(S, 0, 1)
    cos_s, sin_s = _rope_tables(N, PAST_LEN, 0)

    l = 0
    W = _layer_weights(l, g_mix, w_in, g_out_a, g_out_b, sinks, w_out, g_cross, g_mem, w_xq, w_mem_kv, w_xo,
                       g_ffn, w_up, conv_w, conv_b, w_down)
    M = cache_mem_k.shape[2]

    yp, (p_ak, p_av, p_bk, p_bv, p_mk, p_mv, p_conv) = _prompt_layer(x_prompt, mem_prompt, W, gf, cos_p, sin_p)
    ys, (s_akt, s_avt, s_bkt, s_bvt, s_conv) = _sample_layer(
        x_sample.reshape(N, D), _time_minor(cache_a_k[l]), _time_minor(cache_a_v[l]),
        _time_minor(cache_b_k[l]), _time_minor(cache_b_v[l]),
        cache_mem_k[l].reshape(N, M, D), cache_mem_v[l].reshape(N, M, D), state_conv[l], W, gf, cos_s, sin_s)

    la, lb = p_ak.shape[1], p_bk.shape[1]
    return (yp, ys.reshape(N, 1, D),
            p_ak.reshape(1, B, la, A_HEADS, HEAD_DIM), p_av.reshape(1, B, la, A_HEADS, HEAD_DIM),
            p_bk.reshape(1, B, lb, B_KV_HEADS, HEAD_DIM), p_bv.reshape(1, B, lb, B_KV_HEADS, HEAD_DIM),
            p_mk.reshape(1, B, -1, X_HEADS, D // X_HEADS), p_mv.reshape(1, B, -1, X_HEADS, D // X_HEADS),
            p_conv[None],
            _time_major(s_akt, A_HEADS)[None], _time_major(s_avt, A_HEADS)[None],
            _time_major(s_bkt, B_KV_HEADS)[None], _time_major(s_bvt, B_KV_HEADS)[None],
            s_conv[None])
```

```python
import functools

import jax
import jax.numpy as jnp
import numpy as np
from jax import lax
from jax.experimental import pallas as pl
from jax.experimental.pallas import tpu as pltpu

F32 = jnp.float32
BF16 = jnp.bfloat16

HEAD_DIM = 64
A_HEADS = 8
B_HEADS = 8
B_KV_HEADS = 2
B_GROUP = B_HEADS // B_KV_HEADS
A_W = A_HEADS * HEAD_DIM
B_W = B_HEADS * HEAD_DIM
B_KV_W = B_KV_HEADS * HEAD_DIM
DILATIONS = (1, 4, 16)
A_WINDOW = 2048
B_WINDOW = 128
BLOCK = 128
ROPE_THETA = 10000.0
PAST_LEN = 16384
X_HEADS = 4
CONV_W = 3
EPS = 1e-6
NEG = -1e30
LOG2E = 1.4426950408889634
ATTN_UNROLL = 8
R4 = 4

V7X_LANES = 128
V7X_MXU_DIM = 256
V7X_VMEM_BYTES = 64 * 1024 * 1024
VMEM_LIMIT = V7X_VMEM_BYTES - 8 * 1024 * 1024


def _params(*sem):
    return pltpu.CompilerParams(dimension_semantics=sem, vmem_limit_bytes=VMEM_LIMIT)


def _rms(x, g):
    return (x * lax.rsqrt(jnp.mean(x * x, axis=-1, keepdims=True) + EPS)) * g


def _dot(a, b):
    return jnp.dot(a, b, preferred_element_type=F32)


def _dot_t(a, b):
    return lax.dot_general(a, b, (((1,), (1,)), ((), ())), preferred_element_type=F32)


def _rope_table_kernel(inv_ref, cos_ref, sin_ref, *, pos0, pos_step):
    rows = cos_ref.shape[0]
    row = lax.broadcasted_iota(jnp.int32, (rows, V7X_LANES), 0)
    lane = lax.broadcasted_iota(jnp.int32, (rows, V7X_LANES), 1)
    ang = (pos0 + pos_step * row).astype(F32) * inv_ref[...]
    first_half = (lane % HEAD_DIM) < (HEAD_DIM // 2)
    cos_ref[...] = jnp.cos(ang)
    sin_ref[...] = jnp.where(first_half, -jnp.sin(ang), jnp.sin(ang))


def _rope_tables(rows, pos0, pos_step):
    half = HEAD_DIM // 2
    inv = jnp.power(ROPE_THETA, -jnp.arange(half, dtype=F32) / half)
    inv = jnp.tile(inv, V7X_LANES // half).reshape(1, V7X_LANES)
    return pl.pallas_call(
        functools.partial(_rope_table_kernel, pos0=pos0, pos_step=pos_step),
        out_shape=[jax.ShapeDtypeStruct((rows, V7X_LANES), F32)] * 2,
        name="rope_tables",
    )(inv)


def _rope(slab, cos, sin, first_half):
    partner = jnp.where(first_half, pltpu.roll(slab, 96, 1), pltpu.roll(slab, 32, 1))
    return slab * cos + partner * sin


def _inproj_kernel(x_ref, g_ref, w_ref, cos_ref, sin_ref,
                   qa_ref, ka_ref, va_ref, qb_ref, kb_ref, vb_ref, *rest,
                   tail_skip, n_tiles, prompt):
    tm = x_ref.shape[0]
    hn = _rms(x_ref[...], g_ref[...]).astype(BF16)
    cos = cos_ref[...]
    sin = sin_ref[...]
    lane = lax.broadcasted_iota(jnp.int32, (tm, V7X_LANES), 1)
    first_half = (lane % HEAD_DIM) < (HEAD_DIM // 2)

    def seg(c0, width):
        return _dot(hn, w_ref[:, c0:c0 + width])

    def roped(z):
        return jnp.concatenate(
            [_rope(z[:, c:c + V7X_LANES], cos, sin, first_half)
             for c in range(0, z.shape[1], V7X_LANES)], axis=1)

    if prompt:
        kat_ref, vat_ref, kbt_ref, vbt_ref, shuf_ref = rest

        def put_a(dst_ref, z):
            for c in range(A_W // V7X_LANES):
                shuf_ref[c] = z[:, c * V7X_LANES:(c + 1) * V7X_LANES]
            for c in range(A_W // V7X_LANES):
                for r in range(R4):
                    dst_ref[r, :, c * V7X_LANES:(c + 1) * V7X_LANES] = shuf_ref[c, pl.ds(r, tm // R4, stride=R4), :]
    else:
        def put_a(dst_ref, z):
            dst_ref[...] = z

    put_a(qa_ref, roped(seg(0, A_W)))
    ka = roped(seg(A_W, A_W))
    put_a(ka_ref, ka)
    va = seg(2 * A_W, A_W)
    put_a(va_ref, va)
    qb_ref[...] = roped(seg(3 * A_W, B_W))
    kvb = seg(3 * A_W + B_W, 2 * B_KV_W)
    kb = roped(kvb[:, :B_KV_W])
    vb = kvb[:, B_KV_W:]
    kb_ref[...] = kb
    vb_ref[...] = vb

    if prompt:
        i = pl.program_id(1)

        @pl.when(i >= tail_skip)
        def _():
            kat_ref[...] = ka
            vat_ref[...] = va

        @pl.when(i == n_tiles - 1)
        def _():
            kbt_ref[...] = kb[tm - B_WINDOW:, :]
            vbt_ref[...] = vb[tm - B_WINDOW:, :]


def _inproj(x, g, w, cos, sin, *, tm, prompt):
    B, S, D = x.shape
    n_tiles = S // tm
    la = min(A_WINDOW, S)
    lb = min(B_WINDOW, S)
    tail_skip = (S - la) // tm
    tok = lambda w_: pl.BlockSpec((None, tm, w_), lambda b, i: (b, i, 0))
    const = lambda shape: pl.BlockSpec(shape, lambda b, i: (0,) * len(shape))
    if prompt:
        a_shape = jax.ShapeDtypeStruct((B, R4, S // R4, A_W), F32)
        a_spec = pl.BlockSpec((None, R4, tm // R4, A_W), lambda b, i: (b, 0, i, 0))
    else:
        a_shape = jax.ShapeDtypeStruct((B, S, A_W), F32)
        a_spec = tok(A_W)
    out_shape = [a_shape] * 3 + [
        jax.ShapeDtypeStruct((B, S, B_W), F32),
        jax.ShapeDtypeStruct((B, S, B_KV_W), F32),
        jax.ShapeDtypeStruct((B, S, B_KV_W), F32)]
    out_specs = [a_spec] * 3 + [tok(B_W), tok(B_KV_W), tok(B_KV_W)]
    scratch = []
    if prompt:
        out_shape += [jax.ShapeDtypeStruct((B, la, A_W), F32)] * 2
        out_shape += [jax.ShapeDtypeStruct((B, lb, B_KV_W), F32)] * 2
        a_tail = pl.BlockSpec((None, tm, A_W), lambda b, i: (b, jnp.maximum(i - tail_skip, 0), 0))
        b_tail = pl.BlockSpec((None, lb, B_KV_W), lambda b, i: (b, 0, 0))
        out_specs += [a_tail, a_tail, b_tail, b_tail]
        scratch = [pltpu.VMEM((A_W // V7X_LANES, tm, V7X_LANES), F32)]
    return pl.pallas_call(
        functools.partial(_inproj_kernel, tail_skip=tail_skip, n_tiles=n_tiles, prompt=prompt),
        grid=(B, n_tiles),
        in_specs=[tok(D), const((1, D)), const(w.shape),
                  pl.BlockSpec((tm, V7X_LANES), lambda b, i: (i, 0)),
                  pl.BlockSpec((tm, V7X_LANES), lambda b, i: (i, 0))],
        out_specs=out_specs,
        out_shape=out_shape,
        scratch_shapes=scratch,
        compiler_params=_params("arbitrary", "arbitrary"),
        name="inproj_rope",
    )(x, g, w, cos, sin)


def _band_bias(first, chunked):
    a = lax.broadcasted_iota(jnp.int32, (2 * BLOCK, 2 * BLOCK), 0) % BLOCK
    b = lax.broadcasted_iota(jnp.int32, (2 * BLOCK, 2 * BLOCK), 1)
    own = b >= BLOCK
    bb = b % BLOCK
    if chunked:
        sub = BLOCK // R4
        a = R4 * (a % sub) + a // sub
        bb = R4 * (bb % sub) + bb // sub
    dist = BLOCK + a - (bb + jnp.where(own, BLOCK, 0))
    valid = (dist >= 0) & (dist <= BLOCK)
    if first:
        valid = valid & own
    return jnp.where(valid, 0.0, NEG).astype(F32)


def _two_head_block(q, kk, vv, bias):
    lane = lax.broadcasted_iota(jnp.int32, (BLOCK, V7X_LANES), 1)
    lo = lane < HEAD_DIM
    q = q * (HEAD_DIM ** -0.5 * LOG2E)
    qs = jnp.concatenate([jnp.where(lo, q, 0.0), jnp.where(lo, 0.0, q)], axis=0).astype(BF16)
    s = _dot_t(qs, kk) + bias
    m = jnp.max(s, axis=-1, keepdims=True)
    p = jnp.exp2(s - m)
    v1 = jnp.concatenate([vv, jnp.ones_like(vv)], axis=1)
    pv = _dot(p.astype(BF16), v1)
    acc = jnp.where(lo, pv[:BLOCK, :V7X_LANES], pv[BLOCK:, :V7X_LANES])
    l2 = jnp.where(lo, pv[:BLOCK, V7X_LANES:], pv[BLOCK:, V7X_LANES:])
    m2 = jnp.where(lo, m[:BLOCK], m[BLOCK:])
    return acc, m2, l2


def _get(ref, slices):
    return jnp.concatenate([ref[sl, :] for sl in slices], axis=0) if len(slices) > 1 else ref[slices[0], :]


def _put(ref, slices, val):
    off = 0
    for sl in slices:
        ref[sl, :] = val[off:off + sl.size]
        off += sl.size


def _block_slices(layout, d, S, i):
    if layout == "seq":
        start = pl.multiple_of(i * BLOCK, BLOCK)
        prev = pl.multiple_of(jnp.maximum(i - 1, 0) * BLOCK, BLOCK)
        return [pl.ds(start, BLOCK)], [pl.ds(prev, BLOCK)], i == 0
    Sr = S // R4
    if d == 1:
        sub = BLOCK // R4
        pj = jnp.maximum(i - 1, 0)
        own = [pl.ds(pl.multiple_of(r * Sr + i * sub, sub), sub) for r in range(R4)]
        prev = [pl.ds(pl.multiple_of(r * Sr + pj * sub, sub), sub) for r in range(R4)]
        return own, prev, i == 0
    if d == R4:
        per_res = Sr // BLOCK
        jb = i % per_res
        start = pl.multiple_of(i * BLOCK, BLOCK)
        prev = pl.multiple_of(jnp.where(jb == 0, i, i - 1) * BLOCK, BLOCK)
        return [pl.ds(start, BLOCK)], [pl.ds(prev, BLOCK)], jb == 0
    step = d // R4
    per_res = S // (BLOCK * d)
    rd = i // per_res
    jb = i % per_res
    start = (rd % R4) * Sr + rd // R4 + jb * (BLOCK * step)
    prev = jnp.where(jb == 0, start, start - BLOCK * step)
    return [pl.ds(start, BLOCK, stride=step)], [pl.ds(prev, BLOCK, stride=step)], jb == 0


def _branch_blocks(q_ref, k_ref, v_ref, bias_ref, layout, d, visit):
    S = q_ref.shape[0]
    bias_base = 2 if (layout == "r4" and d == 1) else 0

    def body(i, carry):
        own, prev, first = _block_slices(layout, d, S, i)
        kk = jnp.concatenate([_get(k_ref, prev), _get(k_ref, own)], axis=0).astype(BF16)
        vv = jnp.concatenate([_get(v_ref, prev), _get(v_ref, own)], axis=0).astype(BF16)
        bias = bias_ref[bias_base + jnp.where(first, 1, 0)]
        acc, m, l = _two_head_block(_get(q_ref, own), kk, vv, bias)
        visit(own, acc, m, l)
        return carry

    lax.fori_loop(0, S // BLOCK, body, 0, unroll=ATTN_UNROLL)


def _init_bias(bias_ref, chunked_too):
    bias_ref[0] = _band_bias(False, False)
    bias_ref[1] = _band_bias(True, False)
    if chunked_too:
        bias_ref[2] = _band_bias(False, True)
        bias_ref[3] = _band_bias(True, True)


def _attn_a_kernel(q_ref, k_ref, v_ref, o_ref, m_ref, l_ref, bias_ref):
    _init_bias(bias_ref, True)

    def first_visit(rows, acc, m, l):
        _put(o_ref, rows, acc)
        _put(m_ref, rows, m)
        _put(l_ref, rows, l)

    def merged(rows, acc, m, l):
        m_old = _get(m_ref, rows)
        m_new = jnp.maximum(m_old, m)
        a_old = jnp.exp2(m_old - m_new)
        a_new = jnp.exp2(m - m_new)
        return (a_old * _get(o_ref, rows) + a_new * acc, m_new, a_old * _get(l_ref, rows) + a_new * l)

    def mid_visit(rows, acc, m, l):
        acc, m, l = merged(rows, acc, m, l)
        _put(o_ref, rows, acc)
        _put(m_ref, rows, m)
        _put(l_ref, rows, l)

    def last_visit(rows, acc, m, l):
        acc, m, l = merged(rows, acc, m, l)
        _put(o_ref, rows, acc / l)

    order = (R4, 1, 16)
    assert sorted(order) == sorted(DILATIONS)
    visits = [first_visit] + [mid_visit] * (len(order) - 2) + [last_visit]
    for d, visit in zip(order, visits):
        _branch_blocks(q_ref, k_ref, v_ref, bias_ref, "r4", d, visit)


def _attn_b_kernel(q_ref, k_ref, v_ref, sink_ref, o_ref, bias_ref):
    _init_bias(bias_ref, False)
    sink = sink_ref[0:1, :] * LOG2E

    def visit(rows, acc, m, l):
        _put(o_ref, rows, acc / (l + jnp.exp2(sink - m)))

    _branch_blocks(q_ref, k_ref, v_ref, bias_ref, "seq", 1, visit)


def _attn_a(q, k, v):
    B, S, W = q.shape
    slab = pl.BlockSpec((None, S, V7X_LANES), lambda b, j: (b, 0, j))
    return pl.pallas_call(
        _attn_a_kernel,
        grid=(B, W // V7X_LANES),
        in_specs=[slab, slab, slab],
        out_specs=slab,
        out_shape=jax.ShapeDtypeStruct((B, S, W), F32),
        scratch_shapes=[pltpu.VMEM((S, V7X_LANES), F32), pltpu.VMEM((S, V7X_LANES), F32),
                        pltpu.VMEM((4, 2 * BLOCK, 2 * BLOCK), F32)],
        compiler_params=_params("arbitrary", "arbitrary"),
        name="mixer_a_attention",
    )(q, k, v)


def _attn_b(q, k, v, sink_lanes):
    B, S, W = q.shape
    slab = pl.BlockSpec((None, S, V7X_LANES), lambda b, j: (b, 0, j))
    kv = pl.BlockSpec((None, S, V7X_LANES), lambda b, j: (b, 0, 0))
    return pl.pallas_call(
        _attn_b_kernel,
        grid=(B, W // V7X_LANES),
        in_specs=[slab, kv, kv, pl.BlockSpec((None, 8, V7X_LANES), lambda b, j: (j, 0, 0))],
        out_specs=slab,
        out_shape=jax.ShapeDtypeStruct((B, S, W), F32),
        scratch_shapes=[pltpu.VMEM((2, 2 * BLOCK, 2 * BLOCK), F32)],
        compiler_params=_params("arbitrary", "arbitrary"),
        name="mixer_b_attention",
    )(q, k, v, sink_lanes)


def _as_column(row):
    return jnp.broadcast_to(row, (V7X_LANES, row.shape[1])).T


def _roll_in(dst_ref, cache, new_col):
    L = cache.shape[1]
    rolled = pltpu.roll(cache, L - 1, 1)
    lane = lax.broadcasted_iota(jnp.int32, (cache.shape[0], V7X_LANES), 1)
    if L > V7X_LANES:
        dst_ref[:, :L - V7X_LANES] = rolled[:, :L - V7X_LANES]
    dst_ref[:, L - V7X_LANES:] = jnp.where(lane == V7X_LANES - 1, new_col, rolled[:, L - V7X_LANES:])


def _sample_a_kernel(q_ref, kn_ref, vn_ref, kt_ref, vt_ref, w_ref, o_ref, okt_ref, ovt_ref):
    R, L = kt_ref.shape
    H = R // HEAD_DIM
    n_br = len(DILATIONS)
    qc = _as_column(q_ref[...] * (HEAD_DIM ** -0.5))
    knc = _as_column(kn_ref[...])
    vnc = _as_column(vn_ref[...])
    kt = kt_ref[...]
    vt = vt_ref[...]
    w = w_ref[...]
    prod = jnp.concatenate([kt[:, j:j + V7X_LANES] * qc for j in range(0, L, V7X_LANES)], axis=1)
    s = jnp.sum(prod.reshape(H, HEAD_DIM, L), axis=1)
    s_new = jnp.sum((qc * knc).reshape(H, HEAD_DIM, V7X_LANES), axis=1)[:, 0:1]
    s = jnp.where(w > 0, s, NEG)
    m = jnp.maximum(jnp.max(s, axis=-1, keepdims=True), s_new)
    p = w * jnp.exp(s - m)
    p_new = n_br * jnp.exp(s_new - m)
    l = jnp.sum(p, axis=-1, keepdims=True) + p_new

    def per_row(a):
        return jnp.broadcast_to(a.reshape(H, 1, a.shape[1]), (H, HEAD_DIM, a.shape[1])).reshape(R, a.shape[1])

    o_col = (jnp.sum(vt * per_row(p), axis=-1, keepdims=True) + per_row(p_new) * vnc[:, 0:1]) / per_row(l)
    o_ref[...] = jnp.broadcast_to(o_col, (R, V7X_LANES)).T[0:1, :]
    _roll_in(okt_ref, kt, knc)
    _roll_in(ovt_ref, vt, vnc)


def _branch_multiplicity(L):
    dist = L - np.arange(L)
    w = sum(((dist % d == 0) & (dist <= BLOCK * d)).astype(np.float32) for d in DILATIONS)
    return jnp.asarray(w.reshape(1, L))


def _sample_a(q, k_new, v_new, kt, vt):
    N, W, L = kt.shape
    assert L == A_WINDOW, "a cache shorter than the largest dilated window is unsupported"
    half = W // 2
    row = pl.BlockSpec((None, 1, half), lambda n, h: (n, 0, h))
    big = pl.BlockSpec((None, half, L), lambda n, h: (n, h, 0))
    return pl.pallas_call(
        _sample_a_kernel,
        grid=(N, 2),
        in_specs=[row, row, row, big, big, pl.BlockSpec((1, L), lambda n, h: (0, 0))],
        out_specs=[row, big, big],
        out_shape=[jax.ShapeDtypeStruct((N, 1, W), F32), jax.ShapeDtypeStruct((N, W, L), F32),
                   jax.ShapeDtypeStruct((N, W, L), F32)],
        compiler_params=_params("arbitrary", "arbitrary"),
        name="sample_mixer_a",
    )(q, k_new, v_new, kt, vt, _branch_multiplicity(L))


def _sample_b_kernel(q_ref, kn_ref, vn_ref, kt_ref, vt_ref, sink_ref, o_ref, okt_ref, ovt_ref):
    nb = q_ref.shape[0]
    lane = lax.broadcasted_iota(jnp.int32, (1, V7X_LANES), 1)
    lo = lane < HEAD_DIM
    for n in range(nb):
        q = q_ref[n:n + 1, :] * (HEAD_DIM ** -0.5)
        rows = []
        for j in range(B_W // V7X_LANES):
            pair = q[:, j * V7X_LANES:(j + 1) * V7X_LANES]
            rows += [jnp.where(lo, pair, 0.0), jnp.where(lo, 0.0, pair)]
        qh = jnp.concatenate(rows, axis=0)
        kt = kt_ref[n]
        vt = vt_ref[n]
        s_new = jnp.sum(qh * kn_ref[n:n + 1, :], axis=-1, keepdims=True)
        s = _dot(qh.astype(BF16), kt.astype(BF16))
        m = jnp.maximum(s_new, jnp.max(s, axis=-1, keepdims=True))
        p_new = jnp.exp(s_new - m)
        p = jnp.exp(s - m)
        l = p_new + jnp.sum(p, axis=-1, keepdims=True)
        o = p_new * vn_ref[n:n + 1, :] + _dot_t(p.astype(BF16), vt.astype(BF16))
        o = o / (l + jnp.exp(sink_ref[...] - m))
        o_ref[n:n + 1, :] = jnp.concatenate(
            [jnp.where(lo, o[2 * j:2 * j + 1], o[2 * j + 1:2 * j + 2]) for j in range(B_W // V7X_LANES)], axis=1)
        _roll_in(okt_ref.at[n], kt, _as_column(kn_ref[n:n + 1, :]))
        _roll_in(ovt_ref.at[n], vt, _as_column(vn_ref[n:n + 1, :]))


def _sample_b(q, k_new, v_new, kt, vt, sink_rows, *, nb):
    N, W, L = kt.shape
    assert L == B_WINDOW and W == V7X_LANES, "a cache shorter than the sliding window is unsupported"
    row = lambda w_: pl.BlockSpec((nb, w_), lambda i: (i, 0))
    cache = pl.BlockSpec((nb, W, L), lambda i: (i, 0, 0))
    return pl.pallas_call(
        _sample_b_kernel,
        grid=(N // nb,),
        in_specs=[row(B_W), row(W), row(W), cache, cache, pl.BlockSpec((8, 1), lambda i: (0, 0))],
        out_specs=[row(B_W), cache, cache],
        out_shape=[jax.ShapeDtypeStruct((N, B_W), F32), jax.ShapeDtypeStruct((N, W, L), F32),
                   jax.ShapeDtypeStruct((N, W, L), F32)],
        compiler_params=_params("arbitrary"),
        name="sample_mixer_b",
    )(q, k_new, v_new, kt, vt, sink_rows)


def _outproj_kernel(oa_ref, ob_ref, x_ref, ga_ref, gb_ref, w_ref, y_ref, *scratch, oa_residue_major):
    if oa_residue_major:
        (shuf_ref,) = scratch
        tq = oa_ref.shape[1]
        for c in range(A_W // V7X_LANES):
            for r in range(R4):
                shuf_ref[c, pl.ds(r, tq, stride=R4), :] = oa_ref[r, :, c * V7X_LANES:(c + 1) * V7X_LANES]
        oa = jnp.concatenate([shuf_ref[c] for c in range(A_W // V7X_LANES)], axis=1)
    else:
        oa = oa_ref[...]
    ha = _rms(oa, ga_ref[...]).astype(BF16)
    hb = _rms(ob_ref[...], gb_ref[...]).astype(BF16)
    y_ref[...] = x_ref[...] + _dot(ha, w_ref[:A_W, :]) + _dot(hb, w_ref[A_W:, :])


def _outproj(oa, ob, x, ga, gb, w, *, tm):
    B, S, D = x.shape
    residue_major = oa.ndim == 4
    tok = lambda w_: pl.BlockSpec((None, tm, w_), lambda b, i: (b, i, 0))
    const = lambda shape: pl.BlockSpec(shape, lambda b, i: (0,) * len(shape))
    if residue_major:
        oa_spec = pl.BlockSpec((None, R4, tm // R4, A_W), lambda b, i: (b, 0, i, 0))
        scratch = [pltpu.VMEM((A_W // V7X_LANES, tm, V7X_LANES), F32)]
    else:
        oa_spec, scratch = tok(A_W), []
    return pl.pallas_call(
        functools.partial(_outproj_kernel, oa_residue_major=residue_major),
        grid=(B, S // tm),
        in_specs=[oa_spec, tok(B_W), tok(D), const((1, A_W)), const((1, B_W)), const(w.shape)],
        out_specs=tok(D),
        out_shape=jax.ShapeDtypeStruct((B, S, D), F32),
        scratch_shapes=scratch,
        compiler_params=_params("arbitrary", "arbitrary"),
        name="mixers_outproj",
    )(oa, ob, x, ga, gb, w)


def _norm_matmul_kernel(x_ref, g_ref, w_ref, *out_refs):
    h = _rms(x_ref[...], g_ref[...]).astype(BF16)
    width = out_refs[0].shape[1]
    for j, o_ref in enumerate(out_refs):
        o_ref[...] = _dot(h, w_ref[:, j * width:(j + 1) * width])


def _norm_matmul(x, g, w, *, n_out, tm):
    T, D = x.shape
    width = w.shape[1] // n_out
    return pl.pallas_call(
        _norm_matmul_kernel,
        grid=(T // tm,),
        in_specs=[pl.BlockSpec((tm, D), lambda i: (i, 0)), pl.BlockSpec((1, D), lambda i: (0, 0)),
                  pl.BlockSpec(w.shape, lambda i: (0, 0))],
        out_specs=[pl.BlockSpec((tm, width), lambda i: (i, 0))] * n_out,
        out_shape=[jax.ShapeDtypeStruct((T, width), F32)] * n_out,
        compiler_params=_params("arbitrary"),
        name="norm_matmul",
    )(x, g, w)


def _matmul_residual_kernel(a_ref, w_ref, r_ref, y_ref):
    y_ref[...] = r_ref[...] + _dot(a_ref[...].astype(BF16), w_ref[...])


def _matmul_residual(a, w, r, *, tm):
    T, K = a.shape
    D = w.shape[1]
    return pl.pallas_call(
        _matmul_residual_kernel,
        grid=(T // tm,),
        in_specs=[pl.BlockSpec((tm, K), lambda i: (i, 0)), pl.BlockSpec(w.shape, lambda i: (0, 0)),
                  pl.BlockSpec((tm, D), lambda i: (i, 0))],
        out_specs=pl.BlockSpec((tm, D), lambda i: (i, 0)),
        out_shape=jax.ShapeDtypeStruct((T, D), F32),
        compiler_params=_params("arbitrary"),
        name="matmul_residual",
    )(a, w, r)


def _cross_kernel(x_ref, g_ref, wq_ref, mk_ref, mv_ref, wo_ref, y_ref):
    x = x_ref[...]
    D = x.shape[1]
    hd = D // X_HEADS
    q = _dot(_rms(x, g_ref[...]).astype(BF16), wq_ref[...]) * (hd ** -0.5)
    y = x
    for h in range(X_HEADS):
        cols = slice(h * hd, (h + 1) * hd)
        s = _dot_t(q[:, cols].astype(BF16), mk_ref[:, cols].astype(BF16))
        p = jnp.exp(s - jnp.max(s, axis=-1, keepdims=True))
        o = _dot(p.astype(BF16), mv_ref[:, cols].astype(BF16)) / jnp.sum(p, axis=-1, keepdims=True)
        y = y + _dot(o.astype(BF16), wo_ref[cols, :])
    y_ref[...] = y


def _cross(x, g, wq, mk, mv, wo, *, tm):
    B, S, D = x.shape
    M = mk.shape[1]
    tok = pl.BlockSpec((None, tm, D), lambda b, i: (b, i, 0))
    mem = pl.BlockSpec((None, M, D), lambda b, i: (b, 0, 0))
    const = lambda shape: pl.BlockSpec(shape, lambda b, i: (0,) * len(shape))
    return pl.pallas_call(
        _cross_kernel,
        grid=(B, S // tm),
        in_specs=[tok, const((1, D)), const(wq.shape), mem, mem, const(wo.shape)],
        out_specs=tok,
        out_shape=jax.ShapeDtypeStruct((B, S, D), F32),
        compiler_params=_params("arbitrary", "arbitrary"),
        name="cross_attention",
    )(x, g, wq, mk, mv, wo)


MEM_TILE_ROWS = 8


def _sample_cross_kernel(q_ref, mk_ref, mv_ref, o_ref):
    nb = q_ref.shape[0]
    M = mk_ref.shape[1] // MEM_TILE_ROWS
    for n in range(nb):
        q = q_ref[n]
        k3 = mk_ref[n].reshape(M, MEM_TILE_ROWS, V7X_LANES)
        v3 = mv_ref[n].reshape(M, MEM_TILE_ROWS, V7X_LANES)
        part = jnp.sum(k3 * q[None], axis=-1, keepdims=True)
        s = part + pltpu.roll(part, X_HEADS, 1)
        p = jnp.exp(s - jnp.max(s, axis=0, keepdims=True))
        o = jnp.sum(v3 * p, axis=0) / jnp.sum(p, axis=0)
        o_ref[n] = o


def _sample_cross(q, mk, mv, *, nb):
    N, R, _ = mk.shape
    mem = pl.BlockSpec((nb, R, V7X_LANES), lambda i: (i, 0, 0))
    row = pl.BlockSpec((nb, MEM_TILE_ROWS, V7X_LANES), lambda i: (i, 0, 0))
    return pl.pallas_call(
        _sample_cross_kernel,
        grid=(N // nb,),
        in_specs=[row, mem, mem],
        out_specs=row,
        out_shape=jax.ShapeDtypeStruct((N, MEM_TILE_ROWS, V7X_LANES), F32),
        compiler_params=_params("arbitrary"),
        name="sample_cross_attention",
    )(q, mk, mv)


def _mem_tile_rows(cache):
    N, M, H, Dh = cache.shape
    halves = Dh // V7X_LANES
    assert H * halves == MEM_TILE_ROWS
    return cache.reshape(N, M, H, halves, V7X_LANES).transpose(0, 1, 3, 2, 4).reshape(N, M * MEM_TILE_ROWS, V7X_LANES)


FF_CHUNKS = ((0, 1024), (1024, 1024), (2048, 768))


def _ffn_chunk(h, wup_ref, wdn_ref, cw_ref, cb_ref, d_ff, c0, cw, prev_rows):
    gate = _dot(h, wup_ref[:, c0:c0 + cw])
    val = _dot(h, wup_ref[:, d_ff + c0:d_ff + c0 + cw])
    g2, g1 = prev_rows(gate)
    conv = cb_ref[:, c0:c0 + cw] + (g2 * cw_ref[0:1, c0:c0 + cw] + g1 * cw_ref[1:2, c0:c0 + cw]
                                     + gate * cw_ref[2:3, c0:c0 + cw])
    act = (conv * (1.0 / (1.0 + jnp.exp(-conv)))) * val
    return gate, _dot(act.astype(BF16), wdn_ref[c0:c0 + cw, :])


def _ffn_prompt_kernel(x_ref, g_ref, wup_ref, cw_ref, cb_ref, wdn_ref, gf_ref,
                       y_ref, state_ref, halo_ref, *, n_tiles):
    i = pl.program_id(1)
    tm = x_ref.shape[0]
    d_ff = wdn_ref.shape[0]

    @pl.when(i == 0)
    def _():
        halo_ref[...] = jnp.zeros_like(halo_ref)

    x = x_ref[...]
    h = _rms(x, g_ref[...]).astype(BF16)
    y = x
    for c0, cw in FF_CHUNKS:
        row = lax.broadcasted_iota(jnp.int32, (tm, cw), 0)
        halo = halo_ref[:, c0:c0 + cw]

        def prev_rows(gate):
            g1 = jnp.where(row == 0, halo[1:2], pltpu.roll(gate, 1, 0))
            g2 = jnp.where(row == 0, halo[0:1], jnp.where(row == 1, halo[1:2], pltpu.roll(gate, 2, 0)))
            return g2, g1

        gate, contrib = _ffn_chunk(h, wup_ref, wdn_ref, cw_ref, cb_ref, d_ff, c0, cw, prev_rows)
        y = y + contrib
        halo_ref[0:2, c0:c0 + cw] = gate[tm - 2:, :]

        @pl.when(i == n_tiles - 1)
        def _():
            state_ref[:, c0:c0 + cw] = gate[tm - 2:, :]

    y_ref[...] = _rms(y, gf_ref[...])


def _ffn_prompt(x, g, wup, cw, cb, wdn, gf, *, tm):
    B, S, D = x.shape
    d_ff = wdn.shape[0]
    assert FF_CHUNKS[-1][0] + FF_CHUNKS[-1][1] == d_ff
    n_tiles = S // tm
    tok = pl.BlockSpec((None, tm, D), lambda b, i: (b, i, 0))
    const = lambda shape: pl.BlockSpec(shape, lambda b, i: (0,) * len(shape))
    return pl.pallas_call(
        functools.partial(_ffn_prompt_kernel, n_tiles=n_tiles),
        grid=(B, n_tiles),
        in_specs=[tok, const((1, D)), const(wup.shape), const(cw.shape), const(cb.shape),
                  const(wdn.shape), const((1, D))],
        out_specs=[tok, pl.BlockSpec((None, CONV_W - 1, d_ff), lambda b, i: (b, 0, 0))],
        out_shape=[jax.ShapeDtypeStruct((B, S, D), F32),
                   jax.ShapeDtypeStruct((B, CONV_W - 1, d_ff), F32)],
        scratch_shapes=[pltpu.VMEM((8, d_ff), F32)],
        compiler_params=_params("arbitrary", "arbitrary"),
        name="conv_ffn_prompt",
    )(x, g, wup, cw, cb, wdn, gf)


def _ffn_sample_kernel(x_ref, s0_ref, s1_ref, g_ref, wup_ref, cw_ref, cb_ref, wdn_ref, gf_ref,
                       y_ref, gate_ref):
    d_ff = wdn_ref.shape[0]
    x = x_ref[...]
    h = _rms(x, g_ref[...]).astype(BF16)
    y = x
    for c0, cw in FF_CHUNKS:
        prev_rows = lambda gate: (s0_ref[:, c0:c0 + cw], s1_ref[:, c0:c0 + cw])
        gate, contrib = _ffn_chunk(h, wup_ref, wdn_ref, cw_ref, cb_ref, d_ff, c0, cw, prev_rows)
        y = y + contrib
        gate_ref[:, c0:c0 + cw] = gate
    y_ref[...] = _rms(y, gf_ref[...])


def _ffn_sample(x, s0, s1, g, wup, cw, cb, wdn, gf):
    N, D = x.shape
    d_ff = wdn.shape[0]
    return pl.pallas_call(
        _ffn_sample_kernel,
        out_shape=[jax.ShapeDtypeStruct((N, D), F32), jax.ShapeDtypeStruct((N, d_ff), F32)],
        compiler_params=pltpu.CompilerParams(vmem_limit_bytes=VMEM_LIMIT),
        name="conv_ffn_sample",
    )(x, s0, s1, g, wup, cw, cb, wdn, gf)


def _qb_pair_perm():
    cols = []
    for j in range(B_GROUP):
        for hk in range(B_KV_HEADS):
            h = hk * B_GROUP + j
            cols.extend(range(h * HEAD_DIM, (h + 1) * HEAD_DIM))
    return np.asarray(cols, dtype=np.int32)


def _layer_weights(l, g_mix, w_in, g_out_a, g_out_b, sinks, w_out, g_cross, g_mem, w_xq, w_mem_kv, w_xo,
                   g_ffn, w_up, conv_w, conv_b, w_down):
    perm = _qb_pair_perm()
    qb0 = 3 * A_W
    w_in_l = w_in[l]
    w_in_l = jnp.concatenate([w_in_l[:, :qb0], w_in_l[:, qb0:qb0 + B_W][:, perm], w_in_l[:, qb0 + B_W:]], axis=1)
    w_out_l = w_out[l]
    w_out_l = jnp.concatenate([w_out_l[:A_W], w_out_l[A_W:][perm]], axis=0)
    sink = sinks[l].astype(F32)
    pair = jnp.stack([sink[:B_GROUP], sink[B_GROUP:]], axis=1)
    sink_lanes = jnp.broadcast_to(jnp.repeat(pair, HEAD_DIM, axis=1)[:, None, :], (B_GROUP, 8, V7X_LANES))
    sink_rows = pair.reshape(2 * B_GROUP, 1)
    row = lambda v: v.reshape(1, -1).astype(F32)
    return dict(
        g_mix=row(g_mix[l]), w_in=w_in_l.astype(BF16),
        g_out_a=row(g_out_a[l]), g_out_b=row(g_out_b[l][perm]), w_out=w_out_l.astype(BF16),
        sink_lanes=sink_lanes, sink_rows=sink_rows,
        g_cross=row(g_cross[l]), g_mem=row(g_mem[l]), w_xq=w_xq[l].astype(BF16),
        w_mem_kv=w_mem_kv[l].astype(BF16), w_xo=w_xo[l].astype(BF16),
        g_ffn=row(g_ffn[l]), w_up=w_up[l].astype(BF16), conv_w=conv_w[l].astype(F32),
        conv_b=row(conv_b[l]), w_down=w_down[l].astype(BF16))


PROMPT_TM = 512
SAMPLE_NB = 8


def _prompt_layer(x, mem, W, g_final, cos, sin):
    B, S, D = x.shape
    M = mem.shape[1]
    tm = min(PROMPT_TM, S)
    qa, ka, va, qb, kb, vb, ka_t, va_t, kb_t, vb_t = _inproj(
        x, W["g_mix"], W["w_in"], cos, sin, tm=tm, prompt=True)
    flat = lambda t: t.reshape(B, S, A_W)
    oa = _attn_a(flat(qa), flat(ka), flat(va)).reshape(B, R4, S // R4, A_W)
    ob = _attn_b(qb, kb, vb, W["sink_lanes"])
    x1 = _outproj(oa, ob, x, W["g_out_a"], W["g_out_b"], W["w_out"], tm=tm)
    mk, mv = _norm_matmul(mem.reshape(B * M, D), W["g_mem"], W["w_mem_kv"], n_out=2, tm=min(512, B * M))
    mk = mk.reshape(B, M, D)
    mv = mv.reshape(B, M, D)
    x2 = _cross(x1, W["g_cross"], W["w_xq"], mk, mv, W["w_xo"], tm=tm)
    y, conv_state = _ffn_prompt(x2, W["g_ffn"], W["w_up"], W["conv_w"], W["conv_b"], W["w_down"], g_final, tm=tm)
    return y, (ka_t, va_t, kb_t, vb_t, mk, mv, conv_state)


def _sample_layer(x, a_kt, a_vt, b_kt, b_vt, mem_k, mem_v, conv_state, W, g_final, cos, sin):
    N, D = x.shape
    qa, ka, va, qb, kb, vb = [t[0] for t in _inproj(
        x[None], W["g_mix"], W["w_in"], cos, sin, tm=N, prompt=False)]
    oa, s_akt, s_avt = _sample_a(qa[:, None, :], ka[:, None, :], va[:, None, :], a_kt, a_vt)
    ob, s_bkt, s_bvt = _sample_b(qb, kb, vb, b_kt, b_vt, W["sink_rows"], nb=SAMPLE_NB)
    x1 = _outproj(oa.reshape(1, N, A_W), ob[None], x[None], W["g_out_a"], W["g_out_b"], W["w_out"], tm=N)[0]
    (q,) = _norm_matmul(x1, W["g_cross"], W["w_xq"], n_out=1, tm=N)
    hd = D // X_HEADS
    halves = hd // V7X_LANES
    q = (q * (hd ** -0.5)).reshape(N, X_HEADS, halves, V7X_LANES).transpose(0, 2, 1, 3)
    o = _sample_cross(q.reshape(N, MEM_TILE_ROWS, V7X_LANES), mem_k, mem_v, nb=SAMPLE_NB)
    o = o.reshape(N, halves, X_HEADS, V7X_LANES).transpose(0, 2, 1, 3).reshape(N, D)
    x2 = _matmul_residual(o, W["w_xo"], x1, tm=N)
    y, gate = _ffn_sample(x2, conv_state[:, 0], conv_state[:, 1], W["g_ffn"], W["w_up"], W["conv_w"],
                          W["conv_b"], W["w_down"], g_final)
    new_conv = jnp.stack([conv_state[:, 1], gate], axis=1)
    return y, (s_akt, s_avt, s_bkt, s_bvt, new_conv)


def _time_minor(cache):
    N, L, H, Dh = cache.shape
    return cache.transpose(0, 2, 3, 1).reshape(N, H * Dh, L)


def _time_major(cache_t, H):
    N, W, L = cache_t.shape
    return cache_t.reshape(N, H, W // H, L).transpose(0, 3, 1, 2)


def kernel(x_prompt, x_sample, cache_a_k, cache_a_v, cache_b_k, cache_b_v, cache_mem_k, cache_mem_v, state_conv,
           mem_prompt, g_mix, w_in, g_out_a, g_out_b, sinks, w_out, g_cross, g_mem, w_xq, w_mem_kv, w_xo,
           g_ffn, w_up, conv_w, conv_b, w_down, g_final):
    depth = w_in.shape[0]
    assert depth == 1, "layer stacking is not wired up: the problem has a single layer"
    B, S, D = x_prompt.shape
    N, T, _ = x_sample.shape
    assert T == 1, "the sample group decodes one token per sequence"
    gf = g_final.reshape(1, D).astype(F32)
    cos_p, sin_p = _rope_tables(S, 0, 1)
    cos_s, sin_s = _rope_tables(N, PAST_LEN, 0)

    l = 0
    W = _layer_weights(l, g_mix, w_in, g_out_a, g_out_b, sinks, w_out, g_cross, g_mem, w_xq, w_mem_kv, w_xo,
                       g_ffn, w_up, conv_w, conv_b, w_down)

    yp, (p_ak, p_av, p_bk, p_bv, p_mk, p_mv, p_conv) = _prompt_layer(x_prompt, mem_prompt, W, gf, cos_p, sin_p)
    ys, (s_akt, s_avt, s_bkt, s_bvt, s_conv) = _sample_layer(
        x_sample.reshape(N, D), _time_minor(cache_a_k[l]), _time_minor(cache_a_v[l]),
        _time_minor(cache_b_k[l]), _time_minor(cache_b_v[l]),
        _mem_tile_rows(cache_mem_k[l]), _mem_tile_rows(cache_mem_v[l]), state_conv[l], W, gf, cos_s, sin_s)

    la, lb = p_ak.shape[1], p_bk.shape[1]
    return (yp, ys.reshape(N, 1, D),
            p_ak.reshape(1, B, la, A_HEADS, HEAD_DIM), p_av.reshape(1, B, la, A_HEADS, HEAD_DIM),
            p_bk.reshape(1, B, lb, B_KV_HEADS, HEAD_DIM), p_bv.reshape(1, B, lb, B_KV_HEADS, HEAD_DIM),
            p_mk.reshape(1, B, -1, X_HEADS, D // X_HEADS), p_mv.reshape(1, B, -1, X_HEADS, D // X_HEADS),
            p_conv[None],
            _time_major(s_akt, A_HEADS)[None], _time_major(s_avt, A_HEADS)[None],
            _time_major(s_bkt, B_KV_HEADS)[None], _time_major(s_bvt, B_KV_HEADS)[None],
            s_conv[None])
```

```python
import functools

import jax
import jax.numpy as jnp
import numpy as np
from jax import lax
from jax.experimental import pallas as pl
from jax.experimental.pallas import tpu as pltpu

F32 = jnp.float32
BF16 = jnp.bfloat16

HEAD_DIM = 64
A_HEADS = 8
B_HEADS = 8
B_KV_HEADS = 2
B_GROUP = B_HEADS // B_KV_HEADS
A_W = A_HEADS * HEAD_DIM
B_W = B_HEADS * HEAD_DIM
B_KV_W = B_KV_HEADS * HEAD_DIM
DILATIONS = (1, 4, 16)
A_WINDOW = 2048
B_WINDOW = 128
BLOCK = 128
ROPE_THETA = 10000.0
PAST_LEN = 16384
X_HEADS = 4
CONV_W = 3
EPS = 1e-6
NEG = -1e30
LOG2E = 1.4426950408889634
ATTN_UNROLL = 8
R4 = 4

V7X_LANES = 128
V7X_MXU_DIM = 256
V7X_VMEM_BYTES = 64 * 1024 * 1024
VMEM_LIMIT = V7X_VMEM_BYTES - 8 * 1024 * 1024


def _params(*sem):
    return pltpu.CompilerParams(dimension_semantics=sem, vmem_limit_bytes=VMEM_LIMIT)


def _rms(x, g):
    return (x * lax.rsqrt(jnp.mean(x * x, axis=-1, keepdims=True) + EPS)) * g


def _dot(a, b):
    return jnp.dot(a, b, preferred_element_type=F32)


def _dot_t(a, b):
    return lax.dot_general(a, b, (((1,), (1,)), ((), ())), preferred_element_type=F32)


def _rope_table_kernel(inv_ref, cos_ref, sin_ref, *, pos0, pos_step):
    rows = cos_ref.shape[0]
    row = lax.broadcasted_iota(jnp.int32, (rows, V7X_LANES), 0)
    lane = lax.broadcasted_iota(jnp.int32, (rows, V7X_LANES), 1)
    ang = (pos0 + pos_step * row).astype(F32) * inv_ref[...]
    first_half = (lane % HEAD_DIM) < (HEAD_DIM // 2)
    cos_ref[...] = jnp.cos(ang)
    sin_ref[...] = jnp.where(first_half, -jnp.sin(ang), jnp.sin(ang))


def _rope_tables(rows, pos0, pos_step):
    half = HEAD_DIM // 2
    inv = jnp.power(ROPE_THETA, -jnp.arange(half, dtype=F32) / half)
    inv = jnp.tile(inv, V7X_LANES // half).reshape(1, V7X_LANES)
    return pl.pallas_call(
        functools.partial(_rope_table_kernel, pos0=pos0, pos_step=pos_step),
        out_shape=[jax.ShapeDtypeStruct((rows, V7X_LANES), F32)] * 2,
        name="rope_tables",
    )(inv)


def _rope(slab, cos, sin, first_half):
    partner = jnp.where(first_half, pltpu.roll(slab, 96, 1), pltpu.roll(slab, 32, 1))
    return slab * cos + partner * sin


def _inproj_kernel(x_ref, g_ref, w_ref, cos_ref, sin_ref,
                   qa_ref, ka_ref, va_ref, qb_ref, kb_ref, vb_ref, *rest,
                   tail_skip, n_tiles, prompt):
    tm = x_ref.shape[0]
    hn = _rms(x_ref[...], g_ref[...]).astype(BF16)
    cos = cos_ref[...]
    sin = sin_ref[...]
    lane = lax.broadcasted_iota(jnp.int32, (tm, V7X_LANES), 1)
    first_half = (lane % HEAD_DIM) < (HEAD_DIM // 2)

    def seg(c0, width):
        return _dot(hn, w_ref[:, c0:c0 + width])

    def roped(z):
        return jnp.concatenate(
            [_rope(z[:, c:c + V7X_LANES], cos, sin, first_half)
             for c in range(0, z.shape[1], V7X_LANES)], axis=1)

    if prompt:
        kat_ref, vat_ref, kbt_ref, vbt_ref, shuf_ref = rest

        def put_a(dst_ref, z):
            for c in range(A_W // V7X_LANES):
                shuf_ref[c] = z[:, c * V7X_LANES:(c + 1) * V7X_LANES]
            for c in range(A_W // V7X_LANES):
                for r in range(R4):
                    dst_ref[r, :, c * V7X_LANES:(c + 1) * V7X_LANES] = shuf_ref[c, pl.ds(r, tm // R4, stride=R4), :]
    else:
        def put_a(dst_ref, z):
            dst_ref[...] = z

    put_a(qa_ref, roped(seg(0, A_W)))
    ka = roped(seg(A_W, A_W))
    put_a(ka_ref, ka)
    va = seg(2 * A_W, A_W)
    put_a(va_ref, va)
    qb_ref[...] = roped(seg(3 * A_W, B_W))
    kvb = seg(3 * A_W + B_W, 2 * B_KV_W)
    kb = roped(kvb[:, :B_KV_W])
    vb = kvb[:, B_KV_W:]
    kb_ref[...] = kb
    vb_ref[...] = vb

    if prompt:
        i = pl.program_id(1)

        @pl.when(i >= tail_skip)
        def _():
            kat_ref[...] = ka
            vat_ref[...] = va

        @pl.when(i == n_tiles - 1)
        def _():
            kbt_ref[...] = kb[tm - B_WINDOW:, :]
            vbt_ref[...] = vb[tm - B_WINDOW:, :]


def _inproj(x, g, w, cos, sin, *, tm, prompt):
    B, S, D = x.shape
    n_tiles = S // tm
    la = min(A_WINDOW, S)
    lb = min(B_WINDOW, S)
    tail_skip = (S - la) // tm
    tok = lambda w_: pl.BlockSpec((None, tm, w_), lambda b, i: (b, i, 0))
    const = lambda shape: pl.BlockSpec(shape, lambda b, i: (0,) * len(shape))
    if prompt:
        a_shape = jax.ShapeDtypeStruct((B, R4, S // R4, A_W), F32)
        a_spec = pl.BlockSpec((None, R4, tm // R4, A_W), lambda b, i: (b, 0, i, 0))
    else:
        a_shape = jax.ShapeDtypeStruct((B, S, A_W), F32)
        a_spec = tok(A_W)
    out_shape = [a_shape] * 3 + [
        jax.ShapeDtypeStruct((B, S, B_W), F32),
        jax.ShapeDtypeStruct((B, S, B_KV_W), F32),
        jax.ShapeDtypeStruct((B, S, B_KV_W), F32)]
    out_specs = [a_spec] * 3 + [tok(B_W), tok(B_KV_W), tok(B_KV_W)]
    scratch = []
    if prompt:
        out_shape += [jax.ShapeDtypeStruct((B, la, A_W), F32)] * 2
        out_shape += [jax.ShapeDtypeStruct((B, lb, B_KV_W), F32)] * 2
        a_tail = pl.BlockSpec((None, tm, A_W), lambda b, i: (b, jnp.maximum(i - tail_skip, 0), 0))
        b_tail = pl.BlockSpec((None, lb, B_KV_W), lambda b, i: (b, 0, 0))
        out_specs += [a_tail, a_tail, b_tail, b_tail]
        scratch = [pltpu.VMEM((A_W // V7X_LANES, tm, V7X_LANES), F32)]
    return pl.pallas_call(
        functools.partial(_inproj_kernel, tail_skip=tail_skip, n_tiles=n_tiles, prompt=prompt),
        grid=(B, n_tiles),
        in_specs=[tok(D), const((1, D)), const(w.shape),
                  pl.BlockSpec((tm, V7X_LANES), lambda b, i: (i, 0)),
                  pl.BlockSpec((tm, V7X_LANES), lambda b, i: (i, 0))],
        out_specs=out_specs,
        out_shape=out_shape,
        scratch_shapes=scratch,
        compiler_params=_params("arbitrary", "arbitrary"),
        name="inproj_rope",
    )(x, g, w, cos, sin)


def _band_bias(first, chunked):
    a = lax.broadcasted_iota(jnp.int32, (2 * BLOCK, 2 * BLOCK), 0) % BLOCK
    b = lax.broadcasted_iota(jnp.int32, (2 * BLOCK, 2 * BLOCK), 1)
    own = b >= BLOCK
    bb = b % BLOCK
    if chunked:
        sub = BLOCK // R4
        a = R4 * (a % sub) + a // sub
        bb = R4 * (bb % sub) + bb // sub
    dist = BLOCK + a - (bb + jnp.where(own, BLOCK, 0))
    valid = (dist >= 0) & (dist <= BLOCK)
    if first:
        valid = valid & own
    return jnp.where(valid, 0.0, NEG).astype(F32)


def _two_head_block(q, kk, vv, bias):
    lane = lax.broadcasted_iota(jnp.int32, (BLOCK, V7X_LANES), 1)
    lo = lane < HEAD_DIM
    q = q * (HEAD_DIM ** -0.5 * LOG2E)
    qs = jnp.concatenate([jnp.where(lo, q, 0.0), jnp.where(lo, 0.0, q)], axis=0).astype(BF16)
    s = _dot_t(qs, kk) + bias
    m = jnp.max(s, axis=-1, keepdims=True)
    p = jnp.exp2(s - m)
    v1 = jnp.concatenate([vv, jnp.ones_like(vv)], axis=1)
    pv = _dot(p.astype(BF16), v1)
    acc = jnp.where(lo, pv[:BLOCK, :V7X_LANES], pv[BLOCK:, :V7X_LANES])
    l2 = jnp.where(lo, pv[:BLOCK, V7X_LANES:], pv[BLOCK:, V7X_LANES:])
    m2 = jnp.where(lo, m[:BLOCK], m[BLOCK:])
    return acc, m2, l2


def _get(ref, slices):
    return jnp.concatenate([ref[sl, :] for sl in slices], axis=0) if len(slices) > 1 else ref[slices[0], :]


def _put(ref, slices, val):
    off = 0
    for sl in slices:
        ref[sl, :] = val[off:off + sl.size]
        off += sl.size


def _block_slices(layout, d, S, i):
    if layout == "seq":
        start = pl.multiple_of(i * BLOCK, BLOCK)
        prev = pl.multiple_of(jnp.maximum(i - 1, 0) * BLOCK, BLOCK)
        return [pl.ds(start, BLOCK)], [pl.ds(prev, BLOCK)], i == 0
    Sr = S // R4
    if d == 1:
        sub = BLOCK // R4
        pj = jnp.maximum(i - 1, 0)
        own = [pl.ds(pl.multiple_of(r * Sr + i * sub, sub), sub) for r in range(R4)]
        prev = [pl.ds(pl.multiple_of(r * Sr + pj * sub, sub), sub) for r in range(R4)]
        return own, prev, i == 0
    if d == R4:
        per_res = Sr // BLOCK
        jb = i % per_res
        start = pl.multiple_of(i * BLOCK, BLOCK)
        prev = pl.multiple_of(jnp.where(jb == 0, i, i - 1) * BLOCK, BLOCK)
        return [pl.ds(start, BLOCK)], [pl.ds(prev, BLOCK)], jb == 0
    step = d // R4
    per_res = S // (BLOCK * d)
    rd = i // per_res
    jb = i % per_res
    start = (rd % R4) * Sr + rd // R4 + jb * (BLOCK * step)
    prev = jnp.where(jb == 0, start, start - BLOCK * step)
    return [pl.ds(start, BLOCK, stride=step)], [pl.ds(prev, BLOCK, stride=step)], jb == 0


def _branch_blocks(q_ref, k_ref, v_ref, bias_ref, layout, d, visit):
    S = q_ref.shape[0]
    bias_base = 2 if (layout == "r4" and d == 1) else 0

    def body(i, carry):
        own, prev, first = _block_slices(layout, d, S, i)
        kk = jnp.concatenate([_get(k_ref, prev), _get(k_ref, own)], axis=0).astype(BF16)
        vv = jnp.concatenate([_get(v_ref, prev), _get(v_ref, own)], axis=0).astype(BF16)
        bias = bias_ref[bias_base + jnp.where(first, 1, 0)]
        acc, m, l = _two_head_block(_get(q_ref, own), kk, vv, bias)
        visit(own, acc, m, l)
        return carry

    lax.fori_loop(0, S // BLOCK, body, 0, unroll=ATTN_UNROLL)


def _init_bias(bias_ref, chunked_too):
    bias_ref[0] = _band_bias(False, False)
    bias_ref[1] = _band_bias(True, False)
    if chunked_too:
        bias_ref[2] = _band_bias(False, True)
        bias_ref[3] = _band_bias(True, True)


def _attn_a_kernel(q_ref, k_ref, v_ref, o_ref, m_ref, l_ref, bias_ref):
    _init_bias(bias_ref, True)

    def first_visit(rows, acc, m, l):
        _put(o_ref, rows, acc)
        _put(m_ref, rows, m)
        _put(l_ref, rows, l)

    def merged(rows, acc, m, l):
        m_old = _get(m_ref, rows)
        m_new = jnp.maximum(m_old, m)
        a_old = jnp.exp2(m_old - m_new)
        a_new = jnp.exp2(m - m_new)
        return (a_old * _get(o_ref, rows) + a_new * acc, m_new, a_old * _get(l_ref, rows) + a_new * l)

    def mid_visit(rows, acc, m, l):
        acc, m, l = merged(rows, acc, m, l)
        _put(o_ref, rows, acc)
        _put(m_ref, rows, m)
        _put(l_ref, rows, l)

    def last_visit(rows, acc, m, l):
        acc, m, l = merged(rows, acc, m, l)
        _put(o_ref, rows, acc / l)

    order = (R4, 1, 16)
    assert sorted(order) == sorted(DILATIONS)
    visits = [first_visit] + [mid_visit] * (len(order) - 2) + [last_visit]
    for d, visit in zip(order, visits):
        _branch_blocks(q_ref, k_ref, v_ref, bias_ref, "r4", d, visit)


def _attn_b_kernel(q_ref, k_ref, v_ref, sink_ref, o_ref, bias_ref):
    _init_bias(bias_ref, False)
    sink = sink_ref[0:1, :] * LOG2E

    def visit(rows, acc, m, l):
        _put(o_ref, rows, acc / (l + jnp.exp2(sink - m)))

    _branch_blocks(q_ref, k_ref, v_ref, bias_ref, "seq", 1, visit)


def _attn_a(q, k, v):
    B, S, W = q.shape
    slab = pl.BlockSpec((None, S, V7X_LANES), lambda b, j: (b, 0, j))
    return pl.pallas_call(
        _attn_a_kernel,
        grid=(B, W // V7X_LANES),
        in_specs=[slab, slab, slab],
        out_specs=slab,
        out_shape=jax.ShapeDtypeStruct((B, S, W), F32),
        scratch_shapes=[pltpu.VMEM((S, V7X_LANES), F32), pltpu.VMEM((S, V7X_LANES), F32),
                        pltpu.VMEM((4, 2 * BLOCK, 2 * BLOCK), F32)],
        compiler_params=_params("arbitrary", "arbitrary"),
        name="mixer_a_attention",
    )(q, k, v)


def _attn_b(q, k, v, sink_lanes):
    B, S, W = q.shape
    slab = pl.BlockSpec((None, S, V7X_LANES), lambda b, j: (b, 0, j))
    kv = pl.BlockSpec((None, S, V7X_LANES), lambda b, j: (b, 0, 0))
    return pl.pallas_call(
        _attn_b_kernel,
        grid=(B, W // V7X_LANES),
        in_specs=[slab, kv, kv, pl.BlockSpec((None, 8, V7X_LANES), lambda b, j: (j, 0, 0))],
        out_specs=slab,
        out_shape=jax.ShapeDtypeStruct((B, S, W), F32),
        scratch_shapes=[pltpu.VMEM((2, 2 * BLOCK, 2 * BLOCK), F32)],
        compiler_params=_params("arbitrary", "arbitrary"),
        name="mixer_b_attention",
    )(q, k, v, sink_lanes)


def _as_column(row):
    return jnp.broadcast_to(row, (V7X_LANES, row.shape[1])).T


def _roll_in(dst_ref, cache, new_col):
    L = cache.shape[1]
    rolled = pltpu.roll(cache, L - 1, 1)
    lane = lax.broadcasted_iota(jnp.int32, (cache.shape[0], V7X_LANES), 1)
    if L > V7X_LANES:
        dst_ref[:, :L - V7X_LANES] = rolled[:, :L - V7X_LANES]
    dst_ref[:, L - V7X_LANES:] = jnp.where(lane == V7X_LANES - 1, new_col, rolled[:, L - V7X_LANES:])


def _sample_a_unit(q, k_new, v_new, kt, vt, w, okt_ref, ovt_ref):
    R, L = kt.shape
    H = R // HEAD_DIM
    n_br = len(DILATIONS)
    qc = _as_column(q * (HEAD_DIM ** -0.5))
    knc = _as_column(k_new)
    vnc = _as_column(v_new)
    prod = jnp.concatenate([kt[:, j:j + V7X_LANES] * qc for j in range(0, L, V7X_LANES)], axis=1)
    s = jnp.sum(prod.reshape(H, HEAD_DIM, L), axis=1)
    s_new = jnp.sum((qc * knc).reshape(H, HEAD_DIM, V7X_LANES), axis=1)[:, 0:1]
    s = jnp.where(w > 0, s, NEG)
    m = jnp.maximum(jnp.max(s, axis=-1, keepdims=True), s_new)
    p = w * jnp.exp(s - m)
    p_new = n_br * jnp.exp(s_new - m)
    l = jnp.sum(p, axis=-1, keepdims=True) + p_new

    def per_row(a):
        return jnp.broadcast_to(a.reshape(H, 1, a.shape[1]), (H, HEAD_DIM, a.shape[1])).reshape(R, a.shape[1])

    o_col = (jnp.sum(vt * per_row(p), axis=-1, keepdims=True) + per_row(p_new) * vnc[:, 0:1]) / per_row(l)
    _roll_in(okt_ref, kt, knc)
    _roll_in(ovt_ref, vt, vnc)
    return jnp.broadcast_to(o_col, (R, V7X_LANES)).T[0:1, :]


def _branch_multiplicity(L):
    dist = L - np.arange(L)
    w = sum(((dist % d == 0) & (dist <= BLOCK * d)).astype(np.float32) for d in DILATIONS)
    return jnp.asarray(w.reshape(1, L))


def _sample_b_kernel(q_ref, kn_ref, vn_ref, kt_ref, vt_ref, sink_ref, o_ref, okt_ref, ovt_ref):
    nb = q_ref.shape[0]
    lane = lax.broadcasted_iota(jnp.int32, (1, V7X_LANES), 1)
    lo = lane < HEAD_DIM
    for n in range(nb):
        q = q_ref[n:n + 1, :] * (HEAD_DIM ** -0.5)
        rows = []
        for j in range(B_W // V7X_LANES):
            pair = q[:, j * V7X_LANES:(j + 1) * V7X_LANES]
            rows += [jnp.where(lo, pair, 0.0), jnp.where(lo, 0.0, pair)]
        qh = jnp.concatenate(rows, axis=0)
        kt = kt_ref[n]
        vt = vt_ref[n]
        s_new = jnp.sum(qh * kn_ref[n:n + 1, :], axis=-1, keepdims=True)
        s = _dot(qh.astype(BF16), kt.astype(BF16))
        m = jnp.maximum(s_new, jnp.max(s, axis=-1, keepdims=True))
        p_new = jnp.exp(s_new - m)
        p = jnp.exp(s - m)
        l = p_new + jnp.sum(p, axis=-1, keepdims=True)
        o = p_new * vn_ref[n:n + 1, :] + _dot_t(p.astype(BF16), vt.astype(BF16))
        o = o / (l + jnp.exp(sink_ref[...] - m))
        o_ref[n:n + 1, :] = jnp.concatenate(
            [jnp.where(lo, o[2 * j:2 * j + 1], o[2 * j + 1:2 * j + 2]) for j in range(B_W // V7X_LANES)], axis=1)
        _roll_in(okt_ref.at[n], kt, _as_column(kn_ref[n:n + 1, :]))
        _roll_in(ovt_ref.at[n], vt, _as_column(vn_ref[n:n + 1, :]))


def _sample_b(q, k_new, v_new, kt, vt, sink_rows, *, nb):
    N, W, L = kt.shape
    assert L == B_WINDOW and W == V7X_LANES, "a cache shorter than the sliding window is unsupported"
    row = lambda w_: pl.BlockSpec((nb, w_), lambda i: (i, 0))
    cache = pl.BlockSpec((nb, W, L), lambda i: (i, 0, 0))
    return pl.pallas_call(
        _sample_b_kernel,
        grid=(N // nb,),
        in_specs=[row(B_W), row(W), row(W), cache, cache, pl.BlockSpec((8, 1), lambda i: (0, 0))],
        out_specs=[row(B_W), cache, cache],
        out_shape=[jax.ShapeDtypeStruct((N, B_W), F32), jax.ShapeDtypeStruct((N, W, L), F32),
                   jax.ShapeDtypeStruct((N, W, L), F32)],
        compiler_params=_params("arbitrary"),
        name="sample_mixer_b",
    )(q, k_new, v_new, kt, vt, sink_rows)


def _outproj_kernel(oa_ref, ob_ref, x_ref, ga_ref, gb_ref, w_ref, y_ref, *scratch, oa_residue_major):
    if oa_residue_major:
        (shuf_ref,) = scratch
        tq = oa_ref.shape[1]
        for c in range(A_W // V7X_LANES):
            for r in range(R4):
                shuf_ref[c, pl.ds(r, tq, stride=R4), :] = oa_ref[r, :, c * V7X_LANES:(c + 1) * V7X_LANES]
        oa = jnp.concatenate([shuf_ref[c] for c in range(A_W // V7X_LANES)], axis=1)
    else:
        oa = oa_ref[...]
    ha = _rms(oa, ga_ref[...]).astype(BF16)
    hb = _rms(ob_ref[...], gb_ref[...]).astype(BF16)
    y_ref[...] = x_ref[...] + _dot(ha, w_ref[:A_W, :]) + _dot(hb, w_ref[A_W:, :])


def _outproj(oa, ob, x, ga, gb, w, *, tm):
    B, S, D = x.shape
    residue_major = oa.ndim == 4
    tok = lambda w_: pl.BlockSpec((None, tm, w_), lambda b, i: (b, i, 0))
    const = lambda shape: pl.BlockSpec(shape, lambda b, i: (0,) * len(shape))
    if residue_major:
        oa_spec = pl.BlockSpec((None, R4, tm // R4, A_W), lambda b, i: (b, 0, i, 0))
        scratch = [pltpu.VMEM((A_W // V7X_LANES, tm, V7X_LANES), F32)]
    else:
        oa_spec, scratch = tok(A_W), []
    return pl.pallas_call(
        functools.partial(_outproj_kernel, oa_residue_major=residue_major),
        grid=(B, S // tm),
        in_specs=[oa_spec, tok(B_W), tok(D), const((1, A_W)), const((1, B_W)), const(w.shape)],
        out_specs=tok(D),
        out_shape=jax.ShapeDtypeStruct((B, S, D), F32),
        scratch_shapes=scratch,
        compiler_params=_params("arbitrary", "arbitrary"),
        name="mixers_outproj",
    )(oa, ob, x, ga, gb, w)


def _norm_matmul_kernel(x_ref, g_ref, w_ref, *out_refs):
    h = _rms(x_ref[...], g_ref[...]).astype(BF16)
    width = out_refs[0].shape[1]
    for j, o_ref in enumerate(out_refs):
        o_ref[...] = _dot(h, w_ref[:, j * width:(j + 1) * width])


def _norm_matmul(x, g, w, *, n_out, tm):
    T, D = x.shape
    width = w.shape[1] // n_out
    return pl.pallas_call(
        _norm_matmul_kernel,
        grid=(T // tm,),
        in_specs=[pl.BlockSpec((tm, D), lambda i: (i, 0)), pl.BlockSpec((1, D), lambda i: (0, 0)),
                  pl.BlockSpec(w.shape, lambda i: (0, 0))],
        out_specs=[pl.BlockSpec((tm, width), lambda i: (i, 0))] * n_out,
        out_shape=[jax.ShapeDtypeStruct((T, width), F32)] * n_out,
        compiler_params=_params("arbitrary"),
        name="norm_matmul",
    )(x, g, w)


def _matmul_residual_kernel(a_ref, w_ref, r_ref, y_ref):
    y_ref[...] = r_ref[...] + _dot(a_ref[...].astype(BF16), w_ref[...])


def _matmul_residual(a, w, r, *, tm):
    T, K = a.shape
    D = w.shape[1]
    return pl.pallas_call(
        _matmul_residual_kernel,
        grid=(T // tm,),
        in_specs=[pl.BlockSpec((tm, K), lambda i: (i, 0)), pl.BlockSpec(w.shape, lambda i: (0, 0)),
                  pl.BlockSpec((tm, D), lambda i: (i, 0))],
        out_specs=pl.BlockSpec((tm, D), lambda i: (i, 0)),
        out_shape=jax.ShapeDtypeStruct((T, D), F32),
        compiler_params=_params("arbitrary"),
        name="matmul_residual",
    )(a, w, r)


def _cross_kernel(x_ref, g_ref, wq_ref, mk_ref, mv_ref, wo_ref, y_ref):
    x = x_ref[...]
    D = x.shape[1]
    hd = D // X_HEADS
    q = _dot(_rms(x, g_ref[...]).astype(BF16), wq_ref[...]) * (hd ** -0.5)
    y = x
    for h in range(X_HEADS):
        cols = slice(h * hd, (h + 1) * hd)
        s = _dot_t(q[:, cols].astype(BF16), mk_ref[:, cols].astype(BF16))
        p = jnp.exp(s - jnp.max(s, axis=-1, keepdims=True))
        o = _dot(p.astype(BF16), mv_ref[:, cols].astype(BF16)) / jnp.sum(p, axis=-1, keepdims=True)
        y = y + _dot(o.astype(BF16), wo_ref[cols, :])
    y_ref[...] = y


def _cross(x, g, wq, mk, mv, wo, *, tm):
    B, S, D = x.shape
    M = mk.shape[1]
    tok = pl.BlockSpec((None, tm, D), lambda b, i: (b, i, 0))
    mem = pl.BlockSpec((None, M, D), lambda b, i: (b, 0, 0))
    const = lambda shape: pl.BlockSpec(shape, lambda b, i: (0,) * len(shape))
    return pl.pallas_call(
        _cross_kernel,
        grid=(B, S // tm),
        in_specs=[tok, const((1, D)), const(wq.shape), mem, mem, const(wo.shape)],
        out_specs=tok,
        out_shape=jax.ShapeDtypeStruct((B, S, D), F32),
        compiler_params=_params("arbitrary", "arbitrary"),
        name="cross_attention",
    )(x, g, wq, mk, mv, wo)


MEM_TILE_ROWS = 8


def _sample_cross_kernel(q_ref, mk_ref, mv_ref, o_ref):
    nb = q_ref.shape[0]
    M = mk_ref.shape[1] // MEM_TILE_ROWS
    for n in range(nb):
        q = q_ref[n]
        k3 = mk_ref[n].reshape(M, MEM_TILE_ROWS, V7X_LANES)
        v3 = mv_ref[n].reshape(M, MEM_TILE_ROWS, V7X_LANES)
        part = jnp.sum(k3 * q[None], axis=-1, keepdims=True)
        s = part + pltpu.roll(part, X_HEADS, 1)
        p = jnp.exp(s - jnp.max(s, axis=0, keepdims=True))
        o = jnp.sum(v3 * p, axis=0) / jnp.sum(p, axis=0)
        o_ref[n] = o


def _sample_cross(q, mk, mv, *, nb):
    N, R, _ = mk.shape
    mem = pl.BlockSpec((nb, R, V7X_LANES), lambda i: (i, 0, 0))
    row = pl.BlockSpec((nb, MEM_TILE_ROWS, V7X_LANES), lambda i: (i, 0, 0))
    return pl.pallas_call(
        _sample_cross_kernel,
        grid=(N // nb,),
        in_specs=[row, mem, mem],
        out_specs=row,
        out_shape=jax.ShapeDtypeStruct((N, MEM_TILE_ROWS, V7X_LANES), F32),
        compiler_params=_params("arbitrary"),
        name="sample_cross_attention",
    )(q, mk, mv)


def _mem_tile_rows(cache):
    N, M, H, Dh = cache.shape
    halves = Dh // V7X_LANES
    assert H * halves == MEM_TILE_ROWS
    return cache.reshape(N, M, H, halves, V7X_LANES).transpose(0, 1, 3, 2, 4).reshape(N, M * MEM_TILE_ROWS, V7X_LANES)


FF_CHUNKS = ((0, 1024), (1024, 1024), (2048, 768))


def _ffn_chunk(h, wup_ref, wdn_ref, cw_ref, cb_ref, d_ff, c0, cw, prev_rows):
    gate = _dot(h, wup_ref[:, c0:c0 + cw])
    val = _dot(h, wup_ref[:, d_ff + c0:d_ff + c0 + cw])
    g2, g1 = prev_rows(gate)
    conv = cb_ref[:, c0:c0 + cw] + (g2 * cw_ref[0:1, c0:c0 + cw] + g1 * cw_ref[1:2, c0:c0 + cw]
                                     + gate * cw_ref[2:3, c0:c0 + cw])
    act = (conv * (1.0 / (1.0 + jnp.exp(-conv)))) * val
    return gate, _dot(act.astype(BF16), wdn_ref[c0:c0 + cw, :])


SAMPLE_UNITS = 4
FF_CHUNKS_PROMPT = ((0, 768), (768, 768), (1536, 768), (2304, 512))


def _ffn_prompt_kernel(x_ref, g_ref, wup_ref, cw_ref, cb_ref, wdn_ref, gf_ref,
                       q_ref, kn_ref, vn_ref, w_ref, kt_hbm, vt_hbm,
                       y_ref, state_ref, o_ref, okt_hbm, ovt_hbm,
                       halo_ref, ink_ref, inv_ref, outk_ref, outv_ref, sem, *, n_tiles, n_steps):
    i = pl.program_id(1)
    step = pl.program_id(0) * n_tiles + i
    tm = x_ref.shape[0]
    d_ff = wdn_ref.shape[0]
    R = ink_ref.shape[1]
    halves = kt_hbm.shape[1] // R

    def copies(g, slot, inbound):
        n = g // halves
        rows = pl.ds(pl.multiple_of((g % halves) * R, R), R)
        if inbound:
            return (pltpu.make_async_copy(kt_hbm.at[n, rows, :], ink_ref.at[slot], sem.at[0, slot]),
                    pltpu.make_async_copy(vt_hbm.at[n, rows, :], inv_ref.at[slot], sem.at[1, slot]))
        return (pltpu.make_async_copy(outk_ref.at[slot], okt_hbm.at[n, rows, :], sem.at[2, slot]),
                pltpu.make_async_copy(outv_ref.at[slot], ovt_hbm.at[n, rows, :], sem.at[3, slot]))

    def start(g, slot, inbound):
        for cp in copies(g, slot, inbound):
            cp.start()

    def wait(g, slot, inbound):
        for cp in copies(g, slot, inbound):
            cp.wait()

    @pl.when(i == 0)
    def _():
        halo_ref[...] = jnp.zeros_like(halo_ref)

    @pl.when(step == 0)
    def _():
        start(0, 0, True)
        start(1, 1, True)

    x = x_ref[...]
    h = _rms(x, g_ref[...]).astype(BF16)
    y = x
    for u, (c0, cw) in enumerate(FF_CHUNKS_PROMPT):
        g = step * SAMPLE_UNITS + u
        slot = u % 2
        wait(g, slot, True)
        if u >= 2:
            wait(g - 2, slot, False)
        else:
            @pl.when(step > 0)
            def _():
                wait(g - 2, slot, False)

        row = lax.broadcasted_iota(jnp.int32, (tm, cw), 0)
        halo = halo_ref[:, c0:c0 + cw]

        def prev_rows(gate):
            g1 = jnp.where(row == 0, halo[1:2], pltpu.roll(gate, 1, 0))
            g2 = jnp.where(row == 0, halo[0:1], jnp.where(row == 1, halo[1:2], pltpu.roll(gate, 2, 0)))
            return g2, g1

        gate, contrib = _ffn_chunk(h, wup_ref, wdn_ref, cw_ref, cb_ref, d_ff, c0, cw, prev_rows)
        y = y + contrib
        halo_ref[0:2, c0:c0 + cw] = gate[tm - 2:, :]
        state_ref[:, c0:c0 + cw] = gate[tm - 2:, :]

        sn, half = u // halves, u % halves
        cols = slice(half * R, (half + 1) * R)
        o_ref[sn, :, cols] = _sample_a_unit(
            q_ref[sn, :, cols], kn_ref[sn, :, cols], vn_ref[sn, :, cols], ink_ref[slot], inv_ref[slot],
            w_ref[...], outk_ref.at[slot], outv_ref.at[slot])

        start(g, slot, False)
        if u < 2:
            start(g + 2, slot, True)
        else:
            @pl.when(step < n_steps - 1)
            def _():
                start(g + 2, slot, True)

    y_ref[...] = _rms(y, gf_ref[...])

    @pl.when(step == n_steps - 1)
    def _():
        last = n_steps * SAMPLE_UNITS
        wait(last - 2, 0, False)
        wait(last - 1, 1, False)


def _ffn_prompt(x, g, wup, cw, cb, wdn, gf, q, k_new, v_new, kt, vt, *, tm):
    B, S, D = x.shape
    d_ff = wdn.shape[0]
    N, W, L = kt.shape
    assert FF_CHUNKS_PROMPT[-1][0] + FF_CHUNKS_PROMPT[-1][1] == d_ff and len(FF_CHUNKS_PROMPT) == SAMPLE_UNITS
    assert L == A_WINDOW, "a cache shorter than the largest dilated window is unsupported"
    n_tiles = S // tm
    n_steps = B * n_tiles
    halves = 2
    seqs = SAMPLE_UNITS // halves
    assert N == seqs * n_steps, "the sample caches are spread evenly over the FFN grid steps"
    R = W // halves
    tok = pl.BlockSpec((None, tm, D), lambda b, i: (b, i, 0))
    const = lambda shape: pl.BlockSpec(shape, lambda b, i: (0,) * len(shape))
    resident = lambda shape: pl.BlockSpec(shape, lambda b, i: (0,) * len(shape), pipeline_mode=pl.Buffered(1))
    rows = pl.BlockSpec((seqs, 1, W), lambda b, i: (b * n_tiles + i, 0, 0))
    hbm = pl.BlockSpec(memory_space=pl.ANY)
    ring = pltpu.VMEM((2, R, L), F32)
    return pl.pallas_call(
        functools.partial(_ffn_prompt_kernel, n_tiles=n_tiles, n_steps=n_steps),
        grid=(B, n_tiles),
        in_specs=[tok, const((1, D)), resident(wup.shape), const(cw.shape), const(cb.shape),
                  resident(wdn.shape), const((1, D)),
                  rows, rows, rows, const((1, L)), hbm, hbm],
        out_specs=[tok, pl.BlockSpec((None, CONV_W - 1, d_ff), lambda b, i: (b, 0, 0)), rows, hbm, hbm],
        out_shape=[jax.ShapeDtypeStruct((B, S, D), F32),
                   jax.ShapeDtypeStruct((B, CONV_W - 1, d_ff), F32),
                   jax.ShapeDtypeStruct((N, 1, W), F32),
                   jax.ShapeDtypeStruct((N, W, L), F32), jax.ShapeDtypeStruct((N, W, L), F32)],
        scratch_shapes=[pltpu.VMEM((8, d_ff), F32), ring, ring, ring, ring, pltpu.SemaphoreType.DMA((4, 2))],
        compiler_params=_params("arbitrary", "arbitrary"),
        name="conv_ffn_prompt_sample_a",
    )(x, g, wup, cw, cb, wdn, gf, q, k_new, v_new, _branch_multiplicity(L), kt, vt)


def _ffn_sample_kernel(x_ref, s0_ref, s1_ref, g_ref, wup_ref, cw_ref, cb_ref, wdn_ref, gf_ref,
                       y_ref, gate_ref):
    d_ff = wdn_ref.shape[0]
    x = x_ref[...]
    h = _rms(x, g_ref[...]).astype(BF16)
    y = x
    for c0, cw in FF_CHUNKS:
        prev_rows = lambda gate: (s0_ref[:, c0:c0 + cw], s1_ref[:, c0:c0 + cw])
        gate, contrib = _ffn_chunk(h, wup_ref, wdn_ref, cw_ref, cb_ref, d_ff, c0, cw, prev_rows)
        y = y + contrib
        gate_ref[:, c0:c0 + cw] = gate
    y_ref[...] = _rms(y, gf_ref[...])


def _ffn_sample(x, s0, s1, g, wup, cw, cb, wdn, gf):
    N, D = x.shape
    d_ff = wdn.shape[0]
    return pl.pallas_call(
        _ffn_sample_kernel,
        out_shape=[jax.ShapeDtypeStruct((N, D), F32), jax.ShapeDtypeStruct((N, d_ff), F32)],
        compiler_params=pltpu.CompilerParams(vmem_limit_bytes=VMEM_LIMIT),
        name="conv_ffn_sample",
    )(x, s0, s1, g, wup, cw, cb, wdn, gf)


def _qb_pair_perm():
    cols = []
    for j in range(B_GROUP):
        for hk in range(B_KV_HEADS):
            h = hk * B_GROUP + j
            cols.extend(range(h * HEAD_DIM, (h + 1) * HEAD_DIM))
    return np.asarray(cols, dtype=np.int32)


def _layer_weights(l, g_mix, w_in, g_out_a, g_out_b, sinks, w_out, g_cross, g_mem, w_xq, w_mem_kv, w_xo,
                   g_ffn, w_up, conv_w, conv_b, w_down):
    perm = _qb_pair_perm()
    qb0 = 3 * A_W
    w_in_l = w_in[l]
    w_in_l = jnp.concatenate([w_in_l[:, :qb0], w_in_l[:, qb0:qb0 + B_W][:, perm], w_in_l[:, qb0 + B_W:]], axis=1)
    w_out_l = w_out[l]
    w_out_l = jnp.concatenate([w_out_l[:A_W], w_out_l[A_W:][perm]], axis=0)
    sink = sinks[l].astype(F32)
    pair = jnp.stack([sink[:B_GROUP], sink[B_GROUP:]], axis=1)
    sink_lanes = jnp.broadcast_to(jnp.repeat(pair, HEAD_DIM, axis=1)[:, None, :], (B_GROUP, 8, V7X_LANES))
    sink_rows = pair.reshape(2 * B_GROUP, 1)
    row = lambda v: v.reshape(1, -1).astype(F32)
    return dict(
        g_mix=row(g_mix[l]), w_in=w_in_l.astype(BF16),
        g_out_a=row(g_out_a[l]), g_out_b=row(g_out_b[l][perm]), w_out=w_out_l.astype(BF16),
        sink_lanes=sink_lanes, sink_rows=sink_rows,
        g_cross=row(g_cross[l]), g_mem=row(g_mem[l]), w_xq=w_xq[l].astype(BF16),
        w_mem_kv=w_mem_kv[l].astype(BF16), w_xo=w_xo[l].astype(BF16),
        g_ffn=row(g_ffn[l]), w_up=w_up[l].astype(BF16), conv_w=conv_w[l].astype(F32),
        conv_b=row(conv_b[l]), w_down=w_down[l].astype(BF16))


PROMPT_TM = 512
SAMPLE_NB = 8


def _prompt_layer(x, mem, W, g_final, cos, sin, sample_a):
    B, S, D = x.shape
    M = mem.shape[1]
    tm = min(PROMPT_TM, S)
    qa, ka, va, qb, kb, vb, ka_t, va_t, kb_t, vb_t = _inproj(
        x, W["g_mix"], W["w_in"], cos, sin, tm=tm, prompt=True)
    flat = lambda t: t.reshape(B, S, A_W)
    oa = _attn_a(flat(qa), flat(ka), flat(va)).reshape(B, R4, S // R4, A_W)
    ob = _attn_b(qb, kb, vb, W["sink_lanes"])
    x1 = _outproj(oa, ob, x, W["g_out_a"], W["g_out_b"], W["w_out"], tm=tm)
    mk, mv = _norm_matmul(mem.reshape(B * M, D), W["g_mem"], W["w_mem_kv"], n_out=2, tm=min(512, B * M))
    mk = mk.reshape(B, M, D)
    mv = mv.reshape(B, M, D)
    x2 = _cross(x1, W["g_cross"], W["w_xq"], mk, mv, W["w_xo"], tm=tm)
    y, conv_state, oa_s, s_akt, s_avt = _ffn_prompt(
        x2, W["g_ffn"], W["w_up"], W["conv_w"], W["conv_b"], W["w_down"], g_final, *sample_a, tm=tm)
    return y, (ka_t, va_t, kb_t, vb_t, mk, mv, conv_state), (oa_s, s_akt, s_avt)


def _sample_inproj(x, W, cos, sin):
    N = x.shape[0]
    return [t[0] for t in _inproj(x[None], W["g_mix"], W["w_in"], cos, sin, tm=N, prompt=False)]


def _sample_layer(x, oa, qb, kb, vb, b_kt, b_vt, mem_k, mem_v, conv_state, W, g_final):
    N, D = x.shape
    ob, s_bkt, s_bvt = _sample_b(qb, kb, vb, b_kt, b_vt, W["sink_rows"], nb=SAMPLE_NB)
    x1 = _outproj(oa[None], ob[None], x[None], W["g_out_a"], W["g_out_b"], W["w_out"], tm=N)[0]
    (q,) = _norm_matmul(x1, W["g_cross"], W["w_xq"], n_out=1, tm=N)
    hd = D // X_HEADS
    halves = hd // V7X_LANES
    q = (q * (hd ** -0.5)).reshape(N, X_HEADS, halves, V7X_LANES).transpose(0, 2, 1, 3)
    o = _sample_cross(q.reshape(N, MEM_TILE_ROWS, V7X_LANES), mem_k, mem_v, nb=SAMPLE_NB)
    o = o.reshape(N, halves, X_HEADS, V7X_LANES).transpose(0, 2, 1, 3).reshape(N, D)
    x2 = _matmul_residual(o, W["w_xo"], x1, tm=N)
    y, gate = _ffn_sample(x2, conv_state[:, 0], conv_state[:, 1], W["g_ffn"], W["w_up"], W["conv_w"],
                          W["conv_b"], W["w_down"], g_final)
    new_conv = jnp.stack([conv_state[:, 1], gate], axis=1)
    return y, (s_bkt, s_bvt, new_conv)


def _time_minor(cache):
    N, L, H, Dh = cache.shape
    return cache.transpose(0, 2, 3, 1).reshape(N, H * Dh, L)


def _time_major(cache_t, H):
    N, W, L = cache_t.shape
    return cache_t.reshape(N, H, W // H, L).transpose(0, 3, 1, 2)


def kernel(x_prompt, x_sample, cache_a_k, cache_a_v, cache_b_k, cache_b_v, cache_mem_k, cache_mem_v, state_conv,
           mem_prompt, g_mix, w_in, g_out_a, g_out_b, sinks, w_out, g_cross, g_mem, w_xq, w_mem_kv, w_xo,
           g_ffn, w_up, conv_w, conv_b, w_down, g_final):
    depth = w_in.shape[0]
    assert depth == 1, "layer stacking is not wired up: the problem has a single layer"
    B, S, D = x_prompt.shape
    N, T, _ = x_sample.shape
    assert T == 1, "the sample group decodes one token per sequence"
    gf = g_final.reshape(1, D).astype(F32)
    cos_p, sin_p = _rope_tables(S, 0, 1)
    cos_s, sin_s = _rope_tables(N, PAST_LEN, 0)

    l = 0
    W = _layer_weights(l, g_mix, w_in, g_out_a, g_out_b, sinks, w_out, g_cross, g_mem, w_xq, w_mem_kv, w_xo,
                       g_ffn, w_up, conv_w, conv_b, w_down)

    xs = x_sample.reshape(N, D)
    qa, ka, va, qb, kb, vb = _sample_inproj(xs, W, cos_s, sin_s)
    sample_a = (qa[:, None, :], ka[:, None, :], va[:, None, :], _time_minor(cache_a_k[l]), _time_minor(cache_a_v[l]))
    yp, (p_ak, p_av, p_bk, p_bv, p_mk, p_mv, p_conv), (oa_s, s_akt, s_avt) = _prompt_layer(
        x_prompt, mem_prompt, W, gf, cos_p, sin_p, sample_a)
    ys, (s_bkt, s_bvt, s_conv) = _sample_layer(
        xs, oa_s.reshape(N, A_W), qb, kb, vb, _time_minor(cache_b_k[l]), _time_minor(cache_b_v[l]),
        _mem_tile_rows(cache_mem_k[l]), _mem_tile_rows(cache_mem_v[l]), state_conv[l], W, gf)

    la, lb = p_ak.shape[1], p_bk.shape[1]
    return (yp, ys.reshape(N, 1, D),
            p_ak.reshape(1, B, la, A_HEADS, HEAD_DIM), p_av.reshape(1, B, la, A_HEADS, HEAD_DIM),
            p_bk.reshape(1, B, lb, B_KV_HEADS, HEAD_DIM), p_bv.reshape(1, B, lb, B_KV_HEADS, HEAD_DIM),
            p_mk.reshape(1, B, -1, X_HEADS, D // X_HEADS), p_mv.reshape(1, B, -1, X_HEADS, D // X_HEADS),
            p_conv[None],
            _time_major(s_akt, A_HEADS)[None], _time_major(s_avt, A_HEADS)[None],
            _time_major(s_bkt, B_KV_HEADS)[None], _time_major(s_bvt, B_KV_HEADS)[None],
            s_conv[None])
```

```python
import functools

import jax
import jax.numpy as jnp
import numpy as np
from jax import lax
from jax.experimental import pallas as pl
from jax.experimental.pallas import tpu as pltpu

F32 = jnp.float32
BF16 = jnp.bfloat16

HEAD_DIM = 64
A_HEADS = 8
B_HEADS = 8
B_KV_HEADS = 2
B_GROUP = B_HEADS // B_KV_HEADS
A_W = A_HEADS * HEAD_DIM
B_W = B_HEADS * HEAD_DIM
B_KV_W = B_KV_HEADS * HEAD_DIM
DILATIONS = (1, 4, 16)
A_WINDOW = 2048
B_WINDOW = 128
BLOCK = 128
ROPE_THETA = 10000.0
PAST_LEN = 16384
X_HEADS = 4
CONV_W = 3
EPS = 1e-6
NEG = -1e30
LOG2E = 1.4426950408889634
ATTN_UNROLL = 32
R4 = 4

V7X_LANES = 128
V7X_MXU_DIM = 256
V7X_VMEM_BYTES = 64 * 1024 * 1024
VMEM_LIMIT = V7X_VMEM_BYTES - 8 * 1024 * 1024


def _params(*sem):
    return pltpu.CompilerParams(dimension_semantics=sem, vmem_limit_bytes=VMEM_LIMIT)


def _rms(x, g):
    return (x * lax.rsqrt(jnp.mean(x * x, axis=-1, keepdims=True) + EPS)) * g


def _dot(a, b):
    return jnp.dot(a, b, preferred_element_type=F32)


def _dot_t(a, b):
    return lax.dot_general(a, b, (((1,), (1,)), ((), ())), preferred_element_type=F32)


def _rope_table_kernel(inv_ref, cos_ref, sin_ref, *, pos0, pos_step):
    rows = cos_ref.shape[0]
    row = lax.broadcasted_iota(jnp.int32, (rows, V7X_LANES), 0)
    lane = lax.broadcasted_iota(jnp.int32, (rows, V7X_LANES), 1)
    ang = (pos0 + pos_step * row).astype(F32) * inv_ref[...]
    first_half = (lane % HEAD_DIM) < (HEAD_DIM // 2)
    cos_ref[...] = jnp.cos(ang)
    sin_ref[...] = jnp.where(first_half, -jnp.sin(ang), jnp.sin(ang))


def _rope_tables(rows, pos0, pos_step):
    half = HEAD_DIM // 2
    inv = jnp.power(ROPE_THETA, -jnp.arange(half, dtype=F32) / half)
    inv = jnp.tile(inv, V7X_LANES // half).reshape(1, V7X_LANES)
    return pl.pallas_call(
        functools.partial(_rope_table_kernel, pos0=pos0, pos_step=pos_step),
        out_shape=[jax.ShapeDtypeStruct((rows, V7X_LANES), F32)] * 2,
        name="rope_tables",
    )(inv)


def _rope(slab, cos, sin, first_half):
    partner = jnp.where(first_half, pltpu.roll(slab, 96, 1), pltpu.roll(slab, 32, 1))
    return slab * cos + partner * sin


def _inproj_kernel(x_ref, g_ref, w_ref, cos_ref, sin_ref,
                   qa_ref, ka_ref, va_ref, qb_ref, kb_ref, vb_ref, *rest,
                   tail_skip, n_tiles, prompt):
    tm = x_ref.shape[0]
    hn = _rms(x_ref[...], g_ref[...]).astype(BF16)
    cos = cos_ref[...]
    sin = sin_ref[...]
    lane = lax.broadcasted_iota(jnp.int32, (tm, V7X_LANES), 1)
    first_half = (lane % HEAD_DIM) < (HEAD_DIM // 2)

    def seg(c0, width):
        return _dot(hn, w_ref[:, c0:c0 + width])

    def roped(z):
        return jnp.concatenate(
            [_rope(z[:, c:c + V7X_LANES], cos, sin, first_half)
             for c in range(0, z.shape[1], V7X_LANES)], axis=1)

    if prompt:
        kat_ref, vat_ref, kbt_ref, vbt_ref, shuf_ref = rest

        def put_a(dst_ref, z):
            for c in range(A_W // V7X_LANES):
                shuf_ref[c] = z[:, c * V7X_LANES:(c + 1) * V7X_LANES]
            for c in range(A_W // V7X_LANES):
                for r in range(R4):
                    dst_ref[r, :, c * V7X_LANES:(c + 1) * V7X_LANES] = shuf_ref[c, pl.ds(r, tm // R4, stride=R4), :]
    else:
        def put_a(dst_ref, z):
            dst_ref[...] = z

    put_a(qa_ref, roped(seg(0, A_W)))
    ka = roped(seg(A_W, A_W))
    put_a(ka_ref, ka)
    va = seg(2 * A_W, A_W)
    put_a(va_ref, va)
    qb_ref[...] = roped(seg(3 * A_W, B_W))
    kvb = seg(3 * A_W + B_W, 2 * B_KV_W)
    kb = roped(kvb[:, :B_KV_W])
    vb = kvb[:, B_KV_W:]
    kb_ref[...] = kb
    vb_ref[...] = vb

    if prompt:
        i = pl.program_id(1)

        @pl.when(i >= tail_skip)
        def _():
            kat_ref[...] = ka
            vat_ref[...] = va

        @pl.when(i == n_tiles - 1)
        def _():
            kbt_ref[...] = kb[tm - B_WINDOW:, :]
            vbt_ref[...] = vb[tm - B_WINDOW:, :]


def _inproj(x, g, w, cos, sin, *, tm, prompt):
    B, S, D = x.shape
    n_tiles = S // tm
    la = min(A_WINDOW, S)
    lb = min(B_WINDOW, S)
    tail_skip = (S - la) // tm
    tok = lambda w_: pl.BlockSpec((None, tm, w_), lambda b, i: (b, i, 0))
    const = lambda shape: pl.BlockSpec(shape, lambda b, i: (0,) * len(shape))
    if prompt:
        a_shape = jax.ShapeDtypeStruct((B, R4, S // R4, A_W), F32)
        a_spec = pl.BlockSpec((None, R4, tm // R4, A_W), lambda b, i: (b, 0, i, 0))
    else:
        a_shape = jax.ShapeDtypeStruct((B, S, A_W), F32)
        a_spec = tok(A_W)
    out_shape = [a_shape] * 3 + [
        jax.ShapeDtypeStruct((B, S, B_W), F32),
        jax.ShapeDtypeStruct((B, S, B_KV_W), F32),
        jax.ShapeDtypeStruct((B, S, B_KV_W), F32)]
    out_specs = [a_spec] * 3 + [tok(B_W), tok(B_KV_W), tok(B_KV_W)]
    scratch = []
    if prompt:
        out_shape += [jax.ShapeDtypeStruct((B, la, A_W), F32)] * 2
        out_shape += [jax.ShapeDtypeStruct((B, lb, B_KV_W), F32)] * 2
        a_tail = pl.BlockSpec((None, tm, A_W), lambda b, i: (b, jnp.maximum(i - tail_skip, 0), 0))
        b_tail = pl.BlockSpec((None, lb, B_KV_W), lambda b, i: (b, 0, 0))
        out_specs += [a_tail, a_tail, b_tail, b_tail]
        scratch = [pltpu.VMEM((A_W // V7X_LANES, tm, V7X_LANES), F32)]
    return pl.pallas_call(
        functools.partial(_inproj_kernel, tail_skip=tail_skip, n_tiles=n_tiles, prompt=prompt),
        grid=(B, n_tiles),
        in_specs=[tok(D), const((1, D)), const(w.shape),
                  pl.BlockSpec((tm, V7X_LANES), lambda b, i: (i, 0)),
                  pl.BlockSpec((tm, V7X_LANES), lambda b, i: (i, 0))],
        out_specs=out_specs,
        out_shape=out_shape,
        scratch_shapes=scratch,
        compiler_params=_params("arbitrary", "arbitrary"),
        name="inproj_rope",
    )(x, g, w, cos, sin)


def _band_bias(first, chunked):
    a = lax.broadcasted_iota(jnp.int32, (2 * BLOCK, 2 * BLOCK), 0) % BLOCK
    b = lax.broadcasted_iota(jnp.int32, (2 * BLOCK, 2 * BLOCK), 1)
    own = b >= BLOCK
    bb = b % BLOCK
    if chunked:
        sub = BLOCK // R4
        a = R4 * (a % sub) + a // sub
        bb = R4 * (bb % sub) + bb // sub
    dist = BLOCK + a - (bb + jnp.where(own, BLOCK, 0))
    valid = (dist >= 0) & (dist <= BLOCK)
    if first:
        valid = valid & own
    return jnp.where(valid, 0.0, NEG).astype(F32)


def _two_head_block(q, kk, vv, bias):
    lane = lax.broadcasted_iota(jnp.int32, (BLOCK, V7X_LANES), 1)
    lo = lane < HEAD_DIM
    q = q * (HEAD_DIM ** -0.5 * LOG2E)
    qs = jnp.concatenate([jnp.where(lo, q, 0.0), jnp.where(lo, 0.0, q)], axis=0).astype(BF16)
    s = _dot_t(qs, kk) + bias
    m = jnp.max(s, axis=-1, keepdims=True)
    p = jnp.exp2(s - m)
    v1 = jnp.concatenate([vv, jnp.ones_like(vv)], axis=1)
    pv = _dot(p.astype(BF16), v1)
    acc = jnp.where(lo, pv[:BLOCK, :V7X_LANES], pv[BLOCK:, :V7X_LANES])
    l2 = jnp.where(lo, pv[:BLOCK, V7X_LANES:], pv[BLOCK:, V7X_LANES:])
    m2 = jnp.where(lo, m[:BLOCK], m[BLOCK:])
    return acc, m2, l2


def _get(ref, slices):
    return jnp.concatenate([ref[sl, :] for sl in slices], axis=0) if len(slices) > 1 else ref[slices[0], :]


def _put(ref, slices, val):
    off = 0
    for sl in slices:
        ref[sl, :] = val[off:off + sl.size]
        off += sl.size


def _block_slices(layout, d, S, i):
    if layout == "seq":
        start = pl.multiple_of(i * BLOCK, BLOCK)
        prev = pl.multiple_of(jnp.maximum(i - 1, 0) * BLOCK, BLOCK)
        return [pl.ds(start, BLOCK)], [pl.ds(prev, BLOCK)], i == 0
    Sr = S // R4
    if d == 1:
        sub = BLOCK // R4
        pj = jnp.maximum(i - 1, 0)
        own = [pl.ds(pl.multiple_of(r * Sr + i * sub, sub), sub) for r in range(R4)]
        prev = [pl.ds(pl.multiple_of(r * Sr + pj * sub, sub), sub) for r in range(R4)]
        return own, prev, i == 0
    if d == R4:
        per_res = Sr // BLOCK
        jb = i % per_res
        start = pl.multiple_of(i * BLOCK, BLOCK)
        prev = pl.multiple_of(jnp.where(jb == 0, i, i - 1) * BLOCK, BLOCK)
        return [pl.ds(start, BLOCK)], [pl.ds(prev, BLOCK)], jb == 0
    step = d // R4
    per_res = S // (BLOCK * d)
    rd = i // per_res
    jb = i % per_res
    start = (rd % R4) * Sr + rd // R4 + jb * (BLOCK * step)
    prev = jnp.where(jb == 0, start, start - BLOCK * step)
    return [pl.ds(start, BLOCK, stride=step)], [pl.ds(prev, BLOCK, stride=step)], jb == 0


def _branch_blocks(q_ref, k_ref, v_ref, bias_ref, layout, d, visit):
    S = q_ref.shape[0]
    bias_base = 2 if (layout == "r4" and d == 1) else 0

    def body(i, carry):
        own, prev, first = _block_slices(layout, d, S, i)
        kk = jnp.concatenate([_get(k_ref, prev), _get(k_ref, own)], axis=0).astype(BF16)
        vv = jnp.concatenate([_get(v_ref, prev), _get(v_ref, own)], axis=0).astype(BF16)
        bias = bias_ref[bias_base + jnp.where(first, 1, 0)]
        acc, m, l = _two_head_block(_get(q_ref, own), kk, vv, bias)
        visit(own, acc, m, l)
        return carry

    lax.fori_loop(0, S // BLOCK, body, 0, unroll=ATTN_UNROLL)


def _init_bias(bias_ref, chunked_too):
    bias_ref[0] = _band_bias(False, False)
    bias_ref[1] = _band_bias(True, False)
    if chunked_too:
        bias_ref[2] = _band_bias(False, True)
        bias_ref[3] = _band_bias(True, True)


def _attn_a_kernel(q_ref, k_ref, v_ref, o_ref, m_ref, l_ref, bias_ref):
    _init_bias(bias_ref, True)

    def first_visit(rows, acc, m, l):
        _put(o_ref, rows, acc)
        _put(m_ref, rows, m)
        _put(l_ref, rows, l)

    def merged(rows, acc, m, l):
        m_old = _get(m_ref, rows)
        m_new = jnp.maximum(m_old, m)
        a_old = jnp.exp2(m_old - m_new)
        a_new = jnp.exp2(m - m_new)
        return (a_old * _get(o_ref, rows) + a_new * acc, m_new, a_old * _get(l_ref, rows) + a_new * l)

    def mid_visit(rows, acc, m, l):
        acc, m, l = merged(rows, acc, m, l)
        _put(o_ref, rows, acc)
        _put(m_ref, rows, m)
        _put(l_ref, rows, l)

    def last_visit(rows, acc, m, l):
        acc, m, l = merged(rows, acc, m, l)
        _put(o_ref, rows, acc / l)

    order = (R4, 1, 16)
    assert sorted(order) == sorted(DILATIONS)
    visits = [first_visit] + [mid_visit] * (len(order) - 2) + [last_visit]
    for d, visit in zip(order, visits):
        _branch_blocks(q_ref, k_ref, v_ref, bias_ref, "r4", d, visit)


def _attn_b_kernel(q_ref, k_ref, v_ref, sink_ref, o_ref, bias_ref):
    _init_bias(bias_ref, False)
    sink = sink_ref[0:1, :] * LOG2E

    def visit(rows, acc, m, l):
        _put(o_ref, rows, acc / (l + jnp.exp2(sink - m)))

    _branch_blocks(q_ref, k_ref, v_ref, bias_ref, "seq", 1, visit)


def _attn_a(q, k, v):
    B, S, W = q.shape
    slab = pl.BlockSpec((None, S, V7X_LANES), lambda b, j: (b, 0, j))
    return pl.pallas_call(
        _attn_a_kernel,
        grid=(B, W // V7X_LANES),
        in_specs=[slab, slab, slab],
        out_specs=slab,
        out_shape=jax.ShapeDtypeStruct((B, S, W), F32),
        scratch_shapes=[pltpu.VMEM((S, V7X_LANES), F32), pltpu.VMEM((S, V7X_LANES), F32),
                        pltpu.VMEM((4, 2 * BLOCK, 2 * BLOCK), F32)],
        compiler_params=_params("arbitrary", "arbitrary"),
        name="mixer_a_attention",
    )(q, k, v)


def _attn_b(q, k, v, sink_lanes):
    B, S, W = q.shape
    slab = pl.BlockSpec((None, S, V7X_LANES), lambda b, j: (b, 0, j))
    kv = pl.BlockSpec((None, S, V7X_LANES), lambda b, j: (b, 0, 0))
    return pl.pallas_call(
        _attn_b_kernel,
        grid=(B, W // V7X_LANES),
        in_specs=[slab, kv, kv, pl.BlockSpec((None, 8, V7X_LANES), lambda b, j: (j, 0, 0))],
        out_specs=slab,
        out_shape=jax.ShapeDtypeStruct((B, S, W), F32),
        scratch_shapes=[pltpu.VMEM((2, 2 * BLOCK, 2 * BLOCK), F32)],
        compiler_params=_params("arbitrary", "arbitrary"),
        name="mixer_b_attention",
    )(q, k, v, sink_lanes)


def _as_column(row):
    return jnp.broadcast_to(row, (V7X_LANES, row.shape[1])).T


def _roll_in(dst_ref, cache, new_col):
    L = cache.shape[1]
    rolled = pltpu.roll(cache, L - 1, 1)
    lane = lax.broadcasted_iota(jnp.int32, (cache.shape[0], V7X_LANES), 1)
    if L > V7X_LANES:
        dst_ref[:, :L - V7X_LANES] = rolled[:, :L - V7X_LANES]
    dst_ref[:, L - V7X_LANES:] = jnp.where(lane == V7X_LANES - 1, new_col, rolled[:, L - V7X_LANES:])


def _sample_a_unit(q, k_new, v_new, kt, vt, w, okt_ref, ovt_ref):
    R, L = kt.shape
    H = R // HEAD_DIM
    n_br = len(DILATIONS)
    qc = _as_column(q * (HEAD_DIM ** -0.5))
    knc = _as_column(k_new)
    vnc = _as_column(v_new)
    prod = jnp.concatenate([kt[:, j:j + V7X_LANES] * qc for j in range(0, L, V7X_LANES)], axis=1)
    s = jnp.sum(prod.reshape(H, HEAD_DIM, L), axis=1)
    s_new = jnp.sum((qc * knc).reshape(H, HEAD_DIM, V7X_LANES), axis=1)[:, 0:1]
    s = jnp.where(w > 0, s, NEG)
    m = jnp.maximum(jnp.max(s, axis=-1, keepdims=True), s_new)
    p = w * jnp.exp(s - m)
    p_new = n_br * jnp.exp(s_new - m)
    l = jnp.sum(p, axis=-1, keepdims=True) + p_new

    def per_row(a):
        return jnp.broadcast_to(a.reshape(H, 1, a.shape[1]), (H, HEAD_DIM, a.shape[1])).reshape(R, a.shape[1])

    o_col = (jnp.sum(vt * per_row(p), axis=-1, keepdims=True) + per_row(p_new) * vnc[:, 0:1]) / per_row(l)
    _roll_in(okt_ref, kt, knc)
    _roll_in(ovt_ref, vt, vnc)
    return jnp.broadcast_to(o_col, (R, V7X_LANES)).T[0:1, :]


def _branch_multiplicity(L):
    dist = L - np.arange(L)
    w = sum(((dist % d == 0) & (dist <= BLOCK * d)).astype(np.float32) for d in DILATIONS)
    return jnp.asarray(w.reshape(1, L))


def _sample_b_kernel(q_ref, kn_ref, vn_ref, kt_ref, vt_ref, sink_ref, o_ref, okt_ref, ovt_ref):
    nb = q_ref.shape[0]
    lane = lax.broadcasted_iota(jnp.int32, (1, V7X_LANES), 1)
    lo = lane < HEAD_DIM
    for n in range(nb):
        q = q_ref[n:n + 1, :] * (HEAD_DIM ** -0.5)
        rows = []
        for j in range(B_W // V7X_LANES):
            pair = q[:, j * V7X_LANES:(j + 1) * V7X_LANES]
            rows += [jnp.where(lo, pair, 0.0), jnp.where(lo, 0.0, pair)]
        qh = jnp.concatenate(rows, axis=0)
        kt = kt_ref[n]
        vt = vt_ref[n]
        s_new = jnp.sum(qh * kn_ref[n:n + 1, :], axis=-1, keepdims=True)
        s = _dot(qh.astype(BF16), kt.astype(BF16))
        m = jnp.maximum(s_new, jnp.max(s, axis=-1, keepdims=True))
        p_new = jnp.exp(s_new - m)
        p = jnp.exp(s - m)
        l = p_new + jnp.sum(p, axis=-1, keepdims=True)
        o = p_new * vn_ref[n:n + 1, :] + _dot_t(p.astype(BF16), vt.astype(BF16))
        o = o / (l + jnp.exp(sink_ref[...] - m))
        o_ref[n:n + 1, :] = jnp.concatenate(
            [jnp.where(lo, o[2 * j:2 * j + 1], o[2 * j + 1:2 * j + 2]) for j in range(B_W // V7X_LANES)], axis=1)
        _roll_in(okt_ref.at[n], kt, _as_column(kn_ref[n:n + 1, :]))
        _roll_in(ovt_ref.at[n], vt, _as_column(vn_ref[n:n + 1, :]))


def _sample_b(q, k_new, v_new, kt, vt, sink_rows, *, nb):
    N, W, L = kt.shape
    assert L == B_WINDOW and W == V7X_LANES, "a cache shorter than the sliding window is unsupported"
    row = lambda w_: pl.BlockSpec((nb, w_), lambda i: (i, 0))
    cache = pl.BlockSpec((nb, W, L), lambda i: (i, 0, 0))
    return pl.pallas_call(
        _sample_b_kernel,
        grid=(N // nb,),
        in_specs=[row(B_W), row(W), row(W), cache, cache, pl.BlockSpec((8, 1), lambda i: (0, 0))],
        out_specs=[row(B_W), cache, cache],
        out_shape=[jax.ShapeDtypeStruct((N, B_W), F32), jax.ShapeDtypeStruct((N, W, L), F32),
                   jax.ShapeDtypeStruct((N, W, L), F32)],
        compiler_params=_params("arbitrary"),
        name="sample_mixer_b",
    )(q, k_new, v_new, kt, vt, sink_rows)


def _mixers_out(oa, ob, x, ga_ref, gb_ref, w_ref):
    ha = _rms(oa, ga_ref[...]).astype(BF16)
    hb = _rms(ob, gb_ref[...]).astype(BF16)
    return x + _dot(ha, w_ref[:A_W, :]) + _dot(hb, w_ref[A_W:, :])


def _outproj_kernel(oa_ref, ob_ref, x_ref, ga_ref, gb_ref, w_ref, y_ref):
    y_ref[...] = _mixers_out(oa_ref[...], ob_ref[...], x_ref[...], ga_ref, gb_ref, w_ref)


def _outproj(oa, ob, x, ga, gb, w):
    return pl.pallas_call(
        _outproj_kernel,
        out_shape=jax.ShapeDtypeStruct(x.shape, F32),
        compiler_params=pltpu.CompilerParams(vmem_limit_bytes=VMEM_LIMIT),
        name="mixers_outproj",
    )(oa, ob, x, ga, gb, w)


def _norm_matmul_kernel(x_ref, g_ref, w_ref, *out_refs):
    h = _rms(x_ref[...], g_ref[...]).astype(BF16)
    width = out_refs[0].shape[1]
    for j, o_ref in enumerate(out_refs):
        o_ref[...] = _dot(h, w_ref[:, j * width:(j + 1) * width])


def _norm_matmul(x, g, w, *, n_out, tm):
    T, D = x.shape
    width = w.shape[1] // n_out
    return pl.pallas_call(
        _norm_matmul_kernel,
        grid=(T // tm,),
        in_specs=[pl.BlockSpec((tm, D), lambda i: (i, 0)), pl.BlockSpec((1, D), lambda i: (0, 0)),
                  pl.BlockSpec(w.shape, lambda i: (0, 0))],
        out_specs=[pl.BlockSpec((tm, width), lambda i: (i, 0))] * n_out,
        out_shape=[jax.ShapeDtypeStruct((T, width), F32)] * n_out,
        compiler_params=_params("arbitrary"),
        name="norm_matmul",
    )(x, g, w)


def _matmul_residual_kernel(a_ref, w_ref, r_ref, y_ref):
    y_ref[...] = r_ref[...] + _dot(a_ref[...].astype(BF16), w_ref[...])


def _matmul_residual(a, w, r, *, tm):
    T, K = a.shape
    D = w.shape[1]
    return pl.pallas_call(
        _matmul_residual_kernel,
        grid=(T // tm,),
        in_specs=[pl.BlockSpec((tm, K), lambda i: (i, 0)), pl.BlockSpec(w.shape, lambda i: (0, 0)),
                  pl.BlockSpec((tm, D), lambda i: (i, 0))],
        out_specs=pl.BlockSpec((tm, D), lambda i: (i, 0)),
        out_shape=jax.ShapeDtypeStruct((T, D), F32),
        compiler_params=_params("arbitrary"),
        name="matmul_residual",
    )(a, w, r)


def _outproj_cross_kernel(oa_ref, ob_ref, x_ref, ga_ref, gb_ref, wout_ref, gc_ref, wq_ref, mk_ref, mv_ref, wo_ref,
                          y_ref, shuf_ref, q_ref):
    tq = oa_ref.shape[1]
    for c in range(A_W // V7X_LANES):
        for r in range(R4):
            shuf_ref[c, pl.ds(r, tq, stride=R4), :] = oa_ref[r, :, c * V7X_LANES:(c + 1) * V7X_LANES]
    oa = jnp.concatenate([shuf_ref[c] for c in range(A_W // V7X_LANES)], axis=1)
    y_ref[...] = _mixers_out(oa, ob_ref[...], x_ref[...], ga_ref, gb_ref, wout_ref)
    D = y_ref.shape[1]
    hd = D // X_HEADS
    q_ref[...] = (_dot(_rms(y_ref[...], gc_ref[...]).astype(BF16), wq_ref[...]) * (hd ** -0.5)).astype(BF16)
    for h in range(X_HEADS):
        cols = slice(h * hd, (h + 1) * hd)
        s = _dot_t(q_ref[:, cols], mk_ref[:, cols].astype(BF16))
        p = jnp.exp(s - jnp.max(s, axis=-1, keepdims=True))
        o = _dot(p.astype(BF16), mv_ref[:, cols].astype(BF16)) / jnp.sum(p, axis=-1, keepdims=True)
        y_ref[...] += _dot(o.astype(BF16), wo_ref[cols, :])


def _outproj_cross(oa, ob, x, ga, gb, wout, gc, wq, mk, mv, wo, *, tm):
    B, S, D = x.shape
    M = mk.shape[1]
    tok = lambda w_: pl.BlockSpec((None, tm, w_), lambda b, i: (b, i, 0))
    mem = pl.BlockSpec((None, M, D), lambda b, i: (b, 0, 0))
    const = lambda shape: pl.BlockSpec(shape, lambda b, i: (0,) * len(shape))
    return pl.pallas_call(
        _outproj_cross_kernel,
        grid=(B, S // tm),
        in_specs=[pl.BlockSpec((None, R4, tm // R4, A_W), lambda b, i: (b, 0, i, 0)), tok(B_W), tok(D),
                  const((1, A_W)), const((1, B_W)), const(wout.shape), const((1, D)), const(wq.shape),
                  mem, mem, const(wo.shape)],
        out_specs=tok(D),
        out_shape=jax.ShapeDtypeStruct((B, S, D), F32),
        scratch_shapes=[pltpu.VMEM((A_W // V7X_LANES, tm, V7X_LANES), F32), pltpu.VMEM((tm, D), BF16)],
        compiler_params=_params("arbitrary", "arbitrary"),
        name="outproj_cross_attention",
    )(oa, ob, x, ga, gb, wout, gc, wq, mk, mv, wo)


MEM_TILE_ROWS = 8


def _sample_cross_kernel(q_ref, mk_ref, mv_ref, o_ref):
    nb = q_ref.shape[0]
    M = mk_ref.shape[1] // MEM_TILE_ROWS
    for n in range(nb):
        q = q_ref[n]
        k3 = mk_ref[n].reshape(M, MEM_TILE_ROWS, V7X_LANES)
        v3 = mv_ref[n].reshape(M, MEM_TILE_ROWS, V7X_LANES)
        part = jnp.sum(k3 * q[None], axis=-1, keepdims=True)
        s = part + pltpu.roll(part, X_HEADS, 1)
        p = jnp.exp(s - jnp.max(s, axis=0, keepdims=True))
        o = jnp.sum(v3 * p, axis=0) / jnp.sum(p, axis=0)
        o_ref[n] = o


def _sample_cross(q, mk, mv, *, nb):
    N, R, _ = mk.shape
    mem = pl.BlockSpec((nb, R, V7X_LANES), lambda i: (i, 0, 0))
    row = pl.BlockSpec((nb, MEM_TILE_ROWS, V7X_LANES), lambda i: (i, 0, 0))
    return pl.pallas_call(
        _sample_cross_kernel,
        grid=(N // nb,),
        in_specs=[row, mem, mem],
        out_specs=row,
        out_shape=jax.ShapeDtypeStruct((N, MEM_TILE_ROWS, V7X_LANES), F32),
        compiler_params=_params("arbitrary"),
        name="sample_cross_attention",
    )(q, mk, mv)


def _mem_tile_rows(cache):
    N, M, H, Dh = cache.shape
    halves = Dh // V7X_LANES
    assert H * halves == MEM_TILE_ROWS
    return cache.reshape(N, M, H, halves, V7X_LANES).transpose(0, 1, 3, 2, 4).reshape(N, M * MEM_TILE_ROWS, V7X_LANES)


FF_CHUNKS = ((0, 1024), (1024, 1024), (2048, 768))


def _ffn_chunk(h, wup_ref, wdn_ref, cw_ref, cb_ref, d_ff, c0, cw, prev_rows):
    gate = _dot(h, wup_ref[:, c0:c0 + cw])
    val = _dot(h, wup_ref[:, d_ff + c0:d_ff + c0 + cw])
    g2, g1 = prev_rows(gate)
    conv = cb_ref[:, c0:c0 + cw] + (g2 * cw_ref[0:1, c0:c0 + cw] + g1 * cw_ref[1:2, c0:c0 + cw]
                                     + gate * cw_ref[2:3, c0:c0 + cw])
    act = (conv * (1.0 / (1.0 + jnp.exp(-conv)))) * val
    return gate, _dot(act.astype(BF16), wdn_ref[c0:c0 + cw, :])


SAMPLE_UNITS = 4
FF_CHUNKS_PROMPT = ((0, 768), (768, 768), (1536, 768), (2304, 512))


def _ffn_prompt_kernel(x_ref, g_ref, wup_ref, cw_ref, cb_ref, wdn_ref, gf_ref,
                       q_ref, kn_ref, vn_ref, w_ref, kt_hbm, vt_hbm,
                       y_ref, state_ref, o_ref, okt_hbm, ovt_hbm,
                       halo_ref, ink_ref, inv_ref, outk_ref, outv_ref, sem, *, n_tiles, n_steps):
    i = pl.program_id(1)
    step = pl.program_id(0) * n_tiles + i
    tm = x_ref.shape[0]
    d_ff = wdn_ref.shape[0]
    R = ink_ref.shape[1]
    halves = kt_hbm.shape[1] // R

    def copies(g, slot, inbound):
        n = g // halves
        rows = pl.ds(pl.multiple_of((g % halves) * R, R), R)
        if inbound:
            return (pltpu.make_async_copy(kt_hbm.at[n, rows, :], ink_ref.at[slot], sem.at[0, slot]),
                    pltpu.make_async_copy(vt_hbm.at[n, rows, :], inv_ref.at[slot], sem.at[1, slot]))
        return (pltpu.make_async_copy(outk_ref.at[slot], okt_hbm.at[n, rows, :], sem.at[2, slot]),
                pltpu.make_async_copy(outv_ref.at[slot], ovt_hbm.at[n, rows, :], sem.at[3, slot]))

    def start(g, slot, inbound):
        for cp in copies(g, slot, inbound):
            cp.start()

    def wait(g, slot, inbound):
        for cp in copies(g, slot, inbound):
            cp.wait()

    @pl.when(i == 0)
    def _():
        halo_ref[...] = jnp.zeros_like(halo_ref)

    @pl.when(step == 0)
    def _():
        start(0, 0, True)
        start(1, 1, True)

    x = x_ref[...]
    h = _rms(x, g_ref[...]).astype(BF16)
    y = x
    for u, (c0, cw) in enumerate(FF_CHUNKS_PROMPT):
        g = step * SAMPLE_UNITS + u
        slot = u % 2
        wait(g, slot, True)
        if u >= 2:
            wait(g - 2, slot, False)
        else:
            @pl.when(step > 0)
            def _():
                wait(g - 2, slot, False)

        row = lax.broadcasted_iota(jnp.int32, (tm, cw), 0)
        halo = halo_ref[:, c0:c0 + cw]

        def prev_rows(gate):
            g1 = jnp.where(row == 0, halo[1:2], pltpu.roll(gate, 1, 0))
            g2 = jnp.where(row == 0, halo[0:1], jnp.where(row == 1, halo[1:2], pltpu.roll(gate, 2, 0)))
            return g2, g1

        gate, contrib = _ffn_chunk(h, wup_ref, wdn_ref, cw_ref, cb_ref, d_ff, c0, cw, prev_rows)
        y = y + contrib
        halo_ref[0:2, c0:c0 + cw] = gate[tm - 2:, :]
        state_ref[:, c0:c0 + cw] = gate[tm - 2:, :]

        sn, half = u // halves, u % halves
        cols = slice(half * R, (half + 1) * R)
        o_ref[sn, :, cols] = _sample_a_unit(
            q_ref[sn, :, cols], kn_ref[sn, :, cols], vn_ref[sn, :, cols], ink_ref[slot], inv_ref[slot],
            w_ref[...], outk_ref.at[slot], outv_ref.at[slot])

        start(g, slot, False)
        if u < 2:
            start(g + 2, slot, True)
        else:
            @pl.when(step < n_steps - 1)
            def _():
                start(g + 2, slot, True)

    y_ref[...] = _rms(y, gf_ref[...])

    @pl.when(step == n_steps - 1)
    def _():
        last = n_steps * SAMPLE_UNITS
        wait(last - 2, 0, False)
        wait(last - 1, 1, False)


def _ffn_prompt(x, g, wup, cw, cb, wdn, gf, q, k_new, v_new, kt, vt, *, tm):
    B, S, D = x.shape
    d_ff = wdn.shape[0]
    N, W, L = kt.shape
    assert FF_CHUNKS_PROMPT[-1][0] + FF_CHUNKS_PROMPT[-1][1] == d_ff and len(FF_CHUNKS_PROMPT) == SAMPLE_UNITS
    assert L == A_WINDOW, "a cache shorter than the largest dilated window is unsupported"
    n_tiles = S // tm
    n_steps = B * n_tiles
    halves = 2
    seqs = SAMPLE_UNITS // halves
    assert N == seqs * n_steps, "the sample caches are spread evenly over the FFN grid steps"
    R = W // halves
    tok = pl.BlockSpec((None, tm, D), lambda b, i: (b, i, 0))
    const = lambda shape: pl.BlockSpec(shape, lambda b, i: (0,) * len(shape))
    resident = lambda shape: pl.BlockSpec(shape, lambda b, i: (0,) * len(shape), pipeline_mode=pl.Buffered(1))
    rows = pl.BlockSpec((seqs, 1, W), lambda b, i: (b * n_tiles + i, 0, 0))
    hbm = pl.BlockSpec(memory_space=pl.ANY)
    ring = pltpu.VMEM((2, R, L), F32)
    return pl.pallas_call(
        functools.partial(_ffn_prompt_kernel, n_tiles=n_tiles, n_steps=n_steps),
        grid=(B, n_tiles),
        in_specs=[tok, const((1, D)), resident(wup.shape), const(cw.shape), const(cb.shape),
                  resident(wdn.shape), const((1, D)),
                  rows, rows, rows, const((1, L)), hbm, hbm],
        out_specs=[tok, pl.BlockSpec((None, CONV_W - 1, d_ff), lambda b, i: (b, 0, 0)), rows, hbm, hbm],
        out_shape=[jax.ShapeDtypeStruct((B, S, D), F32),
                   jax.ShapeDtypeStruct((B, CONV_W - 1, d_ff), F32),
                   jax.ShapeDtypeStruct((N, 1, W), F32),
                   jax.ShapeDtypeStruct((N, W, L), F32), jax.ShapeDtypeStruct((N, W, L), F32)],
        scratch_shapes=[pltpu.VMEM((8, d_ff), F32), ring, ring, ring, ring, pltpu.SemaphoreType.DMA((4, 2))],
        compiler_params=_params("arbitrary", "arbitrary"),
        name="conv_ffn_prompt_sample_a",
    )(x, g, wup, cw, cb, wdn, gf, q, k_new, v_new, _branch_multiplicity(L), kt, vt)


def _ffn_sample_kernel(x_ref, s0_ref, s1_ref, g_ref, wup_ref, cw_ref, cb_ref, wdn_ref, gf_ref,
                       y_ref, gate_ref):
    d_ff = wdn_ref.shape[0]
    x = x_ref[...]
    h = _rms(x, g_ref[...]).astype(BF16)
    y = x
    for c0, cw in FF_CHUNKS:
        prev_rows = lambda gate: (s0_ref[:, c0:c0 + cw], s1_ref[:, c0:c0 + cw])
        gate, contrib = _ffn_chunk(h, wup_ref, wdn_ref, cw_ref, cb_ref, d_ff, c0, cw, prev_rows)
        y = y + contrib
        gate_ref[:, c0:c0 + cw] = gate
    y_ref[...] = _rms(y, gf_ref[...])


def _ffn_sample(x, s0, s1, g, wup, cw, cb, wdn, gf):
    N, D = x.shape
    d_ff = wdn.shape[0]
    return pl.pallas_call(
        _ffn_sample_kernel,
        out_shape=[jax.ShapeDtypeStruct((N, D), F32), jax.ShapeDtypeStruct((N, d_ff), F32)],
        compiler_params=pltpu.CompilerParams(vmem_limit_bytes=VMEM_LIMIT),
        name="conv_ffn_sample",
    )(x, s0, s1, g, wup, cw, cb, wdn, gf)


def _qb_pair_perm():
    cols = []
    for j in range(B_GROUP):
        for hk in range(B_KV_HEADS):
            h = hk * B_GROUP + j
            cols.extend(range(h * HEAD_DIM, (h + 1) * HEAD_DIM))
    return np.asarray(cols, dtype=np.int32)


def _layer_weights(l, g_mix, w_in, g_out_a, g_out_b, sinks, w_out, g_cross, g_mem, w_xq, w_mem_kv, w_xo,
                   g_ffn, w_up, conv_w, conv_b, w_down):
    perm = _qb_pair_perm()
    qb0 = 3 * A_W
    w_in_l = w_in[l]
    w_in_l = jnp.concatenate([w_in_l[:, :qb0], w_in_l[:, qb0:qb0 + B_W][:, perm], w_in_l[:, qb0 + B_W:]], axis=1)
    w_out_l = w_out[l]
    w_out_l = jnp.concatenate([w_out_l[:A_W], w_out_l[A_W:][perm]], axis=0)
    sink = sinks[l].astype(F32)
    pair = jnp.stack([sink[:B_GROUP], sink[B_GROUP:]], axis=1)
    sink_lanes = jnp.broadcast_to(jnp.repeat(pair, HEAD_DIM, axis=1)[:, None, :], (B_GROUP, 8, V7X_LANES))
    sink_rows = pair.reshape(2 * B_GROUP, 1)
    row = lambda v: v.reshape(1, -1).astype(F32)
    return dict(
        g_mix=row(g_mix[l]), w_in=w_in_l.astype(BF16),
        g_out_a=row(g_out_a[l]), g_out_b=row(g_out_b[l][perm]), w_out=w_out_l.astype(BF16),
        sink_lanes=sink_lanes, sink_rows=sink_rows,
        g_cross=row(g_cross[l]), g_mem=row(g_mem[l]), w_xq=w_xq[l].astype(BF16),
        w_mem_kv=w_mem_kv[l].astype(BF16), w_xo=w_xo[l].astype(BF16),
        g_ffn=row(g_ffn[l]), w_up=w_up[l].astype(BF16), conv_w=conv_w[l].astype(F32),
        conv_b=row(conv_b[l]), w_down=w_down[l].astype(BF16))


PROMPT_TM = 512
SAMPLE_NB = 8


def _prompt_layer(x, mem, W, g_final, cos, sin, sample_a):
    B, S, D = x.shape
    M = mem.shape[1]
    tm = min(PROMPT_TM, S)
    qa, ka, va, qb, kb, vb, ka_t, va_t, kb_t, vb_t = _inproj(
        x, W["g_mix"], W["w_in"], cos, sin, tm=tm, prompt=True)
    flat = lambda t: t.reshape(B, S, A_W)
    oa = _attn_a(flat(qa), flat(ka), flat(va)).reshape(B, R4, S // R4, A_W)
    ob = _attn_b(qb, kb, vb, W["sink_lanes"])
    mk, mv = _norm_matmul(mem.reshape(B * M, D), W["g_mem"], W["w_mem_kv"], n_out=2, tm=min(512, B * M))
    mk = mk.reshape(B, M, D)
    mv = mv.reshape(B, M, D)
    x2 = _outproj_cross(oa, ob, x, W["g_out_a"], W["g_out_b"], W["w_out"], W["g_cross"], W["w_xq"], mk, mv,
                        W["w_xo"], tm=tm)
    y, conv_state, oa_s, s_akt, s_avt = _ffn_prompt(
        x2, W["g_ffn"], W["w_up"], W["conv_w"], W["conv_b"], W["w_down"], g_final, *sample_a, tm=tm)
    return y, (ka_t, va_t, kb_t, vb_t, mk, mv, conv_state), (oa_s, s_akt, s_avt)


def _sample_inproj(x, W, cos, sin):
    N = x.shape[0]
    return [t[0] for t in _inproj(x[None], W["g_mix"], W["w_in"], cos, sin, tm=N, prompt=False)]


def _sample_layer(x, oa, qb, kb, vb, b_kt, b_vt, mem_k, mem_v, conv_state, W, g_final):
    N, D = x.shape
    ob, s_bkt, s_bvt = _sample_b(qb, kb, vb, b_kt, b_vt, W["sink_rows"], nb=SAMPLE_NB)
    x1 = _outproj(oa, ob, x, W["g_out_a"], W["g_out_b"], W["w_out"])
    (q,) = _norm_matmul(x1, W["g_cross"], W["w_xq"], n_out=1, tm=N)
    hd = D // X_HEADS
    halves = hd // V7X_LANES
    q = (q * (hd ** -0.5)).reshape(N, X_HEADS, halves, V7X_LANES).transpose(0, 2, 1, 3)
    o = _sample_cross(q.reshape(N, MEM_TILE_ROWS, V7X_LANES), mem_k, mem_v, nb=SAMPLE_NB)
    o = o.reshape(N, halves, X_HEADS, V7X_LANES).transpose(0, 2, 1, 3).reshape(N, D)
    x2 = _matmul_residual(o, W["w_xo"], x1, tm=N)
    y, gate = _ffn_sample(x2, conv_state[:, 0], conv_state[:, 1], W["g_ffn"], W["w_up"], W["conv_w"],
                          W["conv_b"], W["w_down"], g_final)
    new_conv = jnp.stack([conv_state[:, 1], gate], axis=1)
    return y, (s_bkt, s_bvt, new_conv)


def _time_minor(cache):
    N, L, H, Dh = cache.shape
    return cache.transpose(0, 2, 3, 1).reshape(N, H * Dh, L)


def _time_major(cache_t, H):
    N, W, L = cache_t.shape
    return cache_t.reshape(N, H, W // H, L).transpose(0, 3, 1, 2)


def kernel(x_prompt, x_sample, cache_a_k, cache_a_v, cache_b_k, cache_b_v, cache_mem_k, cache_mem_v, state_conv,
           mem_prompt, g_mix, w_in, g_out_a, g_out_b, sinks, w_out, g_cross, g_mem, w_xq, w_mem_kv, w_xo,
           g_ffn, w_up, conv_w, conv_b, w_down, g_final):
    depth = w_in.shape[0]
    assert depth == 1, "layer stacking is not wired up: the problem has a single layer"
    B, S, D = x_prompt.shape
    N, T, _ = x_sample.shape
    assert T == 1, "the sample group decodes one token per sequence"
    gf = g_final.reshape(1, D).astype(F32)
    cos_p, sin_p = _rope_tables(S, 0, 1)
    cos_s, sin_s = _rope_tables(N, PAST_LEN, 0)

    l = 0
    W = _layer_weights(l, g_mix, w_in, g_out_a, g_out_b, sinks, w_out, g_cross, g_mem, w_xq, w_mem_kv, w_xo,
                       g_ffn, w_up, conv_w, conv_b, w_down)

    xs = x_sample.reshape(N, D)
    qa, ka, va, qb, kb, vb = _sample_inproj(xs, W, cos_s, sin_s)
    sample_a = (qa[:, None, :], ka[:, None, :], va[:, None, :], _time_minor(cache_a_k[l]), _time_minor(cache_a_v[l]))
    yp, (p_ak, p_av, p_bk, p_bv, p_mk, p_mv, p_conv), (oa_s, s_akt, s_avt) = _prompt_layer(
        x_prompt, mem_prompt, W, gf, cos_p, sin_p, sample_a)
    ys, (s_bkt, s_bvt, s_conv) = _sample_layer(
        xs, oa_s.reshape(N, A_W), qb, kb, vb, _time_minor(cache_b_k[l]), _time_minor(cache_b_v[l]),
        _mem_tile_rows(cache_mem_k[l]), _mem_tile_rows(cache_mem_v[l]), state_conv[l], W, gf)

    la, lb = p_ak.shape[1], p_bk.shape[1]
    return (yp, ys.reshape(N, 1, D),
            p_ak.reshape(1, B, la, A_HEADS, HEAD_DIM), p_av.reshape(1, B, la, A_HEADS, HEAD_DIM),
            p_bk.reshape(1, B, lb, B_KV_HEADS, HEAD_DIM), p_bv.reshape(1, B, lb, B_KV_HEADS, HEAD_DIM),
            p_mk.reshape(1, B, -1, X_HEADS, D // X_HEADS), p_mv.reshape(1, B, -1, X_HEADS, D // X_HEADS),
            p_conv[None],
            _time_major(s_akt, A_HEADS)[None], _time_major(s_avt, A_HEADS)[None],
            _time_major(s_bkt, B_KV_HEADS)[None], _time_major(s_bvt, B_KV_HEADS)[None],
            s_conv[None])
```

```python
import functools

import jax
import jax.numpy as jnp
import numpy as np
from jax import lax
from jax.experimental import pallas as pl
from jax.experimental.pallas import tpu as pltpu

F32 = jnp.float32
BF16 = jnp.bfloat16

HEAD_DIM = 64
A_HEADS = 8
B_HEADS = 8
B_KV_HEADS = 2
B_GROUP = B_HEADS // B_KV_HEADS
A_W = A_HEADS * HEAD_DIM
B_W = B_HEADS * HEAD_DIM
B_KV_W = B_KV_HEADS * HEAD_DIM
DILATIONS = (1, 4, 16)
A_WINDOW = 2048
B_WINDOW = 128
BLOCK = 128
ROPE_THETA = 10000.0
PAST_LEN = 16384
X_HEADS = 4
CONV_W = 3
EPS = 1e-6
NEG = -1e30
LOG2E = 1.4426950408889634
ATTN_UNROLL = 32
R4 = 4

V7X_LANES = 128
V7X_MXU_DIM = 256
V7X_VMEM_BYTES = 64 * 1024 * 1024
VMEM_LIMIT = V7X_VMEM_BYTES - 8 * 1024 * 1024


def _params(*sem):
    return pltpu.CompilerParams(dimension_semantics=sem, vmem_limit_bytes=VMEM_LIMIT)


def _rms(x, g):
    return (x * lax.rsqrt(jnp.mean(x * x, axis=-1, keepdims=True) + EPS)) * g


def _dot(a, b):
    return jnp.dot(a, b, preferred_element_type=F32)


def _dot_t(a, b):
    return lax.dot_general(a, b, (((1,), (1,)), ((), ())), preferred_element_type=F32)


def _rope_table_kernel(inv_ref, cos_ref, sin_ref, *, pos0, pos_step):
    rows = cos_ref.shape[0]
    row = lax.broadcasted_iota(jnp.int32, (rows, V7X_LANES), 0)
    lane = lax.broadcasted_iota(jnp.int32, (rows, V7X_LANES), 1)
    ang = (pos0 + pos_step * row).astype(F32) * inv_ref[...]
    first_half = (lane % HEAD_DIM) < (HEAD_DIM // 2)
    cos_ref[...] = jnp.cos(ang)
    sin_ref[...] = jnp.where(first_half, -jnp.sin(ang), jnp.sin(ang))


def _rope_tables(rows, pos0, pos_step):
    half = HEAD_DIM // 2
    inv = jnp.power(ROPE_THETA, -jnp.arange(half, dtype=F32) / half)
    inv = jnp.tile(inv, V7X_LANES // half).reshape(1, V7X_LANES)
    return pl.pallas_call(
        functools.partial(_rope_table_kernel, pos0=pos0, pos_step=pos_step),
        out_shape=[jax.ShapeDtypeStruct((rows, V7X_LANES), F32)] * 2,
        name="rope_tables",
    )(inv)


def _rope(slab, cos, sin, first_half):
    partner = jnp.where(first_half, pltpu.roll(slab, 96, 1), pltpu.roll(slab, 32, 1))
    return slab * cos + partner * sin


def _inproj_kernel(x_ref, g_ref, w_ref, cos_ref, sin_ref,
                   qa_ref, ka_ref, va_ref, qb_ref, kb_ref, vb_ref, *rest,
                   tail_skip, n_tiles, prompt):
    tm = x_ref.shape[0]
    hn = _rms(x_ref[...], g_ref[...]).astype(BF16)
    cos = cos_ref[...]
    sin = sin_ref[...]
    lane = lax.broadcasted_iota(jnp.int32, (tm, V7X_LANES), 1)
    first_half = (lane % HEAD_DIM) < (HEAD_DIM // 2)

    def seg(c0, width):
        return _dot(hn, w_ref[:, c0:c0 + width])

    def roped(z):
        return jnp.concatenate(
            [_rope(z[:, c:c + V7X_LANES], cos, sin, first_half)
             for c in range(0, z.shape[1], V7X_LANES)], axis=1)

    if prompt:
        kat_ref, vat_ref, kbt_ref, vbt_ref, shuf_ref = rest

        def put_a(dst_ref, z):
            for c in range(A_W // V7X_LANES):
                shuf_ref[c] = z[:, c * V7X_LANES:(c + 1) * V7X_LANES]
            for c in range(A_W // V7X_LANES):
                for r in range(R4):
                    dst_ref[r, :, c * V7X_LANES:(c + 1) * V7X_LANES] = shuf_ref[c, pl.ds(r, tm // R4, stride=R4), :]
    else:
        def put_a(dst_ref, z):
            dst_ref[...] = z

    put_a(qa_ref, roped(seg(0, A_W)))
    ka = roped(seg(A_W, A_W))
    put_a(ka_ref, ka)
    va = seg(2 * A_W, A_W)
    put_a(va_ref, va)
    qb_ref[...] = roped(seg(3 * A_W, B_W))
    kvb = seg(3 * A_W + B_W, 2 * B_KV_W)
    kb = roped(kvb[:, :B_KV_W])
    vb = kvb[:, B_KV_W:]
    kb_ref[...] = kb
    vb_ref[...] = vb

    if prompt:
        i = pl.program_id(1)

        @pl.when(i >= tail_skip)
        def _():
            kat_ref[...] = ka
            vat_ref[...] = va

        @pl.when(i == n_tiles - 1)
        def _():
            kbt_ref[...] = kb[tm - B_WINDOW:, :]
            vbt_ref[...] = vb[tm - B_WINDOW:, :]


def _inproj(x, g, w, cos, sin, *, tm, prompt):
    B, S, D = x.shape
    n_tiles = S // tm
    la = min(A_WINDOW, S)
    lb = min(B_WINDOW, S)
    tail_skip = (S - la) // tm
    tok = lambda w_: pl.BlockSpec((None, tm, w_), lambda b, i: (b, i, 0))
    const = lambda shape: pl.BlockSpec(shape, lambda b, i: (0,) * len(shape))
    if prompt:
        a_shape = jax.ShapeDtypeStruct((B, R4, S // R4, A_W), F32)
        a_spec = pl.BlockSpec((None, R4, tm // R4, A_W), lambda b, i: (b, 0, i, 0))
    else:
        a_shape = jax.ShapeDtypeStruct((B, S, A_W), F32)
        a_spec = tok(A_W)
    out_shape = [a_shape] * 3 + [
        jax.ShapeDtypeStruct((B, S, B_W), F32),
        jax.ShapeDtypeStruct((B, S, B_KV_W), F32),
        jax.ShapeDtypeStruct((B, S, B_KV_W), F32)]
    out_specs = [a_spec] * 3 + [tok(B_W), tok(B_KV_W), tok(B_KV_W)]
    scratch = []
    if prompt:
        out_shape += [jax.ShapeDtypeStruct((B, la, A_W), F32)] * 2
        out_shape += [jax.ShapeDtypeStruct((B, lb, B_KV_W), F32)] * 2
        a_tail = pl.BlockSpec((None, tm, A_W), lambda b, i: (b, jnp.maximum(i - tail_skip, 0), 0))
        b_tail = pl.BlockSpec((None, lb, B_KV_W), lambda b, i: (b, 0, 0))
        out_specs += [a_tail, a_tail, b_tail, b_tail]
        scratch = [pltpu.VMEM((A_W // V7X_LANES, tm, V7X_LANES), F32)]
    return pl.pallas_call(
        functools.partial(_inproj_kernel, tail_skip=tail_skip, n_tiles=n_tiles, prompt=prompt),
        grid=(B, n_tiles),
        in_specs=[tok(D), const((1, D)), const(w.shape),
                  pl.BlockSpec((tm, V7X_LANES), lambda b, i: (i, 0)),
                  pl.BlockSpec((tm, V7X_LANES), lambda b, i: (i, 0))],
        out_specs=out_specs,
        out_shape=out_shape,
        scratch_shapes=scratch,
        compiler_params=_params("arbitrary", "arbitrary"),
        name="inproj_rope",
    )(x, g, w, cos, sin)


def _band_bias(first, chunked):
    a = lax.broadcasted_iota(jnp.int32, (2 * BLOCK, 2 * BLOCK), 0) % BLOCK
    b = lax.broadcasted_iota(jnp.int32, (2 * BLOCK, 2 * BLOCK), 1)
    own = b >= BLOCK
    bb = b % BLOCK
    if chunked:
        sub = BLOCK // R4
        a = R4 * (a % sub) + a // sub
        bb = R4 * (bb % sub) + bb // sub
    dist = BLOCK + a - (bb + jnp.where(own, BLOCK, 0))
    valid = (dist >= 0) & (dist <= BLOCK)
    if first:
        valid = valid & own
    return jnp.where(valid, 0.0, NEG).astype(F32)


def _two_head_block(q, kk, vv, bias):
    lane = lax.broadcasted_iota(jnp.int32, (BLOCK, V7X_LANES), 1)
    lo = lane < HEAD_DIM
    q = q * (HEAD_DIM ** -0.5 * LOG2E)
    qs = jnp.concatenate([jnp.where(lo, q, 0.0), jnp.where(lo, 0.0, q)], axis=0).astype(BF16)
    s = _dot_t(qs, kk) + bias
    m = jnp.max(s, axis=-1, keepdims=True)
    p = jnp.exp2(s - m)
    v1 = jnp.concatenate([vv, jnp.ones_like(vv)], axis=1)
    pv = _dot(p.astype(BF16), v1)
    acc = jnp.where(lo, pv[:BLOCK, :V7X_LANES], pv[BLOCK:, :V7X_LANES])
    l2 = jnp.where(lo, pv[:BLOCK, V7X_LANES:], pv[BLOCK:, V7X_LANES:])
    m2 = jnp.where(lo, m[:BLOCK], m[BLOCK:])
    return acc, m2, l2


def _get(ref, slices):
    return jnp.concatenate([ref[sl, :] for sl in slices], axis=0) if len(slices) > 1 else ref[slices[0], :]


def _put(ref, slices, val):
    off = 0
    for sl in slices:
        ref[sl, :] = val[off:off + sl.size]
        off += sl.size


def _block_slices(layout, d, S, i):
    if layout == "seq":
        start = pl.multiple_of(i * BLOCK, BLOCK)
        prev = pl.multiple_of(jnp.maximum(i - 1, 0) * BLOCK, BLOCK)
        return [pl.ds(start, BLOCK)], [pl.ds(prev, BLOCK)], i == 0
    Sr = S // R4
    if d == 1:
        sub = BLOCK // R4
        pj = jnp.maximum(i - 1, 0)
        own = [pl.ds(pl.multiple_of(r * Sr + i * sub, sub), sub) for r in range(R4)]
        prev = [pl.ds(pl.multiple_of(r * Sr + pj * sub, sub), sub) for r in range(R4)]
        return own, prev, i == 0
    if d == R4:
        per_res = Sr // BLOCK
        jb = i % per_res
        start = pl.multiple_of(i * BLOCK, BLOCK)
        prev = pl.multiple_of(jnp.where(jb == 0, i, i - 1) * BLOCK, BLOCK)
        return [pl.ds(start, BLOCK)], [pl.ds(prev, BLOCK)], jb == 0
    step = d // R4
    per_res = S // (BLOCK * d)
    rd = i // per_res
    jb = i % per_res
    start = (rd % R4) * Sr + rd // R4 + jb * (BLOCK * step)
    prev = jnp.where(jb == 0, start, start - BLOCK * step)
    return [pl.ds(start, BLOCK, stride=step)], [pl.ds(prev, BLOCK, stride=step)], jb == 0


def _branch_blocks(q_ref, k_ref, v_ref, bias_ref, layout, d, visit):
    S = q_ref.shape[0]
    bias_base = 2 if (layout == "r4" and d == 1) else 0

    def body(i, carry):
        own, prev, first = _block_slices(layout, d, S, i)
        kk = jnp.concatenate([_get(k_ref, prev), _get(k_ref, own)], axis=0).astype(BF16)
        vv = jnp.concatenate([_get(v_ref, prev), _get(v_ref, own)], axis=0).astype(BF16)
        bias = bias_ref[bias_base + jnp.where(first, 1, 0)]
        acc, m, l = _two_head_block(_get(q_ref, own), kk, vv, bias)
        visit(own, acc, m, l)
        return carry

    lax.fori_loop(0, S // BLOCK, body, 0, unroll=ATTN_UNROLL)


def _init_bias(bias_ref, chunked_too):
    bias_ref[0] = _band_bias(False, False)
    bias_ref[1] = _band_bias(True, False)
    if chunked_too:
        bias_ref[2] = _band_bias(False, True)
        bias_ref[3] = _band_bias(True, True)


def _attn_a_kernel(q_ref, k_ref, v_ref, o_ref, m_ref, l_ref, bias_ref):
    _init_bias(bias_ref, True)

    def first_visit(rows, acc, m, l):
        _put(o_ref, rows, acc)
        _put(m_ref, rows, m)
        _put(l_ref, rows, l)

    def merged(rows, acc, m, l):
        m_old = _get(m_ref, rows)
        m_new = jnp.maximum(m_old, m)
        a_old = jnp.exp2(m_old - m_new)
        a_new = jnp.exp2(m - m_new)
        return (a_old * _get(o_ref, rows) + a_new * acc, m_new, a_old * _get(l_ref, rows) + a_new * l)

    def mid_visit(rows, acc, m, l):
        acc, m, l = merged(rows, acc, m, l)
        _put(o_ref, rows, acc)
        _put(m_ref, rows, m)
        _put(l_ref, rows, l)

    def last_visit(rows, acc, m, l):
        acc, m, l = merged(rows, acc, m, l)
        _put(o_ref, rows, acc / l)

    order = (R4, 1, 16)
    assert sorted(order) == sorted(DILATIONS)
    visits = [first_visit] + [mid_visit] * (len(order) - 2) + [last_visit]
    for d, visit in zip(order, visits):
        _branch_blocks(q_ref, k_ref, v_ref, bias_ref, "r4", d, visit)


def _attn_b_kernel(q_ref, k_ref, v_ref, sink_ref, o_ref, bias_ref):
    _init_bias(bias_ref, False)
    sink = sink_ref[0:1, :] * LOG2E

    def visit(rows, acc, m, l):
        _put(o_ref, rows, acc / (l + jnp.exp2(sink - m)))

    _branch_blocks(q_ref, k_ref, v_ref, bias_ref, "seq", 1, visit)


def _attn_a(q, k, v):
    B, S, W = q.shape
    slab = pl.BlockSpec((None, S, V7X_LANES), lambda b, j: (b, 0, j))
    return pl.pallas_call(
        _attn_a_kernel,
        grid=(B, W // V7X_LANES),
        in_specs=[slab, slab, slab],
        out_specs=slab,
        out_shape=jax.ShapeDtypeStruct((B, S, W), F32),
        scratch_shapes=[pltpu.VMEM((S, V7X_LANES), F32), pltpu.VMEM((S, V7X_LANES), F32),
                        pltpu.VMEM((4, 2 * BLOCK, 2 * BLOCK), F32)],
        compiler_params=_params("arbitrary", "arbitrary"),
        name="mixer_a_attention",
    )(q, k, v)


def _attn_b(q, k, v, sink_lanes):
    B, S, W = q.shape
    slab = pl.BlockSpec((None, S, V7X_LANES), lambda b, j: (b, 0, j))
    kv = pl.BlockSpec((None, S, V7X_LANES), lambda b, j: (b, 0, 0))
    return pl.pallas_call(
        _attn_b_kernel,
        grid=(B, W // V7X_LANES),
        in_specs=[slab, kv, kv, pl.BlockSpec((None, 8, V7X_LANES), lambda b, j: (j, 0, 0))],
        out_specs=slab,
        out_shape=jax.ShapeDtypeStruct((B, S, W), F32),
        scratch_shapes=[pltpu.VMEM((2, 2 * BLOCK, 2 * BLOCK), F32)],
        compiler_params=_params("arbitrary", "arbitrary"),
        name="mixer_b_attention",
    )(q, k, v, sink_lanes)


def _as_column(row):
    return jnp.broadcast_to(row, (V7X_LANES, row.shape[1])).T


def _roll_in(dst_ref, cache, new_col):
    L = cache.shape[1]
    rolled = pltpu.roll(cache, L - 1, 1)
    lane = lax.broadcasted_iota(jnp.int32, (cache.shape[0], V7X_LANES), 1)
    if L > V7X_LANES:
        dst_ref[:, :L - V7X_LANES] = rolled[:, :L - V7X_LANES]
    dst_ref[:, L - V7X_LANES:] = jnp.where(lane == V7X_LANES - 1, new_col, rolled[:, L - V7X_LANES:])


def _sample_a_unit(q, k_new, v_new, kt, vt, w, okt_ref, ovt_ref):
    R, L = kt.shape
    H = R // HEAD_DIM
    n_br = len(DILATIONS)
    qc = _as_column(q * (HEAD_DIM ** -0.5))
    knc = _as_column(k_new)
    vnc = _as_column(v_new)
    prod = jnp.concatenate([kt[:, j:j + V7X_LANES] * qc for j in range(0, L, V7X_LANES)], axis=1)
    s = jnp.sum(prod.reshape(H, HEAD_DIM, L), axis=1)
    s_new = jnp.sum((qc * knc).reshape(H, HEAD_DIM, V7X_LANES), axis=1)[:, 0:1]
    s = jnp.where(w > 0, s, NEG)
    m = jnp.maximum(jnp.max(s, axis=-1, keepdims=True), s_new)
    p = w * jnp.exp(s - m)
    p_new = n_br * jnp.exp(s_new - m)
    l = jnp.sum(p, axis=-1, keepdims=True) + p_new

    def per_row(a):
        return jnp.broadcast_to(a.reshape(H, 1, a.shape[1]), (H, HEAD_DIM, a.shape[1])).reshape(R, a.shape[1])

    o_col = (jnp.sum(vt * per_row(p), axis=-1, keepdims=True) + per_row(p_new) * vnc[:, 0:1]) / per_row(l)
    _roll_in(okt_ref, kt, knc)
    _roll_in(ovt_ref, vt, vnc)
    return jnp.broadcast_to(o_col, (R, V7X_LANES)).T[0:1, :]


def _branch_multiplicity(L):
    dist = L - np.arange(L)
    w = sum(((dist % d == 0) & (dist <= BLOCK * d)).astype(np.float32) for d in DILATIONS)
    return jnp.asarray(w.reshape(1, L))


CACHE_HALVES = 2


class _CacheStream:
    def __init__(self, step, n_steps, per_step, unit0, hbm, ring, sem):
        self.step, self.per_step, self.unit0 = step, per_step, unit0
        self.total = n_steps * per_step
        self.n_steps = n_steps
        self.kt_hbm, self.vt_hbm, self.okt_hbm, self.ovt_hbm = hbm
        self.ink, self.inv, self.outk, self.outv = ring
        self.sem = sem
        self.R = self.ink.shape[1]

    def _copies(self, g, inbound):
        unit = self.unit0 + g
        slot = g % 2
        n = unit // CACHE_HALVES
        rows = pl.ds(pl.multiple_of((unit % CACHE_HALVES) * self.R, self.R), self.R)
        if inbound:
            return (pltpu.make_async_copy(self.kt_hbm.at[n, rows, :], self.ink.at[slot], self.sem.at[0, slot]),
                    pltpu.make_async_copy(self.vt_hbm.at[n, rows, :], self.inv.at[slot], self.sem.at[1, slot]))
        return (pltpu.make_async_copy(self.outk.at[slot], self.okt_hbm.at[n, rows, :], self.sem.at[2, slot]),
                pltpu.make_async_copy(self.outv.at[slot], self.ovt_hbm.at[n, rows, :], self.sem.at[3, slot]))

    def _start(self, g, inbound):
        for cp in self._copies(g, inbound):
            cp.start()

    def _wait(self, g, inbound):
        for cp in self._copies(g, inbound):
            cp.wait()

    def prologue(self):
        @pl.when(self.step == 0)
        def _():
            for g in range(min(2, self.total)):
                self._start(g, True)

    def unit(self, u, q_ref, kn_ref, vn_ref, w_ref, o_ref):
        g = self.step * self.per_step + u
        slot = g % 2
        self._wait(g, True)

        @pl.when(g >= 2)
        def _():
            self._wait(g - 2, False)

        o_ref[g] = _sample_a_unit(q_ref[g], kn_ref[g], vn_ref[g], self.ink[slot], self.inv[slot], w_ref[...],
                                  self.outk.at[slot], self.outv.at[slot])
        self._start(g, False)

        @pl.when(g + 2 < self.total)
        def _():
            self._start(g + 2, True)

    def epilogue(self):
        @pl.when(self.step == self.n_steps - 1)
        def _():
            for g in range(max(self.total - 2, 0), self.total):
                self._wait(g, False)


def _stream_specs(n_units, R, L):
    rows = pl.BlockSpec((n_units, 1, R), lambda *_: (0, 0, 0))
    hbm = pl.BlockSpec(memory_space=pl.ANY)
    ring = pltpu.VMEM((2, R, L), F32)
    in_specs = [rows, rows, rows, pl.BlockSpec((1, L), lambda *_: (0, 0)), hbm, hbm]
    out_specs = [rows, hbm, hbm]
    scratch = [ring, ring, ring, ring, pltpu.SemaphoreType.DMA((4, 2))]
    return in_specs, out_specs, scratch


def _sample_b_kernel(q_ref, kn_ref, vn_ref, kt_ref, vt_ref, sink_ref, o_ref, okt_ref, ovt_ref):
    nb = q_ref.shape[0]
    lane = lax.broadcasted_iota(jnp.int32, (1, V7X_LANES), 1)
    lo = lane < HEAD_DIM
    for n in range(nb):
        q = q_ref[n:n + 1, :] * (HEAD_DIM ** -0.5)
        rows = []
        for j in range(B_W // V7X_LANES):
            pair = q[:, j * V7X_LANES:(j + 1) * V7X_LANES]
            rows += [jnp.where(lo, pair, 0.0), jnp.where(lo, 0.0, pair)]
        qh = jnp.concatenate(rows, axis=0)
        kt = kt_ref[n]
        vt = vt_ref[n]
        s_new = jnp.sum(qh * kn_ref[n:n + 1, :], axis=-1, keepdims=True)
        s = _dot(qh.astype(BF16), kt.astype(BF16))
        m = jnp.maximum(s_new, jnp.max(s, axis=-1, keepdims=True))
        p_new = jnp.exp(s_new - m)
        p = jnp.exp(s - m)
        l = p_new + jnp.sum(p, axis=-1, keepdims=True)
        o = p_new * vn_ref[n:n + 1, :] + _dot_t(p.astype(BF16), vt.astype(BF16))
        o = o / (l + jnp.exp(sink_ref[...] - m))
        o_ref[n:n + 1, :] = jnp.concatenate(
            [jnp.where(lo, o[2 * j:2 * j + 1], o[2 * j + 1:2 * j + 2]) for j in range(B_W // V7X_LANES)], axis=1)
        _roll_in(okt_ref.at[n], kt, _as_column(kn_ref[n:n + 1, :]))
        _roll_in(ovt_ref.at[n], vt, _as_column(vn_ref[n:n + 1, :]))


def _sample_b(q, k_new, v_new, kt, vt, sink_rows, *, nb):
    N, W, L = kt.shape
    assert L == B_WINDOW and W == V7X_LANES, "a cache shorter than the sliding window is unsupported"
    row = lambda w_: pl.BlockSpec((nb, w_), lambda i: (i, 0))
    cache = pl.BlockSpec((nb, W, L), lambda i: (i, 0, 0))
    return pl.pallas_call(
        _sample_b_kernel,
        grid=(N // nb,),
        in_specs=[row(B_W), row(W), row(W), cache, cache, pl.BlockSpec((8, 1), lambda i: (0, 0))],
        out_specs=[row(B_W), cache, cache],
        out_shape=[jax.ShapeDtypeStruct((N, B_W), F32), jax.ShapeDtypeStruct((N, W, L), F32),
                   jax.ShapeDtypeStruct((N, W, L), F32)],
        compiler_params=_params("arbitrary"),
        name="sample_mixer_b",
    )(q, k_new, v_new, kt, vt, sink_rows)


def _mixers_out(oa, ob, x, ga_ref, gb_ref, w_ref):
    ha = _rms(oa, ga_ref[...]).astype(BF16)
    hb = _rms(ob, gb_ref[...]).astype(BF16)
    return x + _dot(ha, w_ref[:A_W, :]) + _dot(hb, w_ref[A_W:, :])


def _outproj_kernel(oa_ref, ob_ref, x_ref, ga_ref, gb_ref, w_ref, y_ref):
    y_ref[...] = _mixers_out(oa_ref[...], ob_ref[...], x_ref[...], ga_ref, gb_ref, w_ref)


def _outproj(oa, ob, x, ga, gb, w):
    return pl.pallas_call(
        _outproj_kernel,
        out_shape=jax.ShapeDtypeStruct(x.shape, F32),
        compiler_params=pltpu.CompilerParams(vmem_limit_bytes=VMEM_LIMIT),
        name="mixers_outproj",
    )(oa, ob, x, ga, gb, w)


def _norm_matmul_kernel(x_ref, g_ref, w_ref, *out_refs):
    h = _rms(x_ref[...], g_ref[...]).astype(BF16)
    width = out_refs[0].shape[1]
    for j, o_ref in enumerate(out_refs):
        o_ref[...] = _dot(h, w_ref[:, j * width:(j + 1) * width])


def _norm_matmul(x, g, w, *, n_out, tm):
    T, D = x.shape
    width = w.shape[1] // n_out
    return pl.pallas_call(
        _norm_matmul_kernel,
        grid=(T // tm,),
        in_specs=[pl.BlockSpec((tm, D), lambda i: (i, 0)), pl.BlockSpec((1, D), lambda i: (0, 0)),
                  pl.BlockSpec(w.shape, lambda i: (0, 0))],
        out_specs=[pl.BlockSpec((tm, width), lambda i: (i, 0))] * n_out,
        out_shape=[jax.ShapeDtypeStruct((T, width), F32)] * n_out,
        compiler_params=_params("arbitrary"),
        name="norm_matmul",
    )(x, g, w)


def _matmul_residual_kernel(a_ref, w_ref, r_ref, y_ref):
    y_ref[...] = r_ref[...] + _dot(a_ref[...].astype(BF16), w_ref[...])


def _matmul_residual(a, w, r, *, tm):
    T, K = a.shape
    D = w.shape[1]
    return pl.pallas_call(
        _matmul_residual_kernel,
        grid=(T // tm,),
        in_specs=[pl.BlockSpec((tm, K), lambda i: (i, 0)), pl.BlockSpec(w.shape, lambda i: (0, 0)),
                  pl.BlockSpec((tm, D), lambda i: (i, 0))],
        out_specs=pl.BlockSpec((tm, D), lambda i: (i, 0)),
        out_shape=jax.ShapeDtypeStruct((T, D), F32),
        compiler_params=_params("arbitrary"),
        name="matmul_residual",
    )(a, w, r)


CROSS_UNITS = 1


def _outproj_cross_kernel(oa_ref, ob_ref, x_ref, ga_ref, gb_ref, wout_ref, gc_ref, wq_ref, mk_ref, mv_ref, wo_ref,
                          sq_ref, skn_ref, svn_ref, sw_ref, kt_hbm, vt_hbm,
                          y_ref, so_ref, okt_hbm, ovt_hbm,
                          shuf_ref, q_ref, ink_ref, inv_ref, outk_ref, outv_ref, sem, *, n_tiles, n_steps):
    step = pl.program_id(0) * n_tiles + pl.program_id(1)
    stream = _CacheStream(step, n_steps, CROSS_UNITS, 0, (kt_hbm, vt_hbm, okt_hbm, ovt_hbm),
                          (ink_ref, inv_ref, outk_ref, outv_ref), sem)
    stream.prologue()
    tq = oa_ref.shape[1]
    for c in range(A_W // V7X_LANES):
        for r in range(R4):
            shuf_ref[c, pl.ds(r, tq, stride=R4), :] = oa_ref[r, :, c * V7X_LANES:(c + 1) * V7X_LANES]
    oa = jnp.concatenate([shuf_ref[c] for c in range(A_W // V7X_LANES)], axis=1)
    y_ref[...] = _mixers_out(oa, ob_ref[...], x_ref[...], ga_ref, gb_ref, wout_ref)
    D = y_ref.shape[1]
    hd = D // X_HEADS
    q_ref[...] = (_dot(_rms(y_ref[...], gc_ref[...]).astype(BF16), wq_ref[...]) * (hd ** -0.5)).astype(BF16)
    for u in range(CROSS_UNITS):
        stream.unit(u, sq_ref, skn_ref, svn_ref, sw_ref, so_ref)
    for h in range(X_HEADS):
        cols = slice(h * hd, (h + 1) * hd)
        s = _dot_t(q_ref[:, cols], mk_ref[:, cols].astype(BF16))
        p = jnp.exp(s - jnp.max(s, axis=-1, keepdims=True))
        o = _dot(p.astype(BF16), mv_ref[:, cols].astype(BF16)) / jnp.sum(p, axis=-1, keepdims=True)
        y_ref[...] += _dot(o.astype(BF16), wo_ref[cols, :])
    stream.epilogue()


def _outproj_cross(oa, ob, x, ga, gb, wout, gc, wq, mk, mv, wo, sq, skn, svn, kt, vt, *, tm):
    B, S, D = x.shape
    M = mk.shape[1]
    N, W, L = kt.shape
    n_tiles = S // tm
    n_steps = B * n_tiles
    n_units = n_steps * CROSS_UNITS
    R = W // CACHE_HALVES
    assert sq.shape == (n_units, 1, R)
    tok = lambda w_: pl.BlockSpec((None, tm, w_), lambda b, i: (b, i, 0))
    mem = pl.BlockSpec((None, M, D), lambda b, i: (b, 0, 0))
    const = lambda shape: pl.BlockSpec(shape, lambda b, i: (0,) * len(shape))
    s_in, s_out, s_scratch = _stream_specs(n_units, R, L)
    return pl.pallas_call(
        functools.partial(_outproj_cross_kernel, n_tiles=n_tiles, n_steps=n_steps),
        grid=(B, n_tiles),
        in_specs=[pl.BlockSpec((None, R4, tm // R4, A_W), lambda b, i: (b, 0, i, 0)), tok(B_W), tok(D),
                  const((1, A_W)), const((1, B_W)), const(wout.shape), const((1, D)), const(wq.shape),
                  mem, mem, const(wo.shape)] + s_in,
        out_specs=[tok(D)] + s_out,
        out_shape=[jax.ShapeDtypeStruct((B, S, D), F32), jax.ShapeDtypeStruct((n_units, 1, R), F32),
                   jax.ShapeDtypeStruct((N, W, L), F32), jax.ShapeDtypeStruct((N, W, L), F32)],
        scratch_shapes=[pltpu.VMEM((A_W // V7X_LANES, tm, V7X_LANES), F32), pltpu.VMEM((tm, D), BF16)] + s_scratch,
        compiler_params=_params("arbitrary", "arbitrary"),
        name="outproj_cross_attention_sample_a",
    )(oa, ob, x, ga, gb, wout, gc, wq, mk, mv, wo, sq, skn, svn, _branch_multiplicity(L), kt, vt)


MEM_TILE_ROWS = 8


def _sample_cross_kernel(q_ref, mk_ref, mv_ref, o_ref):
    nb = q_ref.shape[0]
    M = mk_ref.shape[1] // MEM_TILE_ROWS
    for n in range(nb):
        q = q_ref[n]
        k3 = mk_ref[n].reshape(M, MEM_TILE_ROWS, V7X_LANES)
        v3 = mv_ref[n].reshape(M, MEM_TILE_ROWS, V7X_LANES)
        part = jnp.sum(k3 * q[None], axis=-1, keepdims=True)
        s = part + pltpu.roll(part, X_HEADS, 1)
        p = jnp.exp(s - jnp.max(s, axis=0, keepdims=True))
        o = jnp.sum(v3 * p, axis=0) / jnp.sum(p, axis=0)
        o_ref[n] = o


def _sample_cross(q, mk, mv, *, nb):
    N, R, _ = mk.shape
    mem = pl.BlockSpec((nb, R, V7X_LANES), lambda i: (i, 0, 0))
    row = pl.BlockSpec((nb, MEM_TILE_ROWS, V7X_LANES), lambda i: (i, 0, 0))
    return pl.pallas_call(
        _sample_cross_kernel,
        grid=(N // nb,),
        in_specs=[row, mem, mem],
        out_specs=row,
        out_shape=jax.ShapeDtypeStruct((N, MEM_TILE_ROWS, V7X_LANES), F32),
        compiler_params=_params("arbitrary"),
        name="sample_cross_attention",
    )(q, mk, mv)


def _mem_tile_rows(cache):
    N, M, H, Dh = cache.shape
    halves = Dh // V7X_LANES
    assert H * halves == MEM_TILE_ROWS
    return cache.reshape(N, M, H, halves, V7X_LANES).transpose(0, 1, 3, 2, 4).reshape(N, M * MEM_TILE_ROWS, V7X_LANES)


FF_CHUNKS = ((0, 1024), (1024, 1024), (2048, 768))


def _ffn_chunk(h, wup_ref, wdn_ref, cw_ref, cb_ref, d_ff, c0, cw, prev_rows):
    gate = _dot(h, wup_ref[:, c0:c0 + cw])
    val = _dot(h, wup_ref[:, d_ff + c0:d_ff + c0 + cw])
    g2, g1 = prev_rows(gate)
    conv = cb_ref[:, c0:c0 + cw] + (g2 * cw_ref[0:1, c0:c0 + cw] + g1 * cw_ref[1:2, c0:c0 + cw]
                                     + gate * cw_ref[2:3, c0:c0 + cw])
    act = (conv * (1.0 / (1.0 + jnp.exp(-conv)))) * val
    return gate, _dot(act.astype(BF16), wdn_ref[c0:c0 + cw, :])


FFN_UNITS = 3
FF_CHUNKS_PROMPT = ((0, 768), (768, 768), (1536, 768), (2304, 512))


def _ffn_prompt_kernel(x_ref, g_ref, wup_ref, cw_ref, cb_ref, wdn_ref, gf_ref,
                       sq_ref, skn_ref, svn_ref, sw_ref, kt_hbm, vt_hbm, okt_in, ovt_in,
                       y_ref, state_ref, so_ref, okt_hbm, ovt_hbm,
                       halo_ref, ink_ref, inv_ref, outk_ref, outv_ref, sem, *, n_tiles, n_steps, unit0):
    del okt_in, ovt_in
    i = pl.program_id(1)
    step = pl.program_id(0) * n_tiles + i
    tm = x_ref.shape[0]
    d_ff = wdn_ref.shape[0]
    stream = _CacheStream(step, n_steps, FFN_UNITS, unit0, (kt_hbm, vt_hbm, okt_hbm, ovt_hbm),
                          (ink_ref, inv_ref, outk_ref, outv_ref), sem)

    @pl.when(i == 0)
    def _():
        halo_ref[...] = jnp.zeros_like(halo_ref)

    stream.prologue()
    x = x_ref[...]
    h = _rms(x, g_ref[...]).astype(BF16)
    y = x
    for u, (c0, cw) in enumerate(FF_CHUNKS_PROMPT):
        row = lax.broadcasted_iota(jnp.int32, (tm, cw), 0)
        halo = halo_ref[:, c0:c0 + cw]

        def prev_rows(gate):
            g1 = jnp.where(row == 0, halo[1:2], pltpu.roll(gate, 1, 0))
            g2 = jnp.where(row == 0, halo[0:1], jnp.where(row == 1, halo[1:2], pltpu.roll(gate, 2, 0)))
            return g2, g1

        gate, contrib = _ffn_chunk(h, wup_ref, wdn_ref, cw_ref, cb_ref, d_ff, c0, cw, prev_rows)
        y = y + contrib
        halo_ref[0:2, c0:c0 + cw] = gate[tm - 2:, :]
        state_ref[:, c0:c0 + cw] = gate[tm - 2:, :]
        if u < FFN_UNITS:
            stream.unit(u, sq_ref, skn_ref, svn_ref, sw_ref, so_ref)

    y_ref[...] = _rms(y, gf_ref[...])
    stream.epilogue()


def _ffn_prompt(x, g, wup, cw, cb, wdn, gf, sq, skn, svn, kt, vt, okt, ovt, *, unit0, tm):
    B, S, D = x.shape
    d_ff = wdn.shape[0]
    N, W, L = kt.shape
    assert FF_CHUNKS_PROMPT[-1][0] + FF_CHUNKS_PROMPT[-1][1] == d_ff and len(FF_CHUNKS_PROMPT) >= FFN_UNITS
    assert L == A_WINDOW, "a cache shorter than the largest dilated window is unsupported"
    n_tiles = S // tm
    n_steps = B * n_tiles
    n_units = n_steps * FFN_UNITS
    R = W // CACHE_HALVES
    assert sq.shape == (n_units, 1, R) and unit0 + n_units == N * CACHE_HALVES
    tok = pl.BlockSpec((None, tm, D), lambda b, i: (b, i, 0))
    const = lambda shape: pl.BlockSpec(shape, lambda b, i: (0,) * len(shape))
    resident = lambda shape: pl.BlockSpec(shape, lambda b, i: (0,) * len(shape), pipeline_mode=pl.Buffered(1))
    hbm = pl.BlockSpec(memory_space=pl.ANY)
    s_in, s_out, s_scratch = _stream_specs(n_units, R, L)
    n_in = 7 + len(s_in)
    return pl.pallas_call(
        functools.partial(_ffn_prompt_kernel, n_tiles=n_tiles, n_steps=n_steps, unit0=unit0),
        grid=(B, n_tiles),
        in_specs=[tok, const((1, D)), resident(wup.shape), const(cw.shape), const(cb.shape),
                  resident(wdn.shape), const((1, D))] + s_in + [hbm, hbm],
        out_specs=[tok, pl.BlockSpec((None, CONV_W - 1, d_ff), lambda b, i: (b, 0, 0))] + s_out,
        out_shape=[jax.ShapeDtypeStruct((B, S, D), F32),
                   jax.ShapeDtypeStruct((B, CONV_W - 1, d_ff), F32),
                   jax.ShapeDtypeStruct((n_units, 1, R), F32),
                   jax.ShapeDtypeStruct((N, W, L), F32), jax.ShapeDtypeStruct((N, W, L), F32)],
        scratch_shapes=[pltpu.VMEM((8, d_ff), F32)] + s_scratch,
        input_output_aliases={n_in: 3, n_in + 1: 4},
        compiler_params=_params("arbitrary", "arbitrary"),
        name="conv_ffn_prompt_sample_a",
    )(x, g, wup, cw, cb, wdn, gf, sq, skn, svn, _branch_multiplicity(L), kt, vt, okt, ovt)


def _ffn_sample_kernel(x_ref, s0_ref, s1_ref, g_ref, wup_ref, cw_ref, cb_ref, wdn_ref, gf_ref,
                       y_ref, gate_ref):
    d_ff = wdn_ref.shape[0]
    x = x_ref[...]
    h = _rms(x, g_ref[...]).astype(BF16)
    y = x
    for c0, cw in FF_CHUNKS:
        prev_rows = lambda gate: (s0_ref[:, c0:c0 + cw], s1_ref[:, c0:c0 + cw])
        gate, contrib = _ffn_chunk(h, wup_ref, wdn_ref, cw_ref, cb_ref, d_ff, c0, cw, prev_rows)
        y = y + contrib
        gate_ref[:, c0:c0 + cw] = gate
    y_ref[...] = _rms(y, gf_ref[...])


def _ffn_sample(x, s0, s1, g, wup, cw, cb, wdn, gf):
    N, D = x.shape
    d_ff = wdn.shape[0]
    return pl.pallas_call(
        _ffn_sample_kernel,
        out_shape=[jax.ShapeDtypeStruct((N, D), F32), jax.ShapeDtypeStruct((N, d_ff), F32)],
        compiler_params=pltpu.CompilerParams(vmem_limit_bytes=VMEM_LIMIT),
        name="conv_ffn_sample",
    )(x, s0, s1, g, wup, cw, cb, wdn, gf)


def _qb_pair_perm():
    cols = []
    for j in range(B_GROUP):
        for hk in range(B_KV_HEADS):
            h = hk * B_GROUP + j
            cols.extend(range(h * HEAD_DIM, (h + 1) * HEAD_DIM))
    return np.asarray(cols, dtype=np.int32)


def _layer_weights(l, g_mix, w_in, g_out_a, g_out_b, sinks, w_out, g_cross, g_mem, w_xq, w_mem_kv, w_xo,
                   g_ffn, w_up, conv_w, conv_b, w_down):
    perm = _qb_pair_perm()
    qb0 = 3 * A_W
    w_in_l = w_in[l]
    w_in_l = jnp.concatenate([w_in_l[:, :qb0], w_in_l[:, qb0:qb0 + B_W][:, perm], w_in_l[:, qb0 + B_W:]], axis=1)
    w_out_l = w_out[l]
    w_out_l = jnp.concatenate([w_out_l[:A_W], w_out_l[A_W:][perm]], axis=0)
    sink = sinks[l].astype(F32)
    pair = jnp.stack([sink[:B_GROUP], sink[B_GROUP:]], axis=1)
    sink_lanes = jnp.broadcast_to(jnp.repeat(pair, HEAD_DIM, axis=1)[:, None, :], (B_GROUP, 8, V7X_LANES))
    sink_rows = pair.reshape(2 * B_GROUP, 1)
    row = lambda v: v.reshape(1, -1).astype(F32)
    return dict(
        g_mix=row(g_mix[l]), w_in=w_in_l.astype(BF16),
        g_out_a=row(g_out_a[l]), g_out_b=row(g_out_b[l][perm]), w_out=w_out_l.astype(BF16),
        sink_lanes=sink_lanes, sink_rows=sink_rows,
        g_cross=row(g_cross[l]), g_mem=row(g_mem[l]), w_xq=w_xq[l].astype(BF16),
        w_mem_kv=w_mem_kv[l].astype(BF16), w_xo=w_xo[l].astype(BF16),
        g_ffn=row(g_ffn[l]), w_up=w_up[l].astype(BF16), conv_w=conv_w[l].astype(F32),
        conv_b=row(conv_b[l]), w_down=w_down[l].astype(BF16))


PROMPT_TM = 512
SAMPLE_NB = 8


def _prompt_layer(x, mem, W, g_final, cos, sin, sample_a):
    B, S, D = x.shape
    M = mem.shape[1]
    tm = min(PROMPT_TM, S)
    qa, ka, va, qb, kb, vb, ka_t, va_t, kb_t, vb_t = _inproj(
        x, W["g_mix"], W["w_in"], cos, sin, tm=tm, prompt=True)
    flat = lambda t: t.reshape(B, S, A_W)
    oa = _attn_a(flat(qa), flat(ka), flat(va)).reshape(B, R4, S // R4, A_W)
    ob = _attn_b(qb, kb, vb, W["sink_lanes"])
    mk, mv = _norm_matmul(mem.reshape(B * M, D), W["g_mem"], W["w_mem_kv"], n_out=2, tm=min(512, B * M))
    mk = mk.reshape(B, M, D)
    mv = mv.reshape(B, M, D)

    sq, skn, svn, kt, vt = sample_a
    N, W_, L = kt.shape
    unit_rows = lambda t: t.reshape(N * CACHE_HALVES, 1, W_ // CACHE_HALVES)
    sq, skn, svn = unit_rows(sq), unit_rows(skn), unit_rows(svn)
    n_cross = B * (S // tm) * CROSS_UNITS
    head = lambda t: t[:n_cross]
    tail = lambda t: t[n_cross:]
    x2, o_head, okt, ovt = _outproj_cross(
        oa, ob, x, W["g_out_a"], W["g_out_b"], W["w_out"], W["g_cross"], W["w_xq"], mk, mv, W["w_xo"],
        head(sq), head(skn), head(svn), kt, vt, tm=tm)
    y, conv_state, o_tail, okt, ovt = _ffn_prompt(
        x2, W["g_ffn"], W["w_up"], W["conv_w"], W["conv_b"], W["w_down"], g_final,
        tail(sq), tail(skn), tail(svn), kt, vt, okt, ovt, unit0=n_cross, tm=tm)
    oa_s = jnp.concatenate([o_head, o_tail], axis=0).reshape(N, W_)
    return y, (ka_t, va_t, kb_t, vb_t, mk, mv, conv_state), (oa_s, okt, ovt)


def _sample_inproj(x, W, cos, sin):
    N = x.shape[0]
    return [t[0] for t in _inproj(x[None], W["g_mix"], W["w_in"], cos, sin, tm=N, prompt=False)]


def _sample_layer(x, oa, qb, kb, vb, b_kt, b_vt, mem_k, mem_v, conv_state, W, g_final):
    N, D = x.shape
    ob, s_bkt, s_bvt = _sample_b(qb, kb, vb, b_kt, b_vt, W["sink_rows"], nb=SAMPLE_NB)
    x1 = _outproj(oa, ob, x, W["g_out_a"], W["g_out_b"], W["w_out"])
    (q,) = _norm_matmul(x1, W["g_cross"], W["w_xq"], n_out=1, tm=N)
    hd = D // X_HEADS
    halves = hd // V7X_LANES
    q = (q * (hd ** -0.5)).reshape(N, X_HEADS, halves, V7X_LANES).transpose(0, 2, 1, 3)
    o = _sample_cross(q.reshape(N, MEM_TILE_ROWS, V7X_LANES), mem_k, mem_v, nb=SAMPLE_NB)
    o = o.reshape(N, halves, X_HEADS, V7X_LANES).transpose(0, 2, 1, 3).reshape(N, D)
    x2 = _matmul_residual(o, W["w_xo"], x1, tm=N)
    y, gate = _ffn_sample(x2, conv_state[:, 0], conv_state[:, 1], W["g_ffn"], W["w_up"], W["conv_w"],
                          W["conv_b"], W["w_down"], g_final)
    new_conv = jnp.stack([conv_state[:, 1], gate], axis=1)
    return y, (s_bkt, s_bvt, new_conv)


def _time_minor(cache):
    N, L, H, Dh = cache.shape
    return cache.transpose(0, 2, 3, 1).reshape(N, H * Dh, L)


def _time_major(cache_t, H):
    N, W, L = cache_t.shape
    return cache_t.reshape(N, H, W // H, L).transpose(0, 3, 1, 2)


def kernel(x_prompt, x_sample, cache_a_k, cache_a_v, cache_b_k, cache_b_v, cache_mem_k, cache_mem_v, state_conv,
           mem_prompt, g_mix, w_in, g_out_a, g_out_b, sinks, w_out, g_cross, g_mem, w_xq, w_mem_kv, w_xo,
           g_ffn, w_up, conv_w, conv_b, w_down, g_final):
    depth = w_in.shape[0]
    assert depth == 1, "layer stacking is not wired up: the problem has a single layer"
    B, S, D = x_prompt.shape
    N, T, _ = x_sample.shape
    assert T == 1, "the sample group decodes one token per sequence"
    gf = g_final.reshape(1, D).astype(F32)
    cos_p, sin_p = _rope_tables(S, 0, 1)
    cos_s, sin_s = _rope_tables(N, PAST_LEN, 0)

    l = 0
    W = _layer_weights(l, g_mix, w_in, g_out_a, g_out_b, sinks, w_out, g_cross, g_mem, w_xq, w_mem_kv, w_xo,
                       g_ffn, w_up, conv_w, conv_b, w_down)

    xs = x_sample.reshape(N, D)
    qa, ka, va, qb, kb, vb = _sample_inproj(xs, W, cos_s, sin_s)
    sample_a = (qa, ka, va, _time_minor(cache_a_k[l]), _time_minor(cache_a_v[l]))
    yp, (p_ak, p_av, p_bk, p_bv, p_mk, p_mv, p_conv), (oa_s, s_akt, s_avt) = _prompt_layer(
        x_prompt, mem_prompt, W, gf, cos_p, sin_p, sample_a)
    ys, (s_bkt, s_bvt, s_conv) = _sample_layer(
        xs, oa_s, qb, kb, vb, _time_minor(cache_b_k[l]), _time_minor(cache_b_v[l]),
        _mem_tile_rows(cache_mem_k[l]), _mem_tile_rows(cache_mem_v[l]), state_conv[l], W, gf)

    la, lb = p_ak.shape[1], p_bk.shape[1]
    return (yp, ys.reshape(N, 1, D),
            p_ak.reshape(1, B, la, A_HEADS, HEAD_DIM), p_av.reshape(1, B, la, A_HEADS, HEAD_DIM),
            p_bk.reshape(1, B, lb, B_KV_HEADS, HEAD_DIM), p_bv.reshape(1, B, lb, B_KV_HEADS, HEAD_DIM),
            p_mk.reshape(1, B, -1, X_HEADS, D // X_HEADS), p_mv.reshape(1, B, -1, X_HEADS, D // X_HEADS),
            p_conv[None],
            _time_major(s_akt, A_HEADS)[None], _time_major(s_avt, A_HEADS)[None],
            _time_major(s_bkt, B_KV_HEADS)[None], _time_major(s_bvt, B_KV_HEADS)[None],
            s_conv[None])
```

```python
import functools

import jax
import jax.numpy as jnp
import numpy as np
from jax import lax
from jax.experimental import pallas as pl
from jax.experimental.pallas import tpu as pltpu

F32 = jnp.float32
BF16 = jnp.bfloat16

HEAD_DIM = 64
A_HEADS = 8
B_HEADS = 8
B_KV_HEADS = 2
B_GROUP = B_HEADS // B_KV_HEADS
A_W = A_HEADS * HEAD_DIM
B_W = B_HEADS * HEAD_DIM
B_KV_W = B_KV_HEADS * HEAD_DIM
DILATIONS = (1, 4, 16)
A_WINDOW = 2048
B_WINDOW = 128
BLOCK = 128
ROPE_THETA = 10000.0
PAST_LEN = 16384
X_HEADS = 4
CONV_W = 3
EPS = 1e-6
NEG = -1e30
LOG2E = 1.4426950408889634
ATTN_UNROLL = 32
R4 = 4

V7X_LANES = 128
V7X_MXU_DIM = 256
V7X_VMEM_BYTES = 64 * 1024 * 1024
VMEM_LIMIT = V7X_VMEM_BYTES - 8 * 1024 * 1024
VMEM_LIMIT_FFN = V7X_VMEM_BYTES - 4 * 1024 * 1024


def _params(*sem, vmem_limit=None):
    return pltpu.CompilerParams(dimension_semantics=sem, vmem_limit_bytes=vmem_limit or VMEM_LIMIT)


def _rms(x, g):
    return (x * lax.rsqrt(jnp.mean(x * x, axis=-1, keepdims=True) + EPS)) * g


def _dot(a, b):
    return jnp.dot(a, b, preferred_element_type=F32)


def _dot_t(a, b):
    return lax.dot_general(a, b, (((1,), (1,)), ((), ())), preferred_element_type=F32)


def _rope_table_kernel(inv_ref, cos_ref, sin_ref, *, pos0, pos_step):
    rows = cos_ref.shape[0]
    row = lax.broadcasted_iota(jnp.int32, (rows, V7X_LANES), 0)
    lane = lax.broadcasted_iota(jnp.int32, (rows, V7X_LANES), 1)
    ang = (pos0 + pos_step * row).astype(F32) * inv_ref[...]
    first_half = (lane % HEAD_DIM) < (HEAD_DIM // 2)
    cos_ref[...] = jnp.cos(ang)
    sin_ref[...] = jnp.where(first_half, -jnp.sin(ang), jnp.sin(ang))


def _rope_tables(rows, pos0, pos_step):
    half = HEAD_DIM // 2
    inv = jnp.power(ROPE_THETA, -jnp.arange(half, dtype=F32) / half)
    inv = jnp.tile(inv, V7X_LANES // half).reshape(1, V7X_LANES)
    return pl.pallas_call(
        functools.partial(_rope_table_kernel, pos0=pos0, pos_step=pos_step),
        out_shape=[jax.ShapeDtypeStruct((rows, V7X_LANES), F32)] * 2,
        name="rope_tables",
    )(inv)


def _rope(slab, cos, sin, first_half):
    partner = jnp.where(first_half, pltpu.roll(slab, 96, 1), pltpu.roll(slab, 32, 1))
    return slab * cos + partner * sin


def _inproj_kernel(x_ref, g_ref, w_ref, cos_ref, sin_ref,
                   qa_ref, ka_ref, va_ref, qb_ref, kb_ref, vb_ref, *rest,
                   tail_skip, n_tiles, prompt):
    tm = x_ref.shape[0]
    hn = _rms(x_ref[...], g_ref[...]).astype(BF16)
    cos = cos_ref[...]
    sin = sin_ref[...]
    lane = lax.broadcasted_iota(jnp.int32, (tm, V7X_LANES), 1)
    first_half = (lane % HEAD_DIM) < (HEAD_DIM // 2)

    def seg(c0, width):
        return _dot(hn, w_ref[:, c0:c0 + width])

    def roped(z):
        return jnp.concatenate(
            [_rope(z[:, c:c + V7X_LANES], cos, sin, first_half)
             for c in range(0, z.shape[1], V7X_LANES)], axis=1)

    if prompt:
        kat_ref, vat_ref, kbt_ref, vbt_ref, shuf_ref = rest

        def put_a(dst_ref, z):
            for c in range(A_W // V7X_LANES):
                shuf_ref[c] = z[:, c * V7X_LANES:(c + 1) * V7X_LANES]
            for c in range(A_W // V7X_LANES):
                for r in range(R4):
                    dst_ref[r, :, c * V7X_LANES:(c + 1) * V7X_LANES] = shuf_ref[c, pl.ds(r, tm // R4, stride=R4), :]
    else:
        def put_a(dst_ref, z):
            dst_ref[...] = z

    put_a(qa_ref, roped(seg(0, A_W)))
    ka = roped(seg(A_W, A_W))
    put_a(ka_ref, ka)
    va = seg(2 * A_W, A_W)
    put_a(va_ref, va)
    qb_ref[...] = roped(seg(3 * A_W, B_W))
    kvb = seg(3 * A_W + B_W, 2 * B_KV_W)
    kb = roped(kvb[:, :B_KV_W])
    vb = kvb[:, B_KV_W:]
    kb_ref[...] = kb
    vb_ref[...] = vb

    if prompt:
        i = pl.program_id(1)

        @pl.when(i >= tail_skip)
        def _():
            kat_ref[...] = ka
            vat_ref[...] = va

        @pl.when(i == n_tiles - 1)
        def _():
            kbt_ref[...] = kb[tm - B_WINDOW:, :]
            vbt_ref[...] = vb[tm - B_WINDOW:, :]


def _inproj(x, g, w, cos, sin, *, tm, prompt):
    B, S, D = x.shape
    n_tiles = S // tm
    la = min(A_WINDOW, S)
    lb = min(B_WINDOW, S)
    tail_skip = (S - la) // tm
    tok = lambda w_: pl.BlockSpec((None, tm, w_), lambda b, i: (b, i, 0))
    const = lambda shape: pl.BlockSpec(shape, lambda b, i: (0,) * len(shape))
    if prompt:
        a_shape = jax.ShapeDtypeStruct((B, R4, S // R4, A_W), F32)
        a_spec = pl.BlockSpec((None, R4, tm // R4, A_W), lambda b, i: (b, 0, i, 0))
    else:
        a_shape = jax.ShapeDtypeStruct((B, S, A_W), F32)
        a_spec = tok(A_W)
    out_shape = [a_shape] * 3 + [
        jax.ShapeDtypeStruct((B, S, B_W), F32),
        jax.ShapeDtypeStruct((B, S, B_KV_W), F32),
        jax.ShapeDtypeStruct((B, S, B_KV_W), F32)]
    out_specs = [a_spec] * 3 + [tok(B_W), tok(B_KV_W), tok(B_KV_W)]
    scratch = []
    if prompt:
        out_shape += [jax.ShapeDtypeStruct((B, la, A_W), F32)] * 2
        out_shape += [jax.ShapeDtypeStruct((B, lb, B_KV_W), F32)] * 2
        a_tail = pl.BlockSpec((None, tm, A_W), lambda b, i: (b, jnp.maximum(i - tail_skip, 0), 0))
        b_tail = pl.BlockSpec((None, lb, B_KV_W), lambda b, i: (b, 0, 0))
        out_specs += [a_tail, a_tail, b_tail, b_tail]
        scratch = [pltpu.VMEM((A_W // V7X_LANES, tm, V7X_LANES), F32)]
    return pl.pallas_call(
        functools.partial(_inproj_kernel, tail_skip=tail_skip, n_tiles=n_tiles, prompt=prompt),
        grid=(B, n_tiles),
        in_specs=[tok(D), const((1, D)), const(w.shape),
                  pl.BlockSpec((tm, V7X_LANES), lambda b, i: (i, 0)),
                  pl.BlockSpec((tm, V7X_LANES), lambda b, i: (i, 0))],
        out_specs=out_specs,
        out_shape=out_shape,
        scratch_shapes=scratch,
        compiler_params=_params("arbitrary", "arbitrary"),
        name="inproj_rope",
    )(x, g, w, cos, sin)


def _band_bias(first, chunked):
    a = lax.broadcasted_iota(jnp.int32, (2 * BLOCK, 2 * BLOCK), 0) % BLOCK
    b = lax.broadcasted_iota(jnp.int32, (2 * BLOCK, 2 * BLOCK), 1)
    own = b >= BLOCK
    bb = b % BLOCK
    if chunked:
        sub = BLOCK // R4
        a = R4 * (a % sub) + a // sub
        bb = R4 * (bb % sub) + bb // sub
    dist = BLOCK + a - (bb + jnp.where(own, BLOCK, 0))
    valid = (dist >= 0) & (dist <= BLOCK)
    if first:
        valid = valid & own
    return jnp.where(valid, 0.0, NEG).astype(F32)


def _two_head_block(q, kk, vv, bias):
    lane = lax.broadcasted_iota(jnp.int32, (BLOCK, V7X_LANES), 1)
    lo = lane < HEAD_DIM
    q = q * (HEAD_DIM ** -0.5 * LOG2E)
    qs = jnp.concatenate([jnp.where(lo, q, 0.0), jnp.where(lo, 0.0, q)], axis=0).astype(BF16)
    s = _dot_t(qs, kk) + bias
    m = jnp.max(s, axis=-1, keepdims=True)
    p = jnp.exp2(s - m)
    v1 = jnp.concatenate([vv, jnp.ones_like(vv)], axis=1)
    pv = _dot(p.astype(BF16), v1)
    acc = jnp.where(lo, pv[:BLOCK, :V7X_LANES], pv[BLOCK:, :V7X_LANES])
    l2 = jnp.where(lo, pv[:BLOCK, V7X_LANES:], pv[BLOCK:, V7X_LANES:])
    m2 = jnp.where(lo, m[:BLOCK], m[BLOCK:])
    return acc, m2, l2


def _get(ref, slices):
    return jnp.concatenate([ref[sl, :] for sl in slices], axis=0) if len(slices) > 1 else ref[slices[0], :]


def _put(ref, slices, val):
    off = 0
    for sl in slices:
        ref[sl, :] = val[off:off + sl.size]
        off += sl.size


def _block_slices(layout, d, S, i):
    if layout == "seq":
        start = pl.multiple_of(i * BLOCK, BLOCK)
        prev = pl.multiple_of(jnp.maximum(i - 1, 0) * BLOCK, BLOCK)
        return [pl.ds(start, BLOCK)], [pl.ds(prev, BLOCK)], i == 0
    Sr = S // R4
    if d == 1:
        sub = BLOCK // R4
        pj = jnp.maximum(i - 1, 0)
        own = [pl.ds(pl.multiple_of(r * Sr + i * sub, sub), sub) for r in range(R4)]
        prev = [pl.ds(pl.multiple_of(r * Sr + pj * sub, sub), sub) for r in range(R4)]
        return own, prev, i == 0
    if d == R4:
        per_res = Sr // BLOCK
        jb = i % per_res
        start = pl.multiple_of(i * BLOCK, BLOCK)
        prev = pl.multiple_of(jnp.where(jb == 0, i, i - 1) * BLOCK, BLOCK)
        return [pl.ds(start, BLOCK)], [pl.ds(prev, BLOCK)], jb == 0
    step = d // R4
    per_res = S // (BLOCK * d)
    rd = i // per_res
    jb = i % per_res
    start = (rd % R4) * Sr + rd // R4 + jb * (BLOCK * step)
    prev = jnp.where(jb == 0, start, start - BLOCK * step)
    return [pl.ds(start, BLOCK, stride=step)], [pl.ds(prev, BLOCK, stride=step)], jb == 0


def _branch_blocks(q_ref, k_ref, v_ref, bias_ref, layout, d, visit):
    S = q_ref.shape[0]
    bias_base = 2 if (layout == "r4" and d == 1) else 0

    def body(i, carry):
        own, prev, first = _block_slices(layout, d, S, i)
        kk = jnp.concatenate([_get(k_ref, prev), _get(k_ref, own)], axis=0).astype(BF16)
        vv = jnp.concatenate([_get(v_ref, prev), _get(v_ref, own)], axis=0).astype(BF16)
        bias = bias_ref[bias_base + jnp.where(first, 1, 0)]
        acc, m, l = _two_head_block(_get(q_ref, own), kk, vv, bias)
        visit(own, acc, m, l)
        return carry

    lax.fori_loop(0, S // BLOCK, body, 0, unroll=ATTN_UNROLL)


def _init_bias(bias_ref, chunked_too):
    bias_ref[0] = _band_bias(False, False)
    bias_ref[1] = _band_bias(True, False)
    if chunked_too:
        bias_ref[2] = _band_bias(False, True)
        bias_ref[3] = _band_bias(True, True)


def _attn_a_kernel(q_ref, k_ref, v_ref, o_ref, m_ref, l_ref, bias_ref):
    _init_bias(bias_ref, True)

    def first_visit(rows, acc, m, l):
        _put(o_ref, rows, acc)
        _put(m_ref, rows, m)
        _put(l_ref, rows, l)

    def merged(rows, acc, m, l):
        m_old = _get(m_ref, rows)
        m_new = jnp.maximum(m_old, m)
        a_old = jnp.exp2(m_old - m_new)
        a_new = jnp.exp2(m - m_new)
        return (a_old * _get(o_ref, rows) + a_new * acc, m_new, a_old * _get(l_ref, rows) + a_new * l)

    def mid_visit(rows, acc, m, l):
        acc, m, l = merged(rows, acc, m, l)
        _put(o_ref, rows, acc)
        _put(m_ref, rows, m)
        _put(l_ref, rows, l)

    def last_visit(rows, acc, m, l):
        acc, m, l = merged(rows, acc, m, l)
        _put(o_ref, rows, acc / l)

    order = (R4, 1, 16)
    assert sorted(order) == sorted(DILATIONS)
    visits = [first_visit] + [mid_visit] * (len(order) - 2) + [last_visit]
    for d, visit in zip(order, visits):
        _branch_blocks(q_ref, k_ref, v_ref, bias_ref, "r4", d, visit)


def _attn_b_kernel(q_ref, k_ref, v_ref, sink_ref, o_ref, bias_ref):
    _init_bias(bias_ref, False)
    sink = sink_ref[0:1, :] * LOG2E

    def visit(rows, acc, m, l):
        _put(o_ref, rows, acc / (l + jnp.exp2(sink - m)))

    _branch_blocks(q_ref, k_ref, v_ref, bias_ref, "seq", 1, visit)


def _attn_a(q, k, v):
    B, S, W = q.shape
    slab = pl.BlockSpec((None, S, V7X_LANES), lambda b, j: (b, 0, j))
    return pl.pallas_call(
        _attn_a_kernel,
        grid=(B, W // V7X_LANES),
        in_specs=[slab, slab, slab],
        out_specs=slab,
        out_shape=jax.ShapeDtypeStruct((B, S, W), F32),
        scratch_shapes=[pltpu.VMEM((S, V7X_LANES), F32), pltpu.VMEM((S, V7X_LANES), F32),
                        pltpu.VMEM((4, 2 * BLOCK, 2 * BLOCK), F32)],
        compiler_params=_params("arbitrary", "arbitrary"),
        name="mixer_a_attention",
    )(q, k, v)


def _attn_b(q, k, v, sink_lanes):
    B, S, W = q.shape
    slab = pl.BlockSpec((None, S, V7X_LANES), lambda b, j: (b, 0, j))
    kv = pl.BlockSpec((None, S, V7X_LANES), lambda b, j: (b, 0, 0))
    return pl.pallas_call(
        _attn_b_kernel,
        grid=(B, W // V7X_LANES),
        in_specs=[slab, kv, kv, pl.BlockSpec((None, 8, V7X_LANES), lambda b, j: (j, 0, 0))],
        out_specs=slab,
        out_shape=jax.ShapeDtypeStruct((B, S, W), F32),
        scratch_shapes=[pltpu.VMEM((2, 2 * BLOCK, 2 * BLOCK), F32)],
        compiler_params=_params("arbitrary", "arbitrary"),
        name="mixer_b_attention",
    )(q, k, v, sink_lanes)


def _as_column(row):
    return jnp.broadcast_to(row, (V7X_LANES, row.shape[1])).T


def _roll_in(dst_ref, cache, new_col):
    L = cache.shape[1]
    rolled = pltpu.roll(cache, L - 1, 1)
    lane = lax.broadcasted_iota(jnp.int32, (cache.shape[0], V7X_LANES), 1)
    if L > V7X_LANES:
        dst_ref[:, :L - V7X_LANES] = rolled[:, :L - V7X_LANES]
    dst_ref[:, L - V7X_LANES:] = jnp.where(lane == V7X_LANES - 1, new_col, rolled[:, L - V7X_LANES:])


def _sample_a_unit(q, k_new, v_new, kt, vt, w, okt_ref, ovt_ref):
    R, L = kt.shape
    H = R // HEAD_DIM
    n_br = len(DILATIONS)
    qc = _as_column(q * (HEAD_DIM ** -0.5))
    knc = _as_column(k_new)
    vnc = _as_column(v_new)
    prod = jnp.concatenate([kt[:, j:j + V7X_LANES] * qc for j in range(0, L, V7X_LANES)], axis=1)
    s = jnp.sum(prod.reshape(H, HEAD_DIM, L), axis=1)
    s_new = jnp.sum((qc * knc).reshape(H, HEAD_DIM, V7X_LANES), axis=1)[:, 0:1]
    s = jnp.where(w > 0, s, NEG)
    m = jnp.maximum(jnp.max(s, axis=-1, keepdims=True), s_new)
    p = w * jnp.exp(s - m)
    p_new = n_br * jnp.exp(s_new - m)
    l = jnp.sum(p, axis=-1, keepdims=True) + p_new

    def per_row(a):
        return jnp.broadcast_to(a.reshape(H, 1, a.shape[1]), (H, HEAD_DIM, a.shape[1])).reshape(R, a.shape[1])

    o_col = (jnp.sum(vt * per_row(p), axis=-1, keepdims=True) + per_row(p_new) * vnc[:, 0:1]) / per_row(l)
    _roll_in(okt_ref, kt, knc)
    _roll_in(ovt_ref, vt, vnc)
    return jnp.broadcast_to(o_col, (R, V7X_LANES)).T[0:1, :]


def _branch_multiplicity(L):
    dist = L - np.arange(L)
    w = sum(((dist % d == 0) & (dist <= BLOCK * d)).astype(np.float32) for d in DILATIONS)
    return jnp.asarray(w.reshape(1, L))


CACHE_HALVES = 2


IN_SLOTS = 3
OUT_SLOTS = 3


class _CacheStream:
    def __init__(self, step, n_steps, per_step, unit0, hbm, ring, sem):
        self.step, self.per_step, self.unit0 = step, per_step, unit0
        self.total = n_steps * per_step
        self.n_steps = n_steps
        self.kt_hbm, self.vt_hbm, self.okt_hbm, self.ovt_hbm = hbm
        self.ink, self.inv, self.outk, self.outv = ring
        self.sem = sem
        self.R = self.ink.shape[1]

    def _copies(self, g, inbound):
        unit = self.unit0 + g
        slot = g % (IN_SLOTS if inbound else OUT_SLOTS)
        n = unit // CACHE_HALVES
        rows = pl.ds(pl.multiple_of((unit % CACHE_HALVES) * self.R, self.R), self.R)
        if inbound:
            return (pltpu.make_async_copy(self.kt_hbm.at[n, rows, :], self.ink.at[slot], self.sem.at[0, slot]),
                    pltpu.make_async_copy(self.vt_hbm.at[n, rows, :], self.inv.at[slot], self.sem.at[1, slot]))
        return (pltpu.make_async_copy(self.outk.at[slot], self.okt_hbm.at[n, rows, :], self.sem.at[2, slot]),
                pltpu.make_async_copy(self.outv.at[slot], self.ovt_hbm.at[n, rows, :], self.sem.at[3, slot]))

    def _start(self, g, inbound):
        for cp in self._copies(g, inbound):
            cp.start()

    def _wait(self, g, inbound):
        for cp in self._copies(g, inbound):
            cp.wait()

    def prologue(self):
        @pl.when(self.step == 0)
        def _():
            for g in range(min(IN_SLOTS, self.total)):
                self._start(g, True)

    def wait_unit(self, u):
        g = self.step * self.per_step + u
        self._wait(g, True)

        @pl.when(g >= OUT_SLOTS)
        def _():
            self._wait(g - OUT_SLOTS, False)

    def run_unit(self, u, q_ref, kn_ref, vn_ref, w_ref, o_ref):
        g = self.step * self.per_step + u
        islot = g % IN_SLOTS
        oslot = g % OUT_SLOTS
        o_ref[g] = _sample_a_unit(q_ref[g], kn_ref[g], vn_ref[g], self.ink[islot], self.inv[islot], w_ref[...],
                                  self.outk.at[oslot], self.outv.at[oslot])
        self._start(g, False)

        @pl.when(g + IN_SLOTS < self.total)
        def _():
            self._start(g + IN_SLOTS, True)

    def epilogue(self):
        @pl.when(self.step == self.n_steps - 1)
        def _():
            for g in range(max(self.total - OUT_SLOTS, 0), self.total):
                self._wait(g, False)


def _stream_specs(n_units, R, L):
    rows = pl.BlockSpec((n_units, 1, R), lambda *_: (0, 0, 0))
    hbm = pl.BlockSpec(memory_space=pl.ANY)
    ring_in = pltpu.VMEM((IN_SLOTS, R, L), F32)
    ring_out = pltpu.VMEM((OUT_SLOTS, R, L), F32)
    in_specs = [rows, rows, rows, pl.BlockSpec((1, L), lambda *_: (0, 0)), hbm, hbm]
    out_specs = [rows, hbm, hbm]
    scratch = [ring_in, ring_in, ring_out, ring_out, pltpu.SemaphoreType.DMA((4, max(IN_SLOTS, OUT_SLOTS)))]
    return in_specs, out_specs, scratch


def _sample_b_kernel(q_ref, kn_ref, vn_ref, kt_ref, vt_ref, sink_ref, o_ref, okt_ref, ovt_ref):
    nb = q_ref.shape[0]
    lane = lax.broadcasted_iota(jnp.int32, (1, V7X_LANES), 1)
    lo = lane < HEAD_DIM
    for n in range(nb):
        q = q_ref[n:n + 1, :] * (HEAD_DIM ** -0.5)
        rows = []
        for j in range(B_W // V7X_LANES):
            pair = q[:, j * V7X_LANES:(j + 1) * V7X_LANES]
            rows += [jnp.where(lo, pair, 0.0), jnp.where(lo, 0.0, pair)]
        qh = jnp.concatenate(rows, axis=0)
        kt = kt_ref[n]
        vt = vt_ref[n]
        s_new = jnp.sum(qh * kn_ref[n:n + 1, :], axis=-1, keepdims=True)
        s = _dot(qh.astype(BF16), kt.astype(BF16))
        m = jnp.maximum(s_new, jnp.max(s, axis=-1, keepdims=True))
        p_new = jnp.exp(s_new - m)
        p = jnp.exp(s - m)
        l = p_new + jnp.sum(p, axis=-1, keepdims=True)
        o = p_new * vn_ref[n:n + 1, :] + _dot_t(p.astype(BF16), vt.astype(BF16))
        o = o / (l + jnp.exp(sink_ref[...] - m))
        o_ref[n:n + 1, :] = jnp.concatenate(
            [jnp.where(lo, o[2 * j:2 * j + 1], o[2 * j + 1:2 * j + 2]) for j in range(B_W // V7X_LANES)], axis=1)
        _roll_in(okt_ref.at[n], kt, _as_column(kn_ref[n:n + 1, :]))
        _roll_in(ovt_ref.at[n], vt, _as_column(vn_ref[n:n + 1, :]))


def _sample_b(q, k_new, v_new, kt, vt, sink_rows, *, nb):
    N, W, L = kt.shape
    assert L == B_WINDOW and W == V7X_LANES, "a cache shorter than the sliding window is unsupported"
    row = lambda w_: pl.BlockSpec((nb, w_), lambda i: (i, 0))
    cache = pl.BlockSpec((nb, W, L), lambda i: (i, 0, 0))
    return pl.pallas_call(
        _sample_b_kernel,
        grid=(N // nb,),
        in_specs=[row(B_W), row(W), row(W), cache, cache, pl.BlockSpec((8, 1), lambda i: (0, 0))],
        out_specs=[row(B_W), cache, cache],
        out_shape=[jax.ShapeDtypeStruct((N, B_W), F32), jax.ShapeDtypeStruct((N, W, L), F32),
                   jax.ShapeDtypeStruct((N, W, L), F32)],
        compiler_params=_params("arbitrary"),
        name="sample_mixer_b",
    )(q, k_new, v_new, kt, vt, sink_rows)


def _mixers_out(oa, ob, x, ga_ref, gb_ref, w_ref):
    ha = _rms(oa, ga_ref[...]).astype(BF16)
    hb = _rms(ob, gb_ref[...]).astype(BF16)
    return x + _dot(ha, w_ref[:A_W, :]) + _dot(hb, w_ref[A_W:, :])


def _outproj_kernel(oa_ref, ob_ref, x_ref, ga_ref, gb_ref, w_ref, y_ref):
    y_ref[...] = _mixers_out(oa_ref[...], ob_ref[...], x_ref[...], ga_ref, gb_ref, w_ref)


def _outproj(oa, ob, x, ga, gb, w):
    return pl.pallas_call(
        _outproj_kernel,
        out_shape=jax.ShapeDtypeStruct(x.shape, F32),
        compiler_params=pltpu.CompilerParams(vmem_limit_bytes=VMEM_LIMIT),
        name="mixers_outproj",
    )(oa, ob, x, ga, gb, w)


def _norm_matmul_kernel(x_ref, g_ref, w_ref, *out_refs):
    h = _rms(x_ref[...], g_ref[...]).astype(BF16)
    width = out_refs[0].shape[1]
    for j, o_ref in enumerate(out_refs):
        o_ref[...] = _dot(h, w_ref[:, j * width:(j + 1) * width])


def _norm_matmul(x, g, w, *, n_out, tm):
    T, D = x.shape
    width = w.shape[1] // n_out
    return pl.pallas_call(
        _norm_matmul_kernel,
        grid=(T // tm,),
        in_specs=[pl.BlockSpec((tm, D), lambda i: (i, 0)), pl.BlockSpec((1, D), lambda i: (0, 0)),
                  pl.BlockSpec(w.shape, lambda i: (0, 0))],
        out_specs=[pl.BlockSpec((tm, width), lambda i: (i, 0))] * n_out,
        out_shape=[jax.ShapeDtypeStruct((T, width), F32)] * n_out,
        compiler_params=_params("arbitrary"),
        name="norm_matmul",
    )(x, g, w)


def _matmul_residual_kernel(a_ref, w_ref, r_ref, y_ref):
    y_ref[...] = r_ref[...] + _dot(a_ref[...].astype(BF16), w_ref[...])


def _matmul_residual(a, w, r, *, tm):
    T, K = a.shape
    D = w.shape[1]
    return pl.pallas_call(
        _matmul_residual_kernel,
        grid=(T // tm,),
        in_specs=[pl.BlockSpec((tm, K), lambda i: (i, 0)), pl.BlockSpec(w.shape, lambda i: (0, 0)),
                  pl.BlockSpec((tm, D), lambda i: (i, 0))],
        out_specs=pl.BlockSpec((tm, D), lambda i: (i, 0)),
        out_shape=jax.ShapeDtypeStruct((T, D), F32),
        compiler_params=_params("arbitrary"),
        name="matmul_residual",
    )(a, w, r)


def _outproj_cross_kernel(oa_ref, ob_ref, x_ref, ga_ref, gb_ref, wout_ref, gc_ref, wq_ref, mk_ref, mv_ref, wo_ref,
                          y_ref, shuf_ref, q_ref):
    tq = oa_ref.shape[1]
    for c in range(A_W // V7X_LANES):
        for r in range(R4):
            shuf_ref[c, pl.ds(r, tq, stride=R4), :] = oa_ref[r, :, c * V7X_LANES:(c + 1) * V7X_LANES]
    oa = jnp.concatenate([shuf_ref[c] for c in range(A_W // V7X_LANES)], axis=1)
    y_ref[...] = _mixers_out(oa, ob_ref[...], x_ref[...], ga_ref, gb_ref, wout_ref)
    D = y_ref.shape[1]
    hd = D // X_HEADS
    q_ref[...] = (_dot(_rms(y_ref[...], gc_ref[...]).astype(BF16), wq_ref[...]) * (hd ** -0.5)).astype(BF16)
    for h in range(X_HEADS):
        cols = slice(h * hd, (h + 1) * hd)
        s = _dot_t(q_ref[:, cols], mk_ref[:, cols].astype(BF16))
        p = jnp.exp(s - jnp.max(s, axis=-1, keepdims=True))
        o = _dot(p.astype(BF16), mv_ref[:, cols].astype(BF16)) / jnp.sum(p, axis=-1, keepdims=True)
        y_ref[...] += _dot(o.astype(BF16), wo_ref[cols, :])


def _outproj_cross(oa, ob, x, ga, gb, wout, gc, wq, mk, mv, wo, *, tm):
    B, S, D = x.shape
    M = mk.shape[1]
    tok = lambda w_: pl.BlockSpec((None, tm, w_), lambda b, i: (b, i, 0))
    mem = pl.BlockSpec((None, M, D), lambda b, i: (b, 0, 0))
    const = lambda shape: pl.BlockSpec(shape, lambda b, i: (0,) * len(shape))
    return pl.pallas_call(
        _outproj_cross_kernel,
        grid=(B, S // tm),
        in_specs=[pl.BlockSpec((None, R4, tm // R4, A_W), lambda b, i: (b, 0, i, 0)), tok(B_W), tok(D),
                  const((1, A_W)), const((1, B_W)), const(wout.shape), const((1, D)), const(wq.shape),
                  mem, mem, const(wo.shape)],
        out_specs=tok(D),
        out_shape=jax.ShapeDtypeStruct((B, S, D), F32),
        scratch_shapes=[pltpu.VMEM((A_W // V7X_LANES, tm, V7X_LANES), F32), pltpu.VMEM((tm, D), BF16)],
        compiler_params=_params("arbitrary", "arbitrary"),
        name="outproj_cross_attention",
    )(oa, ob, x, ga, gb, wout, gc, wq, mk, mv, wo)


MEM_TILE_ROWS = 8


def _sample_cross_kernel(q_ref, mk_ref, mv_ref, o_ref):
    nb = q_ref.shape[0]
    M = mk_ref.shape[1] // MEM_TILE_ROWS
    for n in range(nb):
        q = q_ref[n]
        k3 = mk_ref[n].reshape(M, MEM_TILE_ROWS, V7X_LANES)
        v3 = mv_ref[n].reshape(M, MEM_TILE_ROWS, V7X_LANES)
        part = jnp.sum(k3 * q[None], axis=-1, keepdims=True)
        s = part + pltpu.roll(part, X_HEADS, 1)
        p = jnp.exp(s - jnp.max(s, axis=0, keepdims=True))
        o = jnp.sum(v3 * p, axis=0) / jnp.sum(p, axis=0)
        o_ref[n] = o


def _sample_cross(q, mk, mv, *, nb):
    N, R, _ = mk.shape
    mem = pl.BlockSpec((nb, R, V7X_LANES), lambda i: (i, 0, 0))
    row = pl.BlockSpec((nb, MEM_TILE_ROWS, V7X_LANES), lambda i: (i, 0, 0))
    return pl.pallas_call(
        _sample_cross_kernel,
        grid=(N // nb,),
        in_specs=[row, mem, mem],
        out_specs=row,
        out_shape=jax.ShapeDtypeStruct((N, MEM_TILE_ROWS, V7X_LANES), F32),
        compiler_params=_params("arbitrary"),
        name="sample_cross_attention",
    )(q, mk, mv)


def _mem_tile_rows(cache):
    N, M, H, Dh = cache.shape
    halves = Dh // V7X_LANES
    assert H * halves == MEM_TILE_ROWS
    return cache.reshape(N, M, H, halves, V7X_LANES).transpose(0, 1, 3, 2, 4).reshape(N, M * MEM_TILE_ROWS, V7X_LANES)


FF_CHUNKS = ((0, 1024), (1024, 1024), (2048, 768))


def _ffn_chunk(h, wup_ref, wdn_ref, cw_ref, cb_ref, d_ff, c0, cw, prev_rows):
    gate = _dot(h, wup_ref[:, c0:c0 + cw])
    val = _dot(h, wup_ref[:, d_ff + c0:d_ff + c0 + cw])
    g2, g1 = prev_rows(gate)
    conv = cb_ref[:, c0:c0 + cw] + (g2 * cw_ref[0:1, c0:c0 + cw] + g1 * cw_ref[1:2, c0:c0 + cw]
                                     + gate * cw_ref[2:3, c0:c0 + cw])
    act = (conv * (1.0 / (1.0 + jnp.exp(-conv)))) * val
    return gate, _dot(act.astype(BF16), wdn_ref[c0:c0 + cw, :])


FFN_UNITS = 4
FF_CHUNKS_PROMPT = ((0, 768), (768, 768), (1536, 768), (2304, 512))


def _ffn_prompt_kernel(x_ref, g_ref, wup_ref, cw_ref, cb_ref, wdn_ref, gf_ref,
                       sq_ref, skn_ref, svn_ref, sw_ref, kt_hbm, vt_hbm,
                       y_ref, state_ref, so_ref, okt_hbm, ovt_hbm,
                       halo_ref, ink_ref, inv_ref, outk_ref, outv_ref, sem, *, n_tiles, n_steps):
    i = pl.program_id(1)
    step = pl.program_id(0) * n_tiles + i
    tm = x_ref.shape[0]
    d_ff = wdn_ref.shape[0]
    stream = _CacheStream(step, n_steps, FFN_UNITS, 0, (kt_hbm, vt_hbm, okt_hbm, ovt_hbm),
                          (ink_ref, inv_ref, outk_ref, outv_ref), sem)

    @pl.when(i == 0)
    def _():
        halo_ref[...] = jnp.zeros_like(halo_ref)

    stream.prologue()
    x = x_ref[...]
    h = _rms(x, g_ref[...]).astype(BF16)
    y = x
    for u, (c0, cw) in enumerate(FF_CHUNKS_PROMPT):
        stream.wait_unit(u)
        row = lax.broadcasted_iota(jnp.int32, (tm, cw), 0)
        halo = halo_ref[:, c0:c0 + cw]

        def prev_rows(gate):
            g1 = jnp.where(row == 0, halo[1:2], pltpu.roll(gate, 1, 0))
            g2 = jnp.where(row == 0, halo[0:1], jnp.where(row == 1, halo[1:2], pltpu.roll(gate, 2, 0)))
            return g2, g1

        gate, contrib = _ffn_chunk(h, wup_ref, wdn_ref, cw_ref, cb_ref, d_ff, c0, cw, prev_rows)
        y = y + contrib
        halo_ref[0:2, c0:c0 + cw] = gate[tm - 2:, :]
        state_ref[:, c0:c0 + cw] = gate[tm - 2:, :]
        stream.run_unit(u, sq_ref, skn_ref, svn_ref, sw_ref, so_ref)

    y_ref[...] = _rms(y, gf_ref[...])
    stream.epilogue()


def _ffn_prompt(x, g, wup, cw, cb, wdn, gf, sq, skn, svn, kt, vt, *, tm):
    B, S, D = x.shape
    d_ff = wdn.shape[0]
    N, W, L = kt.shape
    assert FF_CHUNKS_PROMPT[-1][0] + FF_CHUNKS_PROMPT[-1][1] == d_ff and len(FF_CHUNKS_PROMPT) == FFN_UNITS
    assert L == A_WINDOW, "a cache shorter than the largest dilated window is unsupported"
    n_tiles = S // tm
    n_steps = B * n_tiles
    n_units = n_steps * FFN_UNITS
    R = W // CACHE_HALVES
    assert sq.shape == (n_units, 1, R) and n_units == N * CACHE_HALVES, "the cache units are spread evenly over the grid"
    tok = pl.BlockSpec((None, tm, D), lambda b, i: (b, i, 0))
    const = lambda shape: pl.BlockSpec(shape, lambda b, i: (0,) * len(shape))
    resident = lambda shape: pl.BlockSpec(shape, lambda b, i: (0,) * len(shape), pipeline_mode=pl.Buffered(1))
    s_in, s_out, s_scratch = _stream_specs(n_units, R, L)
    return pl.pallas_call(
        functools.partial(_ffn_prompt_kernel, n_tiles=n_tiles, n_steps=n_steps),
        grid=(B, n_tiles),
        in_specs=[tok, const((1, D)), resident(wup.shape), const(cw.shape), const(cb.shape),
                  resident(wdn.shape), const((1, D))] + s_in,
        out_specs=[tok, pl.BlockSpec((None, CONV_W - 1, d_ff), lambda b, i: (b, 0, 0))] + s_out,
        out_shape=[jax.ShapeDtypeStruct((B, S, D), F32),
                   jax.ShapeDtypeStruct((B, CONV_W - 1, d_ff), F32),
                   jax.ShapeDtypeStruct((n_units, 1, R), F32),
                   jax.ShapeDtypeStruct((N, W, L), F32), jax.ShapeDtypeStruct((N, W, L), F32)],
        scratch_shapes=[pltpu.VMEM((8, d_ff), F32)] + s_scratch,
        compiler_params=_params("arbitrary", "arbitrary", vmem_limit=VMEM_LIMIT_FFN),
        name="conv_ffn_prompt_sample_a",
    )(x, g, wup, cw, cb, wdn, gf, sq, skn, svn, _branch_multiplicity(L), kt, vt)


def _ffn_sample_kernel(x_ref, s0_ref, s1_ref, g_ref, wup_ref, cw_ref, cb_ref, wdn_ref, gf_ref,
                       y_ref, gate_ref):
    d_ff = wdn_ref.shape[0]
    x = x_ref[...]
    h = _rms(x, g_ref[...]).astype(BF16)
    y = x
    for c0, cw in FF_CHUNKS:
        prev_rows = lambda gate: (s0_ref[:, c0:c0 + cw], s1_ref[:, c0:c0 + cw])
        gate, contrib = _ffn_chunk(h, wup_ref, wdn_ref, cw_ref, cb_ref, d_ff, c0, cw, prev_rows)
        y = y + contrib
        gate_ref[:, c0:c0 + cw] = gate
    y_ref[...] = _rms(y, gf_ref[...])


def _ffn_sample(x, s0, s1, g, wup, cw, cb, wdn, gf):
    N, D = x.shape
    d_ff = wdn.shape[0]
    return pl.pallas_call(
        _ffn_sample_kernel,
        out_shape=[jax.ShapeDtypeStruct((N, D), F32), jax.ShapeDtypeStruct((N, d_ff), F32)],
        compiler_params=pltpu.CompilerParams(vmem_limit_bytes=VMEM_LIMIT),
        name="conv_ffn_sample",
    )(x, s0, s1, g, wup, cw, cb, wdn, gf)


def _qb_pair_perm():
    cols = []
    for j in range(B_GROUP):
        for hk in range(B_KV_HEADS):
            h = hk * B_GROUP + j
            cols.extend(range(h * HEAD_DIM, (h + 1) * HEAD_DIM))
    return np.asarray(cols, dtype=np.int32)


def _layer_weights(l, g_mix, w_in, g_out_a, g_out_b, sinks, w_out, g_cross, g_mem, w_xq, w_mem_kv, w_xo,
                   g_ffn, w_up, conv_w, conv_b, w_down):
    perm = _qb_pair_perm()
    qb0 = 3 * A_W
    w_in_l = w_in[l]
    w_in_l = jnp.concatenate([w_in_l[:, :qb0], w_in_l[:, qb0:qb0 + B_W][:, perm], w_in_l[:, qb0 + B_W:]], axis=1)
    w_out_l = w_out[l]
    w_out_l = jnp.concatenate([w_out_l[:A_W], w_out_l[A_W:][perm]], axis=0)
    sink = sinks[l].astype(F32)
    pair = jnp.stack([sink[:B_GROUP], sink[B_GROUP:]], axis=1)
    sink_lanes = jnp.broadcast_to(jnp.repeat(pair, HEAD_DIM, axis=1)[:, None, :], (B_GROUP, 8, V7X_LANES))
    sink_rows = pair.reshape(2 * B_GROUP, 1)
    row = lambda v: v.reshape(1, -1).astype(F32)
    return dict(
        g_mix=row(g_mix[l]), w_in=w_in_l.astype(BF16),
        g_out_a=row(g_out_a[l]), g_out_b=row(g_out_b[l][perm]), w_out=w_out_l.astype(BF16),
        sink_lanes=sink_lanes, sink_rows=sink_rows,
        g_cross=row(g_cross[l]), g_mem=row(g_mem[l]), w_xq=w_xq[l].astype(BF16),
        w_mem_kv=w_mem_kv[l].astype(BF16), w_xo=w_xo[l].astype(BF16),
        g_ffn=row(g_ffn[l]), w_up=w_up[l].astype(BF16), conv_w=conv_w[l].astype(F32),
        conv_b=row(conv_b[l]), w_down=w_down[l].astype(BF16))


PROMPT_TM = 512
SAMPLE_NB = 8


def _prompt_layer(x, mem, W, g_final, cos, sin, sample_a):
    B, S, D = x.shape
    M = mem.shape[1]
    tm = min(PROMPT_TM, S)
    qa, ka, va, qb, kb, vb, ka_t, va_t, kb_t, vb_t = _inproj(
        x, W["g_mix"], W["w_in"], cos, sin, tm=tm, prompt=True)
    flat = lambda t: t.reshape(B, S, A_W)
    oa = _attn_a(flat(qa), flat(ka), flat(va)).reshape(B, R4, S // R4, A_W)
    ob = _attn_b(qb, kb, vb, W["sink_lanes"])
    mk, mv = _norm_matmul(mem.reshape(B * M, D), W["g_mem"], W["w_mem_kv"], n_out=2, tm=min(512, B * M))
    mk = mk.reshape(B, M, D)
    mv = mv.reshape(B, M, D)
    x2 = _outproj_cross(oa, ob, x, W["g_out_a"], W["g_out_b"], W["w_out"], W["g_cross"], W["w_xq"], mk, mv,
                        W["w_xo"], tm=tm)
    sq, skn, svn, kt, vt = sample_a
    N, W_, L = kt.shape
    unit_rows = lambda t: t.reshape(N * CACHE_HALVES, 1, W_ // CACHE_HALVES)
    y, conv_state, o_units, s_akt, s_avt = _ffn_prompt(
        x2, W["g_ffn"], W["w_up"], W["conv_w"], W["conv_b"], W["w_down"], g_final,
        unit_rows(sq), unit_rows(skn), unit_rows(svn), kt, vt, tm=tm)
    return y, (ka_t, va_t, kb_t, vb_t, mk, mv, conv_state), (o_units.reshape(N, W_), s_akt, s_avt)


def _sample_inproj(x, W, cos, sin):
    N = x.shape[0]
    return [t[0] for t in _inproj(x[None], W["g_mix"], W["w_in"], cos, sin, tm=N, prompt=False)]


def _sample_layer(x, oa, qb, kb, vb, b_kt, b_vt, mem_k, mem_v, conv_state, W, g_final):
    N, D = x.shape
    ob, s_bkt, s_bvt = _sample_b(qb, kb, vb, b_kt, b_vt, W["sink_rows"], nb=SAMPLE_NB)
    x1 = _outproj(oa, ob, x, W["g_out_a"], W["g_out_b"], W["w_out"])
    (q,) = _norm_matmul(x1, W["g_cross"], W["w_xq"], n_out=1, tm=N)
    hd = D // X_HEADS
    halves = hd // V7X_LANES
    q = (q * (hd ** -0.5)).reshape(N, X_HEADS, halves, V7X_LANES).transpose(0, 2, 1, 3)
    o = _sample_cross(q.reshape(N, MEM_TILE_ROWS, V7X_LANES), mem_k, mem_v, nb=SAMPLE_NB)
    o = o.reshape(N, halves, X_HEADS, V7X_LANES).transpose(0, 2, 1, 3).reshape(N, D)
    x2 = _matmul_residual(o, W["w_xo"], x1, tm=N)
    y, gate = _ffn_sample(x2, conv_state[:, 0], conv_state[:, 1], W["g_ffn"], W["w_up"], W["conv_w"],
                          W["conv_b"], W["w_down"], g_final)
    new_conv = jnp.stack([conv_state[:, 1], gate], axis=1)
    return y, (s_bkt, s_bvt, new_conv)


def _time_minor(cache):
    N, L, H, Dh = cache.shape
    return cache.transpose(0, 2, 3, 1).reshape(N, H * Dh, L)


def _time_major(cache_t, H):
    N, W, L = cache_t.shape
    return cache_t.reshape(N, H, W // H, L).transpose(0, 3, 1, 2)


def kernel(x_prompt, x_sample, cache_a_k, cache_a_v, cache_b_k, cache_b_v, cache_mem_k, cache_mem_v, state_conv,
           mem_prompt, g_mix, w_in, g_out_a, g_out_b, sinks, w_out, g_cross, g_mem, w_xq, w_mem_kv, w_xo,
           g_ffn, w_up, conv_w, conv_b, w_down, g_final):
    depth = w_in.shape[0]
    assert depth == 1, "layer stacking is not wired up: the problem has a single layer"
    B, S, D = x_prompt.shape
    N, T, _ = x_sample.shape
    assert T == 1, "the sample group decodes one token per sequence"
    gf = g_final.reshape(1, D).astype(F32)
    cos_p, sin_p = _rope_tables(S, 0, 1)
    cos_s, sin_s = _rope_tables(N, PAST_LEN, 0)

    l = 0
    W = _layer_weights(l, g_mix, w_in, g_out_a, g_out_b, sinks, w_out, g_cross, g_mem, w_xq, w_mem_kv, w_xo,
                       g_ffn, w_up, conv_w, conv_b, w_down)

    xs = x_sample.reshape(N, D)
    qa, ka, va, qb, kb, vb = _sample_inproj(xs, W, cos_s, sin_s)
    sample_a = (qa, ka, va, _time_minor(cache_a_k[l]), _time_minor(cache_a_v[l]))
    yp, (p_ak, p_av, p_bk, p_bv, p_mk, p_mv, p_conv), (oa_s, s_akt, s_avt) = _prompt_layer(
        x_prompt, mem_prompt, W, gf, cos_p, sin_p, sample_a)
    ys, (s_bkt, s_bvt, s_conv) = _sample_layer(
        xs, oa_s, qb, kb, vb, _time_minor(cache_b_k[l]), _time_minor(cache_b_v[l]),
        _mem_tile_rows(cache_mem_k[l]), _mem_tile_rows(cache_mem_v[l]), state_conv[l], W, gf)

    la, lb = p_ak.shape[1], p_bk.shape[1]
    return (yp, ys.reshape(N, 1, D),
            p_ak.reshape(1, B, la, A_HEADS, HEAD_DIM), p_av.reshape(1, B, la, A_HEADS, HEAD_DIM),
            p_bk.reshape(1, B, lb, B_KV_HEADS, HEAD_DIM), p_bv.reshape(1, B, lb, B_KV_HEADS, HEAD_DIM),
            p_mk.reshape(1, B, -1, X_HEADS, D // X_HEADS), p_mv.reshape(1, B, -1, X_HEADS, D // X_HEADS),
            p_conv[None],
            _time_major(s_akt, A_HEADS)[None], _time_major(s_avt, A_HEADS)[None],
            _time_major(s_bkt, B_KV_HEADS)[None], _time_major(s_bvt, B_KV_HEADS)[None],
            s_conv[None])
```

```python
import functools

import jax
import jax.numpy as jnp
import numpy as np
from jax import lax
from jax.experimental import pallas as pl
from jax.experimental.pallas import tpu as pltpu

F32 = jnp.float32
BF16 = jnp.bfloat16

HEAD_DIM = 64
A_HEADS = 8
B_HEADS = 8
B_KV_HEADS = 2
B_GROUP = B_HEADS // B_KV_HEADS
A_W = A_HEADS * HEAD_DIM
B_W = B_HEADS * HEAD_DIM
B_KV_W = B_KV_HEADS * HEAD_DIM
DILATIONS = (1, 4, 16)
A_WINDOW = 2048
B_WINDOW = 128
BLOCK = 128
ROPE_THETA = 10000.0
PAST_LEN = 16384
X_HEADS = 4
CONV_W = 3
EPS = 1e-6
NEG = -1e30
LOG2E = 1.4426950408889634
ATTN_UNROLL = 32
R4 = 4

V7X_LANES = 128
V7X_MXU_DIM = 256
V7X_VMEM_BYTES = 64 * 1024 * 1024
VMEM_LIMIT = V7X_VMEM_BYTES - 8 * 1024 * 1024


def _params(*sem):
    return pltpu.CompilerParams(dimension_semantics=sem, vmem_limit_bytes=VMEM_LIMIT)


def _rms(x, g):
    return (x * lax.rsqrt(jnp.mean(x * x, axis=-1, keepdims=True) + EPS)) * g


def _dot(a, b):
    return jnp.dot(a, b, preferred_element_type=F32)


def _dot_t(a, b):
    return lax.dot_general(a, b, (((1,), (1,)), ((), ())), preferred_element_type=F32)


def _rope_table_kernel(inv_ref, cos_ref, sin_ref, *, pos0, pos_step):
    rows = cos_ref.shape[0]
    row = lax.broadcasted_iota(jnp.int32, (rows, V7X_LANES), 0)
    lane = lax.broadcasted_iota(jnp.int32, (rows, V7X_LANES), 1)
    ang = (pos0 + pos_step * row).astype(F32) * inv_ref[...]
    first_half = (lane % HEAD_DIM) < (HEAD_DIM // 2)
    cos_ref[...] = jnp.cos(ang)
    sin_ref[...] = jnp.where(first_half, -jnp.sin(ang), jnp.sin(ang))


def _rope_tables(rows, pos0, pos_step):
    half = HEAD_DIM // 2
    inv = jnp.power(ROPE_THETA, -jnp.arange(half, dtype=F32) / half)
    inv = jnp.tile(inv, V7X_LANES // half).reshape(1, V7X_LANES)
    return pl.pallas_call(
        functools.partial(_rope_table_kernel, pos0=pos0, pos_step=pos_step),
        out_shape=[jax.ShapeDtypeStruct((rows, V7X_LANES), F32)] * 2,
        name="rope_tables",
    )(inv)


def _rope(slab, cos, sin, first_half):
    partner = jnp.where(first_half, pltpu.roll(slab, 96, 1), pltpu.roll(slab, 32, 1))
    return slab * cos + partner * sin


def _inproj_kernel(x_ref, g_ref, w_ref, cos_ref, sin_ref,
                   qa_ref, ka_ref, va_ref, qb_ref, kb_ref, vb_ref, *rest,
                   tail_skip, n_tiles, prompt):
    tm = x_ref.shape[0]
    hn = _rms(x_ref[...], g_ref[...]).astype(BF16)
    cos = cos_ref[...]
    sin = sin_ref[...]
    lane = lax.broadcasted_iota(jnp.int32, (tm, V7X_LANES), 1)
    first_half = (lane % HEAD_DIM) < (HEAD_DIM // 2)

    def seg(c0, width):
        return _dot(hn, w_ref[:, c0:c0 + width])

    def roped(z):
        return jnp.concatenate(
            [_rope(z[:, c:c + V7X_LANES], cos, sin, first_half)
             for c in range(0, z.shape[1], V7X_LANES)], axis=1)

    if prompt:
        kat_ref, vat_ref, kbt_ref, vbt_ref, shuf_ref = rest

        def put_a(dst_ref, z):
            for c in range(A_W // V7X_LANES):
                shuf_ref[c] = z[:, c * V7X_LANES:(c + 1) * V7X_LANES]
            for c in range(A_W // V7X_LANES):
                for r in range(R4):
                    dst_ref[r, :, c * V7X_LANES:(c + 1) * V7X_LANES] = shuf_ref[c, pl.ds(r, tm // R4, stride=R4), :]
    else:
        def put_a(dst_ref, z):
            dst_ref[...] = z

    put_a(qa_ref, roped(seg(0, A_W)))
    ka = roped(seg(A_W, A_W))
    put_a(ka_ref, ka)
    va = seg(2 * A_W, A_W)
    put_a(va_ref, va)
    qb_ref[...] = roped(seg(3 * A_W, B_W))
    kvb = seg(3 * A_W + B_W, 2 * B_KV_W)
    kb = roped(kvb[:, :B_KV_W])
    vb = kvb[:, B_KV_W:]
    kb_ref[...] = kb
    vb_ref[...] = vb

    if prompt:
        i = pl.program_id(1)

        @pl.when(i >= tail_skip)
        def _():
            kat_ref[...] = ka
            vat_ref[...] = va

        @pl.when(i == n_tiles - 1)
        def _():
            kbt_ref[...] = kb[tm - B_WINDOW:, :]
            vbt_ref[...] = vb[tm - B_WINDOW:, :]


def _inproj(x, g, w, cos, sin, *, tm, prompt):
    B, S, D = x.shape
    n_tiles = S // tm
    la = min(A_WINDOW, S)
    lb = min(B_WINDOW, S)
    tail_skip = (S - la) // tm
    tok = lambda w_: pl.BlockSpec((None, tm, w_), lambda b, i: (b, i, 0))
    const = lambda shape: pl.BlockSpec(shape, lambda b, i: (0,) * len(shape))
    if prompt:
        a_shape = jax.ShapeDtypeStruct((B, R4, S // R4, A_W), F32)
        a_spec = pl.BlockSpec((None, R4, tm // R4, A_W), lambda b, i: (b, 0, i, 0))
    else:
        a_shape = jax.ShapeDtypeStruct((B, S, A_W), F32)
        a_spec = tok(A_W)
    out_shape = [a_shape] * 3 + [
        jax.ShapeDtypeStruct((B, S, B_W), F32),
        jax.ShapeDtypeStruct((B, S, B_KV_W), F32),
        jax.ShapeDtypeStruct((B, S, B_KV_W), F32)]
    out_specs = [a_spec] * 3 + [tok(B_W), tok(B_KV_W), tok(B_KV_W)]
    scratch = []
    if prompt:
        out_shape += [jax.ShapeDtypeStruct((B, la, A_W), F32)] * 2
        out_shape += [jax.ShapeDtypeStruct((B, lb, B_KV_W), F32)] * 2
        a_tail = pl.BlockSpec((None, tm, A_W), lambda b, i: (b, jnp.maximum(i - tail_skip, 0), 0))
        b_tail = pl.BlockSpec((None, lb, B_KV_W), lambda b, i: (b, 0, 0))
        out_specs += [a_tail, a_tail, b_tail, b_tail]
        scratch = [pltpu.VMEM((A_W // V7X_LANES, tm, V7X_LANES), F32)]
    return pl.pallas_call(
        functools.partial(_inproj_kernel, tail_skip=tail_skip, n_tiles=n_tiles, prompt=prompt),
        grid=(B, n_tiles),
        in_specs=[tok(D), const((1, D)), const(w.shape),
                  pl.BlockSpec((tm, V7X_LANES), lambda b, i: (i, 0)),
                  pl.BlockSpec((tm, V7X_LANES), lambda b, i: (i, 0))],
        out_specs=out_specs,
        out_shape=out_shape,
        scratch_shapes=scratch,
        compiler_params=_params("arbitrary", "arbitrary"),
        name="inproj_rope",
    )(x, g, w, cos, sin)


def _band_bias(first, chunked):
    a = lax.broadcasted_iota(jnp.int32, (2 * BLOCK, 2 * BLOCK), 0) % BLOCK
    b = lax.broadcasted_iota(jnp.int32, (2 * BLOCK, 2 * BLOCK), 1)
    own = b >= BLOCK
    bb = b % BLOCK
    if chunked:
        sub = BLOCK // R4
        a = R4 * (a % sub) + a // sub
        bb = R4 * (bb % sub) + bb // sub
    dist = BLOCK + a - (bb + jnp.where(own, BLOCK, 0))
    valid = (dist >= 0) & (dist <= BLOCK)
    if first:
        valid = valid & own
    return jnp.where(valid, 0.0, NEG).astype(F32)


def _two_head_block(q, kk, vv, bias):
    lane = lax.broadcasted_iota(jnp.int32, (BLOCK, V7X_LANES), 1)
    lo = lane < HEAD_DIM
    q = q * (HEAD_DIM ** -0.5 * LOG2E)
    qs = jnp.concatenate([jnp.where(lo, q, 0.0), jnp.where(lo, 0.0, q)], axis=0).astype(BF16)
    s = _dot_t(qs, kk) + bias
    m = jnp.max(s, axis=-1, keepdims=True)
    p = jnp.exp2(s - m)
    v1 = jnp.concatenate([vv, jnp.ones_like(vv)], axis=1)
    pv = _dot(p.astype(BF16), v1)
    acc = jnp.where(lo, pv[:BLOCK, :V7X_LANES], pv[BLOCK:, :V7X_LANES])
    l2 = jnp.where(lo, pv[:BLOCK, V7X_LANES:], pv[BLOCK:, V7X_LANES:])
    m2 = jnp.where(lo, m[:BLOCK], m[BLOCK:])
    return acc, m2, l2


def _get(ref, slices):
    return jnp.concatenate([ref[sl, :] for sl in slices], axis=0) if len(slices) > 1 else ref[slices[0], :]


def _put(ref, slices, val):
    off = 0
    for sl in slices:
        ref[sl, :] = val[off:off + sl.size]
        off += sl.size


def _block_slices(layout, d, S, i):
    if layout == "seq":
        start = pl.multiple_of(i * BLOCK, BLOCK)
        prev = pl.multiple_of(jnp.maximum(i - 1, 0) * BLOCK, BLOCK)
        return [pl.ds(start, BLOCK)], [pl.ds(prev, BLOCK)], i == 0
    Sr = S // R4
    if d == 1:
        sub = BLOCK // R4
        pj = jnp.maximum(i - 1, 0)
        own = [pl.ds(pl.multiple_of(r * Sr + i * sub, sub), sub) for r in range(R4)]
        prev = [pl.ds(pl.multiple_of(r * Sr + pj * sub, sub), sub) for r in range(R4)]
        return own, prev, i == 0
    if d == R4:
        per_res = Sr // BLOCK
        jb = i % per_res
        start = pl.multiple_of(i * BLOCK, BLOCK)
        prev = pl.multiple_of(jnp.where(jb == 0, i, i - 1) * BLOCK, BLOCK)
        return [pl.ds(start, BLOCK)], [pl.ds(prev, BLOCK)], jb == 0
    step = d // R4
    per_res = S // (BLOCK * d)
    rd = i // per_res
    jb = i % per_res
    start = (rd % R4) * Sr + rd // R4 + jb * (BLOCK * step)
    prev = jnp.where(jb == 0, start, start - BLOCK * step)
    return [pl.ds(start, BLOCK, stride=step)], [pl.ds(prev, BLOCK, stride=step)], jb == 0


def _branch_blocks(q_ref, k_ref, v_ref, bias_ref, layout, d, visit):
    S = q_ref.shape[0]
    bias_base = 2 if (layout == "r4" and d == 1) else 0

    def body(i, carry):
        own, prev, first = _block_slices(layout, d, S, i)
        kk = jnp.concatenate([_get(k_ref, prev), _get(k_ref, own)], axis=0).astype(BF16)
        vv = jnp.concatenate([_get(v_ref, prev), _get(v_ref, own)], axis=0).astype(BF16)
        bias = bias_ref[bias_base + jnp.where(first, 1, 0)]
        acc, m, l = _two_head_block(_get(q_ref, own), kk, vv, bias)
        visit(own, acc, m, l)
        return carry

    lax.fori_loop(0, S // BLOCK, body, 0, unroll=ATTN_UNROLL)


def _init_bias(bias_ref, chunked_too):
    bias_ref[0] = _band_bias(False, False)
    bias_ref[1] = _band_bias(True, False)
    if chunked_too:
        bias_ref[2] = _band_bias(False, True)
        bias_ref[3] = _band_bias(True, True)


def _attn_a_kernel(q_ref, k_ref, v_ref, o_ref, m_ref, l_ref, bias_ref):
    _init_bias(bias_ref, True)

    def first_visit(rows, acc, m, l):
        _put(o_ref, rows, acc)
        _put(m_ref, rows, m)
        _put(l_ref, rows, l)

    def merged(rows, acc, m, l):
        m_old = _get(m_ref, rows)
        m_new = jnp.maximum(m_old, m)
        a_old = jnp.exp2(m_old - m_new)
        a_new = jnp.exp2(m - m_new)
        return (a_old * _get(o_ref, rows) + a_new * acc, m_new, a_old * _get(l_ref, rows) + a_new * l)

    def mid_visit(rows, acc, m, l):
        acc, m, l = merged(rows, acc, m, l)
        _put(o_ref, rows, acc)
        _put(m_ref, rows, m)
        _put(l_ref, rows, l)

    def last_visit(rows, acc, m, l):
        acc, m, l = merged(rows, acc, m, l)
        _put(o_ref, rows, acc / l)

    order = (R4, 1, 16)
    assert sorted(order) == sorted(DILATIONS)
    visits = [first_visit] + [mid_visit] * (len(order) - 2) + [last_visit]
    for d, visit in zip(order, visits):
        _branch_blocks(q_ref, k_ref, v_ref, bias_ref, "r4", d, visit)


def _attn_b_kernel(q_ref, k_ref, v_ref, sink_ref, o_ref, bias_ref):
    _init_bias(bias_ref, False)
    sink = sink_ref[0:1, :] * LOG2E

    def visit(rows, acc, m, l):
        _put(o_ref, rows, acc / (l + jnp.exp2(sink - m)))

    _branch_blocks(q_ref, k_ref, v_ref, bias_ref, "seq", 1, visit)


def _attn_a(q, k, v):
    B, S, W = q.shape
    slab = pl.BlockSpec((None, S, V7X_LANES), lambda b, j: (b, 0, j))
    return pl.pallas_call(
        _attn_a_kernel,
        grid=(B, W // V7X_LANES),
        in_specs=[slab, slab, slab],
        out_specs=slab,
        out_shape=jax.ShapeDtypeStruct((B, S, W), F32),
        scratch_shapes=[pltpu.VMEM((S, V7X_LANES), F32), pltpu.VMEM((S, V7X_LANES), F32),
                        pltpu.VMEM((4, 2 * BLOCK, 2 * BLOCK), F32)],
        compiler_params=_params("arbitrary", "arbitrary"),
        name="mixer_a_attention",
    )(q, k, v)


def _attn_b(q, k, v, sink_lanes):
    B, S, W = q.shape
    slab = pl.BlockSpec((None, S, V7X_LANES), lambda b, j: (b, 0, j))
    kv = pl.BlockSpec((None, S, V7X_LANES), lambda b, j: (b, 0, 0))
    return pl.pallas_call(
        _attn_b_kernel,
        grid=(B, W // V7X_LANES),
        in_specs=[slab, kv, kv, pl.BlockSpec((None, 8, V7X_LANES), lambda b, j: (j, 0, 0))],
        out_specs=slab,
        out_shape=jax.ShapeDtypeStruct((B, S, W), F32),
        scratch_shapes=[pltpu.VMEM((2, 2 * BLOCK, 2 * BLOCK), F32)],
        compiler_params=_params("arbitrary", "arbitrary"),
        name="mixer_b_attention",
    )(q, k, v, sink_lanes)


def _as_column(row):
    return jnp.broadcast_to(row, (V7X_LANES, row.shape[1])).T


def _roll_in(dst_ref, cache, new_col):
    L = cache.shape[1]
    rolled = pltpu.roll(cache, L - 1, 1)
    lane = lax.broadcasted_iota(jnp.int32, (cache.shape[0], V7X_LANES), 1)
    if L > V7X_LANES:
        dst_ref[:, :L - V7X_LANES] = rolled[:, :L - V7X_LANES]
    dst_ref[:, L - V7X_LANES:] = jnp.where(lane == V7X_LANES - 1, new_col, rolled[:, L - V7X_LANES:])


def _sample_a_unit(q, k_new, v_new, kt, vt, w, okt_ref, ovt_ref):
    R, L = kt.shape
    H = R // HEAD_DIM
    n_br = len(DILATIONS)
    qc = _as_column(q * (HEAD_DIM ** -0.5))
    knc = _as_column(k_new)
    vnc = _as_column(v_new)
    prod = jnp.concatenate([kt[:, j:j + V7X_LANES] * qc for j in range(0, L, V7X_LANES)], axis=1)
    s = jnp.sum(prod.reshape(H, HEAD_DIM, L), axis=1)
    s_new = jnp.sum((qc * knc).reshape(H, HEAD_DIM, V7X_LANES), axis=1)[:, 0:1]
    s = jnp.where(w > 0, s, NEG)
    m = jnp.maximum(jnp.max(s, axis=-1, keepdims=True), s_new)
    p = w * jnp.exp(s - m)
    p_new = n_br * jnp.exp(s_new - m)
    l = jnp.sum(p, axis=-1, keepdims=True) + p_new

    def per_row(a):
        return jnp.broadcast_to(a.reshape(H, 1, a.shape[1]), (H, HEAD_DIM, a.shape[1])).reshape(R, a.shape[1])

    o_col = (jnp.sum(vt * per_row(p), axis=-1, keepdims=True) + per_row(p_new) * vnc[:, 0:1]) / per_row(l)
    _roll_in(okt_ref, kt, knc)
    _roll_in(ovt_ref, vt, vnc)
    return jnp.broadcast_to(o_col, (R, V7X_LANES)).T[0:1, :]


def _branch_multiplicity(L):
    dist = L - np.arange(L)
    w = sum(((dist % d == 0) & (dist <= BLOCK * d)).astype(np.float32) for d in DILATIONS)
    return jnp.asarray(w.reshape(1, L))


CACHE_HALVES = 2


IN_SLOTS = 2
OUT_SLOTS = 2


class _CacheStream:
    def __init__(self, step, n_steps, per_step, unit0, hbm, ring, sem):
        self.step, self.per_step, self.unit0 = step, per_step, unit0
        self.total = n_steps * per_step
        self.n_steps = n_steps
        self.kt_hbm, self.vt_hbm, self.okt_hbm, self.ovt_hbm = hbm
        self.ink, self.inv, self.outk, self.outv = ring
        self.sem = sem
        self.R = self.ink.shape[1]

    def _slot(self, u, depth):
        assert self.per_step % depth == 0
        return u % depth

    def _copies(self, u, shift, inbound):
        depth = IN_SLOTS if inbound else OUT_SLOTS
        assert shift % depth == 0
        g = self.step * self.per_step + u + shift
        unit = self.unit0 + g
        slot = self._slot(u, depth)
        n = unit // CACHE_HALVES
        rows = pl.ds(pl.multiple_of((unit % CACHE_HALVES) * self.R, self.R), self.R)
        if inbound:
            return (pltpu.make_async_copy(self.kt_hbm.at[n, rows, :], self.ink.at[slot], self.sem.at[0, slot]),
                    pltpu.make_async_copy(self.vt_hbm.at[n, rows, :], self.inv.at[slot], self.sem.at[1, slot]))
        return (pltpu.make_async_copy(self.outk.at[slot], self.okt_hbm.at[n, rows, :], self.sem.at[2, slot]),
                pltpu.make_async_copy(self.outv.at[slot], self.ovt_hbm.at[n, rows, :], self.sem.at[3, slot]))

    def _start(self, u, shift, inbound):
        for cp in self._copies(u, shift, inbound):
            cp.start()

    def _wait(self, u, shift, inbound):
        for cp in self._copies(u, shift, inbound):
            cp.wait()

    def prologue(self):
        @pl.when(self.step == 0)
        def _():
            for u in range(IN_SLOTS):
                self._start(u, 0, True)

    def wait_unit(self, u):
        self._wait(u, 0, True)
        if u >= OUT_SLOTS:
            self._wait(u, -OUT_SLOTS, False)
        else:
            @pl.when(self.step > 0)
            def _():
                self._wait(u, -OUT_SLOTS, False)

    def run_unit(self, u, q_ref, kn_ref, vn_ref, w_ref, o_ref):
        g = self.step * self.per_step + u
        islot = self._slot(u, IN_SLOTS)
        oslot = self._slot(u, OUT_SLOTS)
        o_ref[g] = _sample_a_unit(q_ref[g], kn_ref[g], vn_ref[g], self.ink[islot], self.inv[islot], w_ref[...],
                                  self.outk.at[oslot], self.outv.at[oslot])
        self._start(u, 0, False)
        if u + IN_SLOTS < self.per_step:
            self._start(u, IN_SLOTS, True)
        else:
            @pl.when(self.step < self.n_steps - 1)
            def _():
                self._start(u, IN_SLOTS, True)

    def epilogue(self):
        @pl.when(self.step == self.n_steps - 1)
        def _():
            for u in range(self.per_step - OUT_SLOTS, self.per_step):
                self._wait(u, 0, False)


def _stream_specs(n_units, R, L):
    rows = pl.BlockSpec((n_units, 1, R), lambda *_: (0, 0, 0))
    hbm = pl.BlockSpec(memory_space=pl.ANY)
    ring_in = pltpu.VMEM((IN_SLOTS, R, L), F32)
    ring_out = pltpu.VMEM((OUT_SLOTS, R, L), F32)
    in_specs = [rows, rows, rows, pl.BlockSpec((1, L), lambda *_: (0, 0)), hbm, hbm]
    out_specs = [rows, hbm, hbm]
    scratch = [ring_in, ring_in, ring_out, ring_out, pltpu.SemaphoreType.DMA((4, max(IN_SLOTS, OUT_SLOTS)))]
    return in_specs, out_specs, scratch


def _sample_b_kernel(q_ref, kn_ref, vn_ref, kt_ref, vt_ref, sink_ref, o_ref, okt_ref, ovt_ref):
    nb = q_ref.shape[0]
    lane = lax.broadcasted_iota(jnp.int32, (1, V7X_LANES), 1)
    lo = lane < HEAD_DIM
    for n in range(nb):
        q = q_ref[n:n + 1, :] * (HEAD_DIM ** -0.5)
        rows = []
        for j in range(B_W // V7X_LANES):
            pair = q[:, j * V7X_LANES:(j + 1) * V7X_LANES]
            rows += [jnp.where(lo, pair, 0.0), jnp.where(lo, 0.0, pair)]
        qh = jnp.concatenate(rows, axis=0)
        kt = kt_ref[n]
        vt = vt_ref[n]
        s_new = jnp.sum(qh * kn_ref[n:n + 1, :], axis=-1, keepdims=True)
        s = _dot(qh.astype(BF16), kt.astype(BF16))
        m = jnp.maximum(s_new, jnp.max(s, axis=-1, keepdims=True))
        p_new = jnp.exp(s_new - m)
        p = jnp.exp(s - m)
        l = p_new + jnp.sum(p, axis=-1, keepdims=True)
        o = p_new * vn_ref[n:n + 1, :] + _dot_t(p.astype(BF16), vt.astype(BF16))
        o = o / (l + jnp.exp(sink_ref[...] - m))
        o_ref[n:n + 1, :] = jnp.concatenate(
            [jnp.where(lo, o[2 * j:2 * j + 1], o[2 * j + 1:2 * j + 2]) for j in range(B_W // V7X_LANES)], axis=1)
        _roll_in(okt_ref.at[n], kt, _as_column(kn_ref[n:n + 1, :]))
        _roll_in(ovt_ref.at[n], vt, _as_column(vn_ref[n:n + 1, :]))


def _sample_b(q, k_new, v_new, kt, vt, sink_rows, *, nb):
    N, W, L = kt.shape
    assert L == B_WINDOW and W == V7X_LANES, "a cache shorter than the sliding window is unsupported"
    row = lambda w_: pl.BlockSpec((nb, w_), lambda i: (i, 0))
    cache = pl.BlockSpec((nb, W, L), lambda i: (i, 0, 0))
    return pl.pallas_call(
        _sample_b_kernel,
        grid=(N // nb,),
        in_specs=[row(B_W), row(W), row(W), cache, cache, pl.BlockSpec((8, 1), lambda i: (0, 0))],
        out_specs=[row(B_W), cache, cache],
        out_shape=[jax.ShapeDtypeStruct((N, B_W), F32), jax.ShapeDtypeStruct((N, W, L), F32),
                   jax.ShapeDtypeStruct((N, W, L), F32)],
        compiler_params=_params("arbitrary"),
        name="sample_mixer_b",
    )(q, k_new, v_new, kt, vt, sink_rows)


def _mixers_out(oa, ob, x, ga_ref, gb_ref, w_ref):
    ha = _rms(oa, ga_ref[...]).astype(BF16)
    hb = _rms(ob, gb_ref[...]).astype(BF16)
    return x + _dot(ha, w_ref[:A_W, :]) + _dot(hb, w_ref[A_W:, :])


def _outproj_kernel(oa_ref, ob_ref, x_ref, ga_ref, gb_ref, w_ref, y_ref):
    y_ref[...] = _mixers_out(oa_ref[...], ob_ref[...], x_ref[...], ga_ref, gb_ref, w_ref)


def _outproj(oa, ob, x, ga, gb, w):
    return pl.pallas_call(
        _outproj_kernel,
        out_shape=jax.ShapeDtypeStruct(x.shape, F32),
        compiler_params=pltpu.CompilerParams(vmem_limit_bytes=VMEM_LIMIT),
        name="mixers_outproj",
    )(oa, ob, x, ga, gb, w)


def _norm_matmul_kernel(x_ref, g_ref, w_ref, *out_refs):
    h = _rms(x_ref[...], g_ref[...]).astype(BF16)
    width = out_refs[0].shape[1]
    for j, o_ref in enumerate(out_refs):
        o_ref[...] = _dot(h, w_ref[:, j * width:(j + 1) * width])


def _norm_matmul(x, g, w, *, n_out, tm):
    T, D = x.shape
    width = w.shape[1] // n_out
    return pl.pallas_call(
        _norm_matmul_kernel,
        grid=(T // tm,),
        in_specs=[pl.BlockSpec((tm, D), lambda i: (i, 0)), pl.BlockSpec((1, D), lambda i: (0, 0)),
                  pl.BlockSpec(w.shape, lambda i: (0, 0))],
        out_specs=[pl.BlockSpec((tm, width), lambda i: (i, 0))] * n_out,
        out_shape=[jax.ShapeDtypeStruct((T, width), F32)] * n_out,
        compiler_params=_params("arbitrary"),
        name="norm_matmul",
    )(x, g, w)


def _matmul_residual_kernel(a_ref, w_ref, r_ref, y_ref):
    y_ref[...] = r_ref[...] + _dot(a_ref[...].astype(BF16), w_ref[...])


def _matmul_residual(a, w, r, *, tm):
    T, K = a.shape
    D = w.shape[1]
    return pl.pallas_call(
        _matmul_residual_kernel,
        grid=(T // tm,),
        in_specs=[pl.BlockSpec((tm, K), lambda i: (i, 0)), pl.BlockSpec(w.shape, lambda i: (0, 0)),
                  pl.BlockSpec((tm, D), lambda i: (i, 0))],
        out_specs=pl.BlockSpec((tm, D), lambda i: (i, 0)),
        out_shape=jax.ShapeDtypeStruct((T, D), F32),
        compiler_params=_params("arbitrary"),
        name="matmul_residual",
    )(a, w, r)


def _outproj_cross_kernel(oa_ref, ob_ref, x_ref, ga_ref, gb_ref, wout_ref, gc_ref, wq_ref, mk_ref, mv_ref, wo_ref,
                          y_ref, shuf_ref, q_ref):
    tq = oa_ref.shape[1]
    for c in range(A_W // V7X_LANES):
        for r in range(R4):
            shuf_ref[c, pl.ds(r, tq, stride=R4), :] = oa_ref[r, :, c * V7X_LANES:(c + 1) * V7X_LANES]
    oa = jnp.concatenate([shuf_ref[c] for c in range(A_W // V7X_LANES)], axis=1)
    y_ref[...] = _mixers_out(oa, ob_ref[...], x_ref[...], ga_ref, gb_ref, wout_ref)
    D = y_ref.shape[1]
    hd = D // X_HEADS
    q_ref[...] = (_dot(_rms(y_ref[...], gc_ref[...]).astype(BF16), wq_ref[...]) * (hd ** -0.5)).astype(BF16)
    for h in range(X_HEADS):
        cols = slice(h * hd, (h + 1) * hd)
        s = _dot_t(q_ref[:, cols], mk_ref[:, cols].astype(BF16))
        p = jnp.exp(s - jnp.max(s, axis=-1, keepdims=True))
        o = _dot(p.astype(BF16), mv_ref[:, cols].astype(BF16)) / jnp.sum(p, axis=-1, keepdims=True)
        y_ref[...] += _dot(o.astype(BF16), wo_ref[cols, :])


def _outproj_cross(oa, ob, x, ga, gb, wout, gc, wq, mk, mv, wo, *, tm):
    B, S, D = x.shape
    M = mk.shape[1]
    tok = lambda w_: pl.BlockSpec((None, tm, w_), lambda b, i: (b, i, 0))
    mem = pl.BlockSpec((None, M, D), lambda b, i: (b, 0, 0))
    const = lambda shape: pl.BlockSpec(shape, lambda b, i: (0,) * len(shape))
    return pl.pallas_call(
        _outproj_cross_kernel,
        grid=(B, S // tm),
        in_specs=[pl.BlockSpec((None, R4, tm // R4, A_W), lambda b, i: (b, 0, i, 0)), tok(B_W), tok(D),
                  const((1, A_W)), const((1, B_W)), const(wout.shape), const((1, D)), const(wq.shape),
                  mem, mem, const(wo.shape)],
        out_specs=tok(D),
        out_shape=jax.ShapeDtypeStruct((B, S, D), F32),
        scratch_shapes=[pltpu.VMEM((A_W // V7X_LANES, tm, V7X_LANES), F32), pltpu.VMEM((tm, D), BF16)],
        compiler_params=_params("arbitrary", "arbitrary"),
        name="outproj_cross_attention",
    )(oa, ob, x, ga, gb, wout, gc, wq, mk, mv, wo)


MEM_TILE_ROWS = 8


def _sample_cross_kernel(q_ref, mk_ref, mv_ref, o_ref):
    nb = q_ref.shape[0]
    M = mk_ref.shape[1] // MEM_TILE_ROWS
    for n in range(nb):
        q = q_ref[n]
        k3 = mk_ref[n].reshape(M, MEM_TILE_ROWS, V7X_LANES)
        v3 = mv_ref[n].reshape(M, MEM_TILE_ROWS, V7X_LANES)
        part = jnp.sum(k3 * q[None], axis=-1, keepdims=True)
        s = part + pltpu.roll(part, X_HEADS, 1)
        p = jnp.exp(s - jnp.max(s, axis=0, keepdims=True))
        o = jnp.sum(v3 * p, axis=0) / jnp.sum(p, axis=0)
        o_ref[n] = o


def _sample_cross(q, mk, mv, *, nb):
    N, R, _ = mk.shape
    mem = pl.BlockSpec((nb, R, V7X_LANES), lambda i: (i, 0, 0))
    row = pl.BlockSpec((nb, MEM_TILE_ROWS, V7X_LANES), lambda i: (i, 0, 0))
    return pl.pallas_call(
        _sample_cross_kernel,
        grid=(N // nb,),
        in_specs=[row, mem, mem],
        out_specs=row,
        out_shape=jax.ShapeDtypeStruct((N, MEM_TILE_ROWS, V7X_LANES), F32),
        compiler_params=_params("arbitrary"),
        name="sample_cross_attention",
    )(q, mk, mv)


def _mem_tile_rows(cache):
    N, M, H, Dh = cache.shape
    halves = Dh // V7X_LANES
    assert H * halves == MEM_TILE_ROWS
    return cache.reshape(N, M, H, halves, V7X_LANES).transpose(0, 1, 3, 2, 4).reshape(N, M * MEM_TILE_ROWS, V7X_LANES)


FF_CHUNKS = ((0, 1024), (1024, 1024), (2048, 768))


def _ffn_chunk(h, wup_ref, wdn_ref, cw_ref, cb_ref, d_ff, c0, cw, prev_rows):
    gate = _dot(h, wup_ref[:, c0:c0 + cw])
    val = _dot(h, wup_ref[:, d_ff + c0:d_ff + c0 + cw])
    g2, g1 = prev_rows(gate)
    conv = cb_ref[:, c0:c0 + cw] + (g2 * cw_ref[0:1, c0:c0 + cw] + g1 * cw_ref[1:2, c0:c0 + cw]
                                     + gate * cw_ref[2:3, c0:c0 + cw])
    act = (conv * (1.0 / (1.0 + jnp.exp(-conv)))) * val
    return gate, _dot(act.astype(BF16), wdn_ref[c0:c0 + cw, :])


FFN_UNITS = 4
FF_CHUNKS_PROMPT = ((0, 768), (768, 768), (1536, 768), (2304, 512))


def _ffn_prompt_kernel(x_ref, g_ref, wup_ref, cw_ref, cb_ref, wdn_ref, gf_ref,
                       sq_ref, skn_ref, svn_ref, sw_ref, kt_hbm, vt_hbm,
                       y_ref, state_ref, so_ref, okt_hbm, ovt_hbm,
                       halo_ref, ink_ref, inv_ref, outk_ref, outv_ref, sem, *, n_tiles, n_steps):
    i = pl.program_id(1)
    step = pl.program_id(0) * n_tiles + i
    tm = x_ref.shape[0]
    d_ff = wdn_ref.shape[0]
    stream = _CacheStream(step, n_steps, FFN_UNITS, 0, (kt_hbm, vt_hbm, okt_hbm, ovt_hbm),
                          (ink_ref, inv_ref, outk_ref, outv_ref), sem)

    @pl.when(i == 0)
    def _():
        halo_ref[...] = jnp.zeros_like(halo_ref)

    stream.prologue()
    x = x_ref[...]
    h = _rms(x, g_ref[...]).astype(BF16)
    y = x
    for u, (c0, cw) in enumerate(FF_CHUNKS_PROMPT):
        stream.wait_unit(u)
        row = lax.broadcasted_iota(jnp.int32, (tm, cw), 0)
        halo = halo_ref[:, c0:c0 + cw]

        def prev_rows(gate):
            g1 = jnp.where(row == 0, halo[1:2], pltpu.roll(gate, 1, 0))
            g2 = jnp.where(row == 0, halo[0:1], jnp.where(row == 1, halo[1:2], pltpu.roll(gate, 2, 0)))
            return g2, g1

        gate, contrib = _ffn_chunk(h, wup_ref, wdn_ref, cw_ref, cb_ref, d_ff, c0, cw, prev_rows)
        y = y + contrib
        halo_ref[0:2, c0:c0 + cw] = gate[tm - 2:, :]
        state_ref[:, c0:c0 + cw] = gate[tm - 2:, :]
        stream.run_unit(u, sq_ref, skn_ref, svn_ref, sw_ref, so_ref)

    y_ref[...] = _rms(y, gf_ref[...])
    stream.epilogue()


def _ffn_prompt(x, g, wup, cw, cb, wdn, gf, sq, skn, svn, kt, vt, *, tm):
    B, S, D = x.shape
    d_ff = wdn.shape[0]
    N, W, L = kt.shape
    assert FF_CHUNKS_PROMPT[-1][0] + FF_CHUNKS_PROMPT[-1][1] == d_ff and len(FF_CHUNKS_PROMPT) == FFN_UNITS
    assert L == A_WINDOW, "a cache shorter than the largest dilated window is unsupported"
    n_tiles = S // tm
    n_steps = B * n_tiles
    n_units = n_steps * FFN_UNITS
    R = W // CACHE_HALVES
    assert sq.shape == (n_units, 1, R) and n_units == N * CACHE_HALVES, "the cache units are spread evenly over the grid"
    tok = pl.BlockSpec((None, tm, D), lambda b, i: (b, i, 0))
    const = lambda shape: pl.BlockSpec(shape, lambda b, i: (0,) * len(shape))
    resident = lambda shape: pl.BlockSpec(shape, lambda b, i: (0,) * len(shape), pipeline_mode=pl.Buffered(1))
    s_in, s_out, s_scratch = _stream_specs(n_units, R, L)
    return pl.pallas_call(
        functools.partial(_ffn_prompt_kernel, n_tiles=n_tiles, n_steps=n_steps),
        grid=(B, n_tiles),
        in_specs=[tok, const((1, D)), resident(wup.shape), const(cw.shape), const(cb.shape),
                  resident(wdn.shape), const((1, D))] + s_in,
        out_specs=[tok, pl.BlockSpec((None, CONV_W - 1, d_ff), lambda b, i: (b, 0, 0))] + s_out,
        out_shape=[jax.ShapeDtypeStruct((B, S, D), F32),
                   jax.ShapeDtypeStruct((B, CONV_W - 1, d_ff), F32),
                   jax.ShapeDtypeStruct((n_units, 1, R), F32),
                   jax.ShapeDtypeStruct((N, W, L), F32), jax.ShapeDtypeStruct((N, W, L), F32)],
        scratch_shapes=[pltpu.VMEM((8, d_ff), F32)] + s_scratch,
        compiler_params=_params("arbitrary", "arbitrary"),
        name="conv_ffn_prompt_sample_a",
    )(x, g, wup, cw, cb, wdn, gf, sq, skn, svn, _branch_multiplicity(L), kt, vt)


def _ffn_sample_kernel(x_ref, s0_ref, s1_ref, g_ref, wup_ref, cw_ref, cb_ref, wdn_ref, gf_ref,
                       y_ref, gate_ref):
    d_ff = wdn_ref.shape[0]
    x = x_ref[...]
    h = _rms(x, g_ref[...]).astype(BF16)
    y = x
    for c0, cw in FF_CHUNKS:
        prev_rows = lambda gate: (s0_ref[:, c0:c0 + cw], s1_ref[:, c0:c0 + cw])
        gate, contrib = _ffn_chunk(h, wup_ref, wdn_ref, cw_ref, cb_ref, d_ff, c0, cw, prev_rows)
        y = y + contrib
        gate_ref[:, c0:c0 + cw] = gate
    y_ref[...] = _rms(y, gf_ref[...])


def _ffn_sample(x, s0, s1, g, wup, cw, cb, wdn, gf):
    N, D = x.shape
    d_ff = wdn.shape[0]
    return pl.pallas_call(
        _ffn_sample_kernel,
        out_shape=[jax.ShapeDtypeStruct((N, D), F32), jax.ShapeDtypeStruct((N, d_ff), F32)],
        compiler_params=pltpu.CompilerParams(vmem_limit_bytes=VMEM_LIMIT),
        name="conv_ffn_sample",
    )(x, s0, s1, g, wup, cw, cb, wdn, gf)


def _qb_pair_perm():
    cols = []
    for j in range(B_GROUP):
        for hk in range(B_KV_HEADS):
            h = hk * B_GROUP + j
            cols.extend(range(h * HEAD_DIM, (h + 1) * HEAD_DIM))
    return np.asarray(cols, dtype=np.int32)


def _layer_weights(l, g_mix, w_in, g_out_a, g_out_b, sinks, w_out, g_cross, g_mem, w_xq, w_mem_kv, w_xo,
                   g_ffn, w_up, conv_w, conv_b, w_down):
    perm = _qb_pair_perm()
    qb0 = 3 * A_W
    w_in_l = w_in[l]
    w_in_l = jnp.concatenate([w_in_l[:, :qb0], w_in_l[:, qb0:qb0 + B_W][:, perm], w_in_l[:, qb0 + B_W:]], axis=1)
    w_out_l = w_out[l]
    w_out_l = jnp.concatenate([w_out_l[:A_W], w_out_l[A_W:][perm]], axis=0)
    sink = sinks[l].astype(F32)
    pair = jnp.stack([sink[:B_GROUP], sink[B_GROUP:]], axis=1)
    sink_lanes = jnp.broadcast_to(jnp.repeat(pair, HEAD_DIM, axis=1)[:, None, :], (B_GROUP, 8, V7X_LANES))
    sink_rows = pair.reshape(2 * B_GROUP, 1)
    row = lambda v: v.reshape(1, -1).astype(F32)
    return dict(
        g_mix=row(g_mix[l]), w_in=w_in_l.astype(BF16),
        g_out_a=row(g_out_a[l]), g_out_b=row(g_out_b[l][perm]), w_out=w_out_l.astype(BF16),
        sink_lanes=sink_lanes, sink_rows=sink_rows,
        g_cross=row(g_cross[l]), g_mem=row(g_mem[l]), w_xq=w_xq[l].astype(BF16),
        w_mem_kv=w_mem_kv[l].astype(BF16), w_xo=w_xo[l].astype(BF16),
        g_ffn=row(g_ffn[l]), w_up=w_up[l].astype(BF16), conv_w=conv_w[l].astype(F32),
        conv_b=row(conv_b[l]), w_down=w_down[l].astype(BF16))


PROMPT_TM = 512
PROMPT_TM_WIDE = 1024
SAMPLE_NB = 8


def _prompt_layer(x, mem, W, g_final, cos, sin, sample_a):
    B, S, D = x.shape
    M = mem.shape[1]
    tm = min(PROMPT_TM, S)
    tm_wide = min(PROMPT_TM_WIDE, S)
    qa, ka, va, qb, kb, vb, ka_t, va_t, kb_t, vb_t = _inproj(
        x, W["g_mix"], W["w_in"], cos, sin, tm=tm_wide, prompt=True)
    flat = lambda t: t.reshape(B, S, A_W)
    oa = _attn_a(flat(qa), flat(ka), flat(va)).reshape(B, R4, S // R4, A_W)
    ob = _attn_b(qb, kb, vb, W["sink_lanes"])
    mk, mv = _norm_matmul(mem.reshape(B * M, D), W["g_mem"], W["w_mem_kv"], n_out=2, tm=min(512, B * M))
    mk = mk.reshape(B, M, D)
    mv = mv.reshape(B, M, D)
    x2 = _outproj_cross(oa, ob, x, W["g_out_a"], W["g_out_b"], W["w_out"], W["g_cross"], W["w_xq"], mk, mv,
                        W["w_xo"], tm=tm_wide)
    sq, skn, svn, kt, vt = sample_a
    N, W_, L = kt.shape
    unit_rows = lambda t: t.reshape(N * CACHE_HALVES, 1, W_ // CACHE_HALVES)
    y, conv_state, o_units, s_akt, s_avt = _ffn_prompt(
        x2, W["g_ffn"], W["w_up"], W["conv_w"], W["conv_b"], W["w_down"], g_final,
        unit_rows(sq), unit_rows(skn), unit_rows(svn), kt, vt, tm=tm)
    return y, (ka_t, va_t, kb_t, vb_t, mk, mv, conv_state), (o_units.reshape(N, W_), s_akt, s_avt)


def _sample_inproj(x, W, cos, sin):
    N = x.shape[0]
    return [t[0] for t in _inproj(x[None], W["g_mix"], W["w_in"], cos, sin, tm=N, prompt=False)]


def _sample_layer(x, oa, qb, kb, vb, b_kt, b_vt, mem_k, mem_v, conv_state, W, g_final):
    N, D = x.shape
    ob, s_bkt, s_bvt = _sample_b(qb, kb, vb, b_kt, b_vt, W["sink_rows"], nb=SAMPLE_NB)
    x1 = _outproj(oa, ob, x, W["g_out_a"], W["g_out_b"], W["w_out"])
    (q,) = _norm_matmul(x1, W["g_cross"], W["w_xq"], n_out=1, tm=N)
    hd = D // X_HEADS
    halves = hd // V7X_LANES
    q = (q * (hd ** -0.5)).reshape(N, X_HEADS, halves, V7X_LANES).transpose(0, 2, 1, 3)
    o = _sample_cross(q.reshape(N, MEM_TILE_ROWS, V7X_LANES), mem_k, mem_v, nb=SAMPLE_NB)
    o = o.reshape(N, halves, X_HEADS, V7X_LANES).transpose(0, 2, 1, 3).reshape(N, D)
    x2 = _matmul_residual(o, W["w_xo"], x1, tm=N)
    y, gate = _ffn_sample(x2, conv_state[:, 0], conv_state[:, 1], W["g_ffn"], W["w_up"], W["conv_w"],
                          W["conv_b"], W["w_down"], g_final)
    new_conv = jnp.stack([conv_state[:, 1], gate], axis=1)
    return y, (s_bkt, s_bvt, new_conv)


def _time_minor(cache):
    N, L, H, Dh = cache.shape
    return cache.transpose(0, 2, 3, 1).reshape(N, H * Dh, L)


def _time_major(cache_t, H):
    N, W, L = cache_t.shape
    return cache_t.reshape(N, H, W // H, L).transpose(0, 3, 1, 2)


def kernel(x_prompt, x_sample, cache_a_k, cache_a_v, cache_b_k, cache_b_v, cache_mem_k, cache_mem_v, state_conv,
           mem_prompt, g_mix, w_in, g_out_a, g_out_b, sinks, w_out, g_cross, g_mem, w_xq, w_mem_kv, w_xo,
           g_ffn, w_up, conv_w, conv_b, w_down, g_final):
    depth = w_in.shape[0]
    assert depth == 1, "layer stacking is not wired up: the problem has a single layer"
    B, S, D = x_prompt.shape
    N, T, _ = x_sample.shape
    assert T == 1, "the sample group decodes one token per sequence"
    gf = g_final.reshape(1, D).astype(F32)
    cos_p, sin_p = _rope_tables(S, 0, 1)
    cos_s, sin_s = _rope_tables(N, PAST_LEN, 0)

    l = 0
    W = _layer_weights(l, g_mix, w_in, g_out_a, g_out_b, sinks, w_out, g_cross, g_mem, w_xq, w_mem_kv, w_xo,
                       g_ffn, w_up, conv_w, conv_b, w_down)

    xs = x_sample.reshape(N, D)
    qa, ka, va, qb, kb, vb = _sample_inproj(xs, W, cos_s, sin_s)
    sample_a = (qa, ka, va, _time_minor(cache_a_k[l]), _time_minor(cache_a_v[l]))
    yp, (p_ak, p_av, p_bk, p_bv, p_mk, p_mv, p_conv), (oa_s, s_akt, s_avt) = _prompt_layer(
        x_prompt, mem_prompt, W, gf, cos_p, sin_p, sample_a)
    ys, (s_bkt, s_bvt, s_conv) = _sample_layer(
        xs, oa_s, qb, kb, vb, _time_minor(cache_b_k[l]), _time_minor(cache_b_v[l]),
        _mem_tile_rows(cache_mem_k[l]), _mem_tile_rows(cache_mem_v[l]), state_conv[l], W, gf)

    la, lb = p_ak.shape[1], p_bk.shape[1]
    return (yp, ys.reshape(N, 1, D),
            p_ak.reshape(1, B, la, A_HEADS, HEAD_DIM), p_av.reshape(1, B, la, A_HEADS, HEAD_DIM),
            p_bk.reshape(1, B, lb, B_KV_HEADS, HEAD_DIM), p_bv.reshape(1, B, lb, B_KV_HEADS, HEAD_DIM),
            p_mk.reshape(1, B, -1, X_HEADS, D // X_HEADS), p_mv.reshape(1, B, -1, X_HEADS, D // X_HEADS),
            p_conv[None],
            _time_major(s_akt, A_HEADS)[None], _time_major(s_avt, A_HEADS)[None],
            _time_major(s_bkt, B_KV_HEADS)[None], _time_major(s_bvt, B_KV_HEADS)[None],
            s_conv[None])
```

```python
import functools

import jax
import jax.numpy as jnp
import numpy as np
from jax import lax
from jax.experimental import pallas as pl
from jax.experimental.pallas import tpu as pltpu

F32 = jnp.float32
BF16 = jnp.bfloat16

HEAD_DIM = 64
A_HEADS = 8
B_HEADS = 8
B_KV_HEADS = 2
B_GROUP = B_HEADS // B_KV_HEADS
A_W = A_HEADS * HEAD_DIM
B_W = B_HEADS * HEAD_DIM
B_KV_W = B_KV_HEADS * HEAD_DIM
DILATIONS = (1, 4, 16)
A_WINDOW = 2048
B_WINDOW = 128
BLOCK = 128
ROPE_THETA = 10000.0
PAST_LEN = 16384
X_HEADS = 4
CONV_W = 3
EPS = 1e-6
NEG = -1e30
LOG2E = 1.4426950408889634
R4 = 4

V7X_LANES = 128
V7X_MXU_DIM = 256
V7X_VMEM_BYTES = 64 * 1024 * 1024
VMEM_LIMIT = V7X_VMEM_BYTES - 8 * 1024 * 1024


def _params(*sem):
    return pltpu.CompilerParams(dimension_semantics=sem, vmem_limit_bytes=VMEM_LIMIT)


def _rms(x, g):
    return (x * lax.rsqrt(jnp.mean(x * x, axis=-1, keepdims=True) + EPS)) * g


def _dot(a, b):
    return jnp.dot(a, b, preferred_element_type=F32)


def _dot_t(a, b):
    return lax.dot_general(a, b, (((1,), (1,)), ((), ())), preferred_element_type=F32)


def _rope_table_kernel(inv_ref, cos_ref, sin_ref, *, pos0, pos_step):
    rows = cos_ref.shape[0]
    row = lax.broadcasted_iota(jnp.int32, (rows, V7X_LANES), 0)
    lane = lax.broadcasted_iota(jnp.int32, (rows, V7X_LANES), 1)
    ang = (pos0 + pos_step * row).astype(F32) * inv_ref[...]
    first_half = (lane % HEAD_DIM) < (HEAD_DIM // 2)
    cos_ref[...] = jnp.cos(ang)
    sin_ref[...] = jnp.where(first_half, -jnp.sin(ang), jnp.sin(ang))


def _rope_tables(rows, pos0, pos_step):
    half = HEAD_DIM // 2
    inv = jnp.power(ROPE_THETA, -jnp.arange(half, dtype=F32) / half)
    inv = jnp.tile(inv, V7X_LANES // half).reshape(1, V7X_LANES)
    return pl.pallas_call(
        functools.partial(_rope_table_kernel, pos0=pos0, pos_step=pos_step),
        out_shape=[jax.ShapeDtypeStruct((rows, V7X_LANES), F32)] * 2,
        name="rope_tables",
    )(inv)


def _rope(slab, cos, sin, first_half):
    partner = jnp.where(first_half, pltpu.roll(slab, 96, 1), pltpu.roll(slab, 32, 1))
    return slab * cos + partner * sin


def _inproj_kernel(x_ref, g_ref, w_ref, cos_ref, sin_ref,
                   qa_ref, ka_ref, va_ref, qb_ref, kb_ref, vb_ref, *rest,
                   tail_skip, n_tiles, prompt):
    tm = x_ref.shape[0]
    hn = _rms(x_ref[...], g_ref[...]).astype(BF16)
    cos = cos_ref[...]
    sin = sin_ref[...]
    lane = lax.broadcasted_iota(jnp.int32, (tm, V7X_LANES), 1)
    first_half = (lane % HEAD_DIM) < (HEAD_DIM // 2)

    def seg(c0, width):
        return _dot(hn, w_ref[:, c0:c0 + width])

    def roped(z):
        return jnp.concatenate(
            [_rope(z[:, c:c + V7X_LANES], cos, sin, first_half)
             for c in range(0, z.shape[1], V7X_LANES)], axis=1)

    if prompt:
        kat_ref, vat_ref, kbt_ref, vbt_ref, shuf_ref = rest

        def put_a(dst_ref, z):
            for c in range(A_W // V7X_LANES):
                shuf_ref[c] = z[:, c * V7X_LANES:(c + 1) * V7X_LANES]
            for c in range(A_W // V7X_LANES):
                for r in range(R4):
                    dst_ref[r, :, c * V7X_LANES:(c + 1) * V7X_LANES] = shuf_ref[c, pl.ds(r, tm // R4, stride=R4), :]
    else:
        def put_a(dst_ref, z):
            dst_ref[...] = z

    put_a(qa_ref, roped(seg(0, A_W)))
    ka = roped(seg(A_W, A_W))
    put_a(ka_ref, ka)
    va = seg(2 * A_W, A_W)
    put_a(va_ref, va)
    qb_ref[...] = roped(seg(3 * A_W, B_W))
    kvb = seg(3 * A_W + B_W, 2 * B_KV_W)
    kb = roped(kvb[:, :B_KV_W])
    vb = kvb[:, B_KV_W:]
    kb_ref[...] = kb
    vb_ref[...] = vb

    if prompt:
        i = pl.program_id(1)

        @pl.when(i >= tail_skip)
        def _():
            kat_ref[...] = ka
            vat_ref[...] = va

        @pl.when(i == n_tiles - 1)
        def _():
            kbt_ref[...] = kb[tm - B_WINDOW:, :]
            vbt_ref[...] = vb[tm - B_WINDOW:, :]


def _inproj(x, g, w, cos, sin, *, tm, prompt):
    B, S, D = x.shape
    n_tiles = S // tm
    la = min(A_WINDOW, S)
    lb = min(B_WINDOW, S)
    tail_skip = (S - la) // tm
    tok = lambda w_: pl.BlockSpec((None, tm, w_), lambda b, i: (b, i, 0))
    const = lambda shape: pl.BlockSpec(shape, lambda b, i: (0,) * len(shape))
    if prompt:
        a_shape = jax.ShapeDtypeStruct((B, R4, S // R4, A_W), F32)
        a_spec = pl.BlockSpec((None, R4, tm // R4, A_W), lambda b, i: (b, 0, i, 0))
    else:
        a_shape = jax.ShapeDtypeStruct((B, S, A_W), F32)
        a_spec = tok(A_W)
    out_shape = [a_shape] * 3 + [
        jax.ShapeDtypeStruct((B, S, B_W), F32),
        jax.ShapeDtypeStruct((B, S, B_KV_W), F32),
        jax.ShapeDtypeStruct((B, S, B_KV_W), F32)]
    out_specs = [a_spec] * 3 + [tok(B_W), tok(B_KV_W), tok(B_KV_W)]
    scratch = []
    if prompt:
        out_shape += [jax.ShapeDtypeStruct((B, la, A_W), F32)] * 2
        out_shape += [jax.ShapeDtypeStruct((B, lb, B_KV_W), F32)] * 2
        a_tail = pl.BlockSpec((None, tm, A_W), lambda b, i: (b, jnp.maximum(i - tail_skip, 0), 0))
        b_tail = pl.BlockSpec((None, lb, B_KV_W), lambda b, i: (b, 0, 0))
        out_specs += [a_tail, a_tail, b_tail, b_tail]
        scratch = [pltpu.VMEM((A_W // V7X_LANES, tm, V7X_LANES), F32)]
    return pl.pallas_call(
        functools.partial(_inproj_kernel, tail_skip=tail_skip, n_tiles=n_tiles, prompt=prompt),
        grid=(B, n_tiles),
        in_specs=[tok(D), const((1, D)), const(w.shape),
                  pl.BlockSpec((tm, V7X_LANES), lambda b, i: (i, 0)),
                  pl.BlockSpec((tm, V7X_LANES), lambda b, i: (i, 0))],
        out_specs=out_specs,
        out_shape=out_shape,
        scratch_shapes=scratch,
        compiler_params=_params("arbitrary", "arbitrary"),
        name="inproj_rope",
    )(x, g, w, cos, sin)


def _band_bias(chunked):
    a = lax.broadcasted_iota(jnp.int32, (2 * BLOCK, 2 * BLOCK), 0) % BLOCK
    b = lax.broadcasted_iota(jnp.int32, (2 * BLOCK, 2 * BLOCK), 1)
    own = b >= BLOCK
    bb = b % BLOCK
    if chunked:
        sub = BLOCK // R4
        a = R4 * (a % sub) + a // sub
        bb = R4 * (bb % sub) + bb // sub
    dist = BLOCK + a - (bb + jnp.where(own, BLOCK, 0))
    valid = (dist >= 0) & (dist <= BLOCK)
    return jnp.where(valid, 0.0, NEG).astype(F32)


def _two_head_block(q, kk, vv, bias):
    lane = lax.broadcasted_iota(jnp.int32, (BLOCK, V7X_LANES), 1)
    lo = lane < HEAD_DIM
    q = q * (HEAD_DIM ** -0.5 * LOG2E)
    qs = jnp.concatenate([jnp.where(lo, q, 0.0), jnp.where(lo, 0.0, q)], axis=0).astype(BF16)
    s = _dot_t(qs, kk) + bias
    m = jnp.max(s, axis=-1, keepdims=True)
    p = jnp.exp2(s - m)
    v1 = jnp.concatenate([vv, jnp.ones_like(vv)], axis=1)
    pv = _dot(p.astype(BF16), v1)
    acc = jnp.where(lo, pv[:BLOCK, :V7X_LANES], pv[BLOCK:, :V7X_LANES])
    l2 = jnp.where(lo, pv[:BLOCK, V7X_LANES:], pv[BLOCK:, V7X_LANES:])
    m2 = jnp.where(lo, m[:BLOCK], m[BLOCK:])
    return acc, m2, l2


def _get(ref, slices):
    return jnp.concatenate([ref[sl, :] for sl in slices], axis=0) if len(slices) > 1 else ref[slices[0], :]


def _put(ref, slices, val):
    off = 0
    for sl in slices:
        ref[sl, :] = val[off:off + sl.size]
        off += sl.size


def _block_slices(layout, d, S, i):
    if layout == "seq":
        return [pl.ds(i * BLOCK, BLOCK)], ([pl.ds((i - 1) * BLOCK, BLOCK)] if i else None)
    Sr = S // R4
    if d == 1:
        sub = BLOCK // R4
        own = [pl.ds(r * Sr + i * sub, sub) for r in range(R4)]
        prev = [pl.ds(r * Sr + (i - 1) * sub, sub) for r in range(R4)] if i else None
        return own, prev
    if d == R4:
        per_res = Sr // BLOCK
        return [pl.ds(i * BLOCK, BLOCK)], ([pl.ds((i - 1) * BLOCK, BLOCK)] if i % per_res else None)
    step = d // R4
    rd, jb = divmod(i, S // (BLOCK * d))
    start = (rd % R4) * Sr + rd // R4 + jb * (BLOCK * step)
    prev = [pl.ds(start - BLOCK * step, BLOCK, stride=step)] if jb else None
    return [pl.ds(start, BLOCK, stride=step)], prev


def _branch_blocks(q_ref, k_ref, v_ref, bias_ref, layout, d, visit):
    S = q_ref.shape[0]
    bias = bias_ref[1 if (layout == "r4" and d == 1) else 0]
    for i in range(S // BLOCK):
        own, prev = _block_slices(layout, d, S, i)
        if prev is None:
            kk = _get(k_ref, own).astype(BF16)
            vv = _get(v_ref, own).astype(BF16)
            acc, m, l = _two_head_block(_get(q_ref, own), kk, vv, bias[:, BLOCK:])
        else:
            kk = jnp.concatenate([_get(k_ref, prev), _get(k_ref, own)], axis=0).astype(BF16)
            vv = jnp.concatenate([_get(v_ref, prev), _get(v_ref, own)], axis=0).astype(BF16)
            acc, m, l = _two_head_block(_get(q_ref, own), kk, vv, bias)
        visit(own, acc, m, l)


def _init_bias(bias_ref, chunked_too):
    bias_ref[0] = _band_bias(False)
    if chunked_too:
        bias_ref[1] = _band_bias(True)


def _attn_a_kernel(q_ref, k_ref, v_ref, o_ref, m_ref, l_ref, bias_ref):
    _init_bias(bias_ref, True)

    def first_visit(rows, acc, m, l):
        _put(o_ref, rows, acc)
        _put(m_ref, rows, m)
        _put(l_ref, rows, l)

    def merged(rows, acc, m, l):
        m_old = _get(m_ref, rows)
        m_new = jnp.maximum(m_old, m)
        a_old = jnp.exp2(m_old - m_new)
        a_new = jnp.exp2(m - m_new)
        return (a_old * _get(o_ref, rows) + a_new * acc, m_new, a_old * _get(l_ref, rows) + a_new * l)

    def mid_visit(rows, acc, m, l):
        acc, m, l = merged(rows, acc, m, l)
        _put(o_ref, rows, acc)
        _put(m_ref, rows, m)
        _put(l_ref, rows, l)

    def last_visit(rows, acc, m, l):
        acc, m, l = merged(rows, acc, m, l)
        _put(o_ref, rows, acc / l)

    order = (R4, 1, 16)
    assert sorted(order) == sorted(DILATIONS)
    visits = [first_visit] + [mid_visit] * (len(order) - 2) + [last_visit]
    for d, visit in zip(order, visits):
        _branch_blocks(q_ref, k_ref, v_ref, bias_ref, "r4", d, visit)


def _attn_b_kernel(q_ref, k_ref, v_ref, sink_ref, o_ref, bias_ref):
    _init_bias(bias_ref, False)
    sink = sink_ref[0:1, :] * LOG2E

    def visit(rows, acc, m, l):
        _put(o_ref, rows, acc / (l + jnp.exp2(sink - m)))

    _branch_blocks(q_ref, k_ref, v_ref, bias_ref, "seq", 1, visit)


def _attn_a(q, k, v):
    B, S, W = q.shape
    slab = pl.BlockSpec((None, S, V7X_LANES), lambda b, j: (b, 0, j))
    return pl.pallas_call(
        _attn_a_kernel,
        grid=(B, W // V7X_LANES),
        in_specs=[slab, slab, slab],
        out_specs=slab,
        out_shape=jax.ShapeDtypeStruct((B, S, W), F32),
        scratch_shapes=[pltpu.VMEM((S, V7X_LANES), F32), pltpu.VMEM((S, V7X_LANES), F32),
                        pltpu.VMEM((2, 2 * BLOCK, 2 * BLOCK), F32)],
        compiler_params=_params("arbitrary", "arbitrary"),
        name="mixer_a_attention",
    )(q, k, v)


def _attn_b(q, k, v, sink_lanes):
    B, S, W = q.shape
    slab = pl.BlockSpec((None, S, V7X_LANES), lambda b, j: (b, 0, j))
    kv = pl.BlockSpec((None, S, V7X_LANES), lambda b, j: (b, 0, 0))
    return pl.pallas_call(
        _attn_b_kernel,
        grid=(B, W // V7X_LANES),
        in_specs=[slab, kv, kv, pl.BlockSpec((None, 8, V7X_LANES), lambda b, j: (j, 0, 0))],
        out_specs=slab,
        out_shape=jax.ShapeDtypeStruct((B, S, W), F32),
        scratch_shapes=[pltpu.VMEM((1, 2 * BLOCK, 2 * BLOCK), F32)],
        compiler_params=_params("arbitrary", "arbitrary"),
        name="mixer_b_attention",
    )(q, k, v, sink_lanes)


def _as_column(row):
    return jnp.broadcast_to(row, (V7X_LANES, row.shape[1])).T


def _roll_in(dst_ref, cache, new_col):
    L = cache.shape[1]
    rolled = pltpu.roll(cache, L - 1, 1)
    lane = lax.broadcasted_iota(jnp.int32, (cache.shape[0], V7X_LANES), 1)
    if L > V7X_LANES:
        dst_ref[:, :L - V7X_LANES] = rolled[:, :L - V7X_LANES]
    dst_ref[:, L - V7X_LANES:] = jnp.where(lane == V7X_LANES - 1, new_col, rolled[:, L - V7X_LANES:])


def _sample_a_unit(q, k_new, v_new, kt, vt, w, okt_ref, ovt_ref):
    R, L = kt.shape
    H = R // HEAD_DIM
    n_br = len(DILATIONS)
    qc = _as_column(q * (HEAD_DIM ** -0.5))
    knc = _as_column(k_new)
    vnc = _as_column(v_new)
    prod = jnp.concatenate([kt[:, j:j + V7X_LANES] * qc for j in range(0, L, V7X_LANES)], axis=1)
    s = jnp.sum(prod.reshape(H, HEAD_DIM, L), axis=1)
    s_new = jnp.sum((qc * knc).reshape(H, HEAD_DIM, V7X_LANES), axis=1)[:, 0:1]
    s = jnp.where(w > 0, s, NEG)
    m = jnp.maximum(jnp.max(s, axis=-1, keepdims=True), s_new)
    p = w * jnp.exp(s - m)
    p_new = n_br * jnp.exp(s_new - m)
    l = jnp.sum(p, axis=-1, keepdims=True) + p_new

    def per_row(a):
        return jnp.broadcast_to(a.reshape(H, 1, a.shape[1]), (H, HEAD_DIM, a.shape[1])).reshape(R, a.shape[1])

    o_col = (jnp.sum(vt * per_row(p), axis=-1, keepdims=True) + per_row(p_new) * vnc[:, 0:1]) / per_row(l)
    _roll_in(okt_ref, kt, knc)
    _roll_in(ovt_ref, vt, vnc)
    return jnp.broadcast_to(o_col, (R, V7X_LANES)).T[0:1, :]


def _branch_multiplicity(L):
    dist = L - np.arange(L)
    w = sum(((dist % d == 0) & (dist <= BLOCK * d)).astype(np.float32) for d in DILATIONS)
    return jnp.asarray(w.reshape(1, L))


CACHE_HALVES = 2


IN_SLOTS = 2
OUT_SLOTS = 2


class _CacheStream:
    def __init__(self, step, n_steps, per_step, unit0, hbm, ring, sem):
        self.step, self.per_step, self.unit0 = step, per_step, unit0
        self.total = n_steps * per_step
        self.n_steps = n_steps
        self.kt_hbm, self.vt_hbm, self.okt_hbm, self.ovt_hbm = hbm
        self.ink, self.inv, self.outk, self.outv = ring
        self.sem = sem
        self.R = self.ink.shape[1]

    def _slot(self, u, depth):
        assert self.per_step % depth == 0
        return u % depth

    def _copies(self, u, shift, inbound):
        depth = IN_SLOTS if inbound else OUT_SLOTS
        assert shift % depth == 0
        g = self.step * self.per_step + u + shift
        unit = self.unit0 + g
        slot = self._slot(u, depth)
        n = unit // CACHE_HALVES
        rows = pl.ds(pl.multiple_of((unit % CACHE_HALVES) * self.R, self.R), self.R)
        if inbound:
            return (pltpu.make_async_copy(self.kt_hbm.at[n, rows, :], self.ink.at[slot], self.sem.at[0, slot]),
                    pltpu.make_async_copy(self.vt_hbm.at[n, rows, :], self.inv.at[slot], self.sem.at[1, slot]))
        return (pltpu.make_async_copy(self.outk.at[slot], self.okt_hbm.at[n, rows, :], self.sem.at[2, slot]),
                pltpu.make_async_copy(self.outv.at[slot], self.ovt_hbm.at[n, rows, :], self.sem.at[3, slot]))

    def _start(self, u, shift, inbound):
        for cp in self._copies(u, shift, inbound):
            cp.start()

    def _wait(self, u, shift, inbound):
        for cp in self._copies(u, shift, inbound):
            cp.wait()

    def prologue(self):
        @pl.when(self.step == 0)
        def _():
            for u in range(IN_SLOTS):
                self._start(u, 0, True)

    def wait_unit(self, u):
        self._wait(u, 0, True)
        if u >= OUT_SLOTS:
            self._wait(u, -OUT_SLOTS, False)
        else:
            @pl.when(self.step > 0)
            def _():
                self._wait(u, -OUT_SLOTS, False)

    def run_unit(self, u, q_ref, kn_ref, vn_ref, w_ref, o_ref):
        g = self.step * self.per_step + u
        islot = self._slot(u, IN_SLOTS)
        oslot = self._slot(u, OUT_SLOTS)
        o_ref[g] = _sample_a_unit(q_ref[g], kn_ref[g], vn_ref[g], self.ink[islot], self.inv[islot], w_ref[...],
                                  self.outk.at[oslot], self.outv.at[oslot])
        self._start(u, 0, False)
        if u + IN_SLOTS < self.per_step:
            self._start(u, IN_SLOTS, True)
        else:
            @pl.when(self.step < self.n_steps - 1)
            def _():
                self._start(u, IN_SLOTS, True)

    def epilogue(self):
        @pl.when(self.step == self.n_steps - 1)
        def _():
            for u in range(self.per_step - OUT_SLOTS, self.per_step):
                self._wait(u, 0, False)


def _stream_specs(n_units, R, L):
    rows = pl.BlockSpec((n_units, 1, R), lambda *_: (0, 0, 0))
    hbm = pl.BlockSpec(memory_space=pl.ANY)
    ring_in = pltpu.VMEM((IN_SLOTS, R, L), F32)
    ring_out = pltpu.VMEM((OUT_SLOTS, R, L), F32)
    in_specs = [rows, rows, rows, pl.BlockSpec((1, L), lambda *_: (0, 0)), hbm, hbm]
    out_specs = [rows, hbm, hbm]
    scratch = [ring_in, ring_in, ring_out, ring_out, pltpu.SemaphoreType.DMA((4, max(IN_SLOTS, OUT_SLOTS)))]
    return in_specs, out_specs, scratch


def _sample_b_kernel(q_ref, kn_ref, vn_ref, kt_ref, vt_ref, sink_ref, o_ref, okt_ref, ovt_ref):
    nb = q_ref.shape[0]
    lane = lax.broadcasted_iota(jnp.int32, (1, V7X_LANES), 1)
    lo = lane < HEAD_DIM
    for n in range(nb):
        q = q_ref[n:n + 1, :] * (HEAD_DIM ** -0.5)
        rows = []
        for j in range(B_W // V7X_LANES):
            pair = q[:, j * V7X_LANES:(j + 1) * V7X_LANES]
            rows += [jnp.where(lo, pair, 0.0), jnp.where(lo, 0.0, pair)]
        qh = jnp.concatenate(rows, axis=0)
        kt = kt_ref[n]
        vt = vt_ref[n]
        s_new = jnp.sum(qh * kn_ref[n:n + 1, :], axis=-1, keepdims=True)
        s = _dot(qh.astype(BF16), kt.astype(BF16))
        m = jnp.maximum(s_new, jnp.max(s, axis=-1, keepdims=True))
        p_new = jnp.exp(s_new - m)
        p = jnp.exp(s - m)
        l = p_new + jnp.sum(p, axis=-1, keepdims=True)
        o = p_new * vn_ref[n:n + 1, :] + _dot_t(p.astype(BF16), vt.astype(BF16))
        o = o / (l + jnp.exp(sink_ref[...] - m))
        o_ref[n:n + 1, :] = jnp.concatenate(
            [jnp.where(lo, o[2 * j:2 * j + 1], o[2 * j + 1:2 * j + 2]) for j in range(B_W // V7X_LANES)], axis=1)
        _roll_in(okt_ref.at[n], kt, _as_column(kn_ref[n:n + 1, :]))
        _roll_in(ovt_ref.at[n], vt, _as_column(vn_ref[n:n + 1, :]))


def _sample_b(q, k_new, v_new, kt, vt, sink_rows, *, nb):
    N, W, L = kt.shape
    assert L == B_WINDOW and W == V7X_LANES, "a cache shorter than the sliding window is unsupported"
    row = lambda w_: pl.BlockSpec((nb, w_), lambda i: (i, 0))
    cache = pl.BlockSpec((nb, W, L), lambda i: (i, 0, 0))
    return pl.pallas_call(
        _sample_b_kernel,
        grid=(N // nb,),
        in_specs=[row(B_W), row(W), row(W), cache, cache, pl.BlockSpec((8, 1), lambda i: (0, 0))],
        out_specs=[row(B_W), cache, cache],
        out_shape=[jax.ShapeDtypeStruct((N, B_W), F32), jax.ShapeDtypeStruct((N, W, L), F32),
                   jax.ShapeDtypeStruct((N, W, L), F32)],
        compiler_params=_params("arbitrary"),
        name="sample_mixer_b",
    )(q, k_new, v_new, kt, vt, sink_rows)


def _mixers_out(oa, ob, x, ga_ref, gb_ref, w_ref):
    ha = _rms(oa, ga_ref[...]).astype(BF16)
    hb = _rms(ob, gb_ref[...]).astype(BF16)
    return x + _dot(ha, w_ref[:A_W, :]) + _dot(hb, w_ref[A_W:, :])


def _outproj_kernel(oa_ref, ob_ref, x_ref, ga_ref, gb_ref, w_ref, y_ref):
    y_ref[...] = _mixers_out(oa_ref[...], ob_ref[...], x_ref[...], ga_ref, gb_ref, w_ref)


def _outproj(oa, ob, x, ga, gb, w):
    return pl.pallas_call(
        _outproj_kernel,
        out_shape=jax.ShapeDtypeStruct(x.shape, F32),
        compiler_params=pltpu.CompilerParams(vmem_limit_bytes=VMEM_LIMIT),
        name="mixers_outproj",
    )(oa, ob, x, ga, gb, w)


def _norm_matmul_kernel(x_ref, g_ref, w_ref, *out_refs):
    h = _rms(x_ref[...], g_ref[...]).astype(BF16)
    width = out_refs[0].shape[1]
    for j, o_ref in enumerate(out_refs):
        o_ref[...] = _dot(h, w_ref[:, j * width:(j + 1) * width])


def _norm_matmul(x, g, w, *, n_out, tm):
    T, D = x.shape
    width = w.shape[1] // n_out
    return pl.pallas_call(
        _norm_matmul_kernel,
        grid=(T // tm,),
        in_specs=[pl.BlockSpec((tm, D), lambda i: (i, 0)), pl.BlockSpec((1, D), lambda i: (0, 0)),
                  pl.BlockSpec(w.shape, lambda i: (0, 0))],
        out_specs=[pl.BlockSpec((tm, width), lambda i: (i, 0))] * n_out,
        out_shape=[jax.ShapeDtypeStruct((T, width), F32)] * n_out,
        compiler_params=_params("arbitrary"),
        name="norm_matmul",
    )(x, g, w)


def _matmul_residual_kernel(a_ref, w_ref, r_ref, y_ref):
    y_ref[...] = r_ref[...] + _dot(a_ref[...].astype(BF16), w_ref[...])


def _matmul_residual(a, w, r, *, tm):
    T, K = a.shape
    D = w.shape[1]
    return pl.pallas_call(
        _matmul_residual_kernel,
        grid=(T // tm,),
        in_specs=[pl.BlockSpec((tm, K), lambda i: (i, 0)), pl.BlockSpec(w.shape, lambda i: (0, 0)),
                  pl.BlockSpec((tm, D), lambda i: (i, 0))],
        out_specs=pl.BlockSpec((tm, D), lambda i: (i, 0)),
        out_shape=jax.ShapeDtypeStruct((T, D), F32),
        compiler_params=_params("arbitrary"),
        name="matmul_residual",
    )(a, w, r)


def _outproj_cross_kernel(oa_ref, ob_ref, x_ref, ga_ref, gb_ref, wout_ref, gc_ref, wq_ref, mk_ref, mv_ref, wo_ref,
                          y_ref, shuf_ref, q_ref):
    tq = oa_ref.shape[1]
    for c in range(A_W // V7X_LANES):
        for r in range(R4):
            shuf_ref[c, pl.ds(r, tq, stride=R4), :] = oa_ref[r, :, c * V7X_LANES:(c + 1) * V7X_LANES]
    oa = jnp.concatenate([shuf_ref[c] for c in range(A_W // V7X_LANES)], axis=1)
    y_ref[...] = _mixers_out(oa, ob_ref[...], x_ref[...], ga_ref, gb_ref, wout_ref)
    D = y_ref.shape[1]
    hd = D // X_HEADS
    q_ref[...] = (_dot(_rms(y_ref[...], gc_ref[...]).astype(BF16), wq_ref[...]) * (hd ** -0.5)).astype(BF16)
    for h in range(X_HEADS):
        cols = slice(h * hd, (h + 1) * hd)
        s = _dot_t(q_ref[:, cols], mk_ref[:, cols].astype(BF16))
        p = jnp.exp(s - jnp.max(s, axis=-1, keepdims=True))
        o = _dot(p.astype(BF16), mv_ref[:, cols].astype(BF16)) / jnp.sum(p, axis=-1, keepdims=True)
        y_ref[...] += _dot(o.astype(BF16), wo_ref[cols, :])


def _outproj_cross(oa, ob, x, ga, gb, wout, gc, wq, mk, mv, wo, *, tm):
    B, S, D = x.shape
    M = mk.shape[1]
    tok = lambda w_: pl.BlockSpec((None, tm, w_), lambda b, i: (b, i, 0))
    mem = pl.BlockSpec((None, M, D), lambda b, i: (b, 0, 0))
    const = lambda shape: pl.BlockSpec(shape, lambda b, i: (0,) * len(shape))
    return pl.pallas_call(
        _outproj_cross_kernel,
        grid=(B, S // tm),
        in_specs=[pl.BlockSpec((None, R4, tm // R4, A_W), lambda b, i: (b, 0, i, 0)), tok(B_W), tok(D),
                  const((1, A_W)), const((1, B_W)), const(wout.shape), const((1, D)), const(wq.shape),
                  mem, mem, const(wo.shape)],
        out_specs=tok(D),
        out_shape=jax.ShapeDtypeStruct((B, S, D), F32),
        scratch_shapes=[pltpu.VMEM((A_W // V7X_LANES, tm, V7X_LANES), F32), pltpu.VMEM((tm, D), BF16)],
        compiler_params=_params("arbitrary", "arbitrary"),
        name="outproj_cross_attention",
    )(oa, ob, x, ga, gb, wout, gc, wq, mk, mv, wo)


MEM_TILE_ROWS = 8


def _sample_cross_kernel(q_ref, mk_ref, mv_ref, o_ref):
    nb = q_ref.shape[0]
    M = mk_ref.shape[1] // MEM_TILE_ROWS
    for n in range(nb):
        q = q_ref[n]
        k3 = mk_ref[n].reshape(M, MEM_TILE_ROWS, V7X_LANES)
        v3 = mv_ref[n].reshape(M, MEM_TILE_ROWS, V7X_LANES)
        part = jnp.sum(k3 * q[None], axis=-1, keepdims=True)
        s = part + pltpu.roll(part, X_HEADS, 1)
        p = jnp.exp(s - jnp.max(s, axis=0, keepdims=True))
        o = jnp.sum(v3 * p, axis=0) / jnp.sum(p, axis=0)
        o_ref[n] = o


def _sample_cross(q, mk, mv, *, nb):
    N, R, _ = mk.shape
    mem = pl.BlockSpec((nb, R, V7X_LANES), lambda i: (i, 0, 0))
    row = pl.BlockSpec((nb, MEM_TILE_ROWS, V7X_LANES), lambda i: (i, 0, 0))
    return pl.pallas_call(
        _sample_cross_kernel,
        grid=(N // nb,),
        in_specs=[row, mem, mem],
        out_specs=row,
        out_shape=jax.ShapeDtypeStruct((N, MEM_TILE_ROWS, V7X_LANES), F32),
        compiler_params=_params("arbitrary"),
        name="sample_cross_attention",
    )(q, mk, mv)


def _mem_tile_rows(cache):
    N, M, H, Dh = cache.shape
    halves = Dh // V7X_LANES
    assert H * halves == MEM_TILE_ROWS
    return cache.reshape(N, M, H, halves, V7X_LANES).transpose(0, 1, 3, 2, 4).reshape(N, M * MEM_TILE_ROWS, V7X_LANES)


FF_CHUNKS = ((0, 1024), (1024, 1024), (2048, 768))


def _ffn_chunk(h, wup_ref, wdn_ref, cw_ref, cb_ref, d_ff, c0, cw, prev_rows):
    gate = _dot(h, wup_ref[:, c0:c0 + cw])
    val = _dot(h, wup_ref[:, d_ff + c0:d_ff + c0 + cw])
    g2, g1 = prev_rows(gate)
    conv = cb_ref[:, c0:c0 + cw] + (g2 * cw_ref[0:1, c0:c0 + cw] + g1 * cw_ref[1:2, c0:c0 + cw]
                                     + gate * cw_ref[2:3, c0:c0 + cw])
    act = (conv * (1.0 / (1.0 + jnp.exp(-conv)))) * val
    return gate, _dot(act.astype(BF16), wdn_ref[c0:c0 + cw, :])


FFN_UNITS = 4
FF_CHUNKS_PROMPT = ((0, 768), (768, 768), (1536, 768), (2304, 512))


def _ffn_prompt_kernel(x_ref, g_ref, wup_ref, cw_ref, cb_ref, wdn_ref, gf_ref,
                       sq_ref, skn_ref, svn_ref, sw_ref, kt_hbm, vt_hbm,
                       y_ref, state_ref, so_ref, okt_hbm, ovt_hbm,
                       halo_ref, ink_ref, inv_ref, outk_ref, outv_ref, sem, *, n_tiles, n_steps):
    i = pl.program_id(1)
    step = pl.program_id(0) * n_tiles + i
    tm = x_ref.shape[0]
    d_ff = wdn_ref.shape[0]
    stream = _CacheStream(step, n_steps, FFN_UNITS, 0, (kt_hbm, vt_hbm, okt_hbm, ovt_hbm),
                          (ink_ref, inv_ref, outk_ref, outv_ref), sem)

    @pl.when(i == 0)
    def _():
        halo_ref[...] = jnp.zeros_like(halo_ref)

    stream.prologue()
    x = x_ref[...]
    h = _rms(x, g_ref[...]).astype(BF16)
    y = x
    for u, (c0, cw) in enumerate(FF_CHUNKS_PROMPT):
        stream.wait_unit(u)
        row = lax.broadcasted_iota(jnp.int32, (tm, cw), 0)
        halo = halo_ref[:, c0:c0 + cw]

        def prev_rows(gate):
            g1 = jnp.where(row == 0, halo[1:2], pltpu.roll(gate, 1, 0))
            g2 = jnp.where(row == 0, halo[0:1], jnp.where(row == 1, halo[1:2], pltpu.roll(gate, 2, 0)))
            return g2, g1

        gate, contrib = _ffn_chunk(h, wup_ref, wdn_ref, cw_ref, cb_ref, d_ff, c0, cw, prev_rows)
        y = y + contrib
        halo_ref[0:2, c0:c0 + cw] = gate[tm - 2:, :]
        state_ref[:, c0:c0 + cw] = gate[tm - 2:, :]
        stream.run_unit(u, sq_ref, skn_ref, svn_ref, sw_ref, so_ref)

    y_ref[...] = _rms(y, gf_ref[...])
    stream.epilogue()


def _ffn_prompt(x, g, wup, cw, cb, wdn, gf, sq, skn, svn, kt, vt, *, tm):
    B, S, D = x.shape
    d_ff = wdn.shape[0]
    N, W, L = kt.shape
    assert FF_CHUNKS_PROMPT[-1][0] + FF_CHUNKS_PROMPT[-1][1] == d_ff and len(FF_CHUNKS_PROMPT) == FFN_UNITS
    assert L == A_WINDOW, "a cache shorter than the largest dilated window is unsupported"
    n_tiles = S // tm
    n_steps = B * n_tiles
    n_units = n_steps * FFN_UNITS
    R = W // CACHE_HALVES
    assert sq.shape == (n_units, 1, R) and n_units == N * CACHE_HALVES, "the cache units are spread evenly over the grid"
    tok = pl.BlockSpec((None, tm, D), lambda b, i: (b, i, 0))
    const = lambda shape: pl.BlockSpec(shape, lambda b, i: (0,) * len(shape))
    resident = lambda shape: pl.BlockSpec(shape, lambda b, i: (0,) * len(shape), pipeline_mode=pl.Buffered(1))
    s_in, s_out, s_scratch = _stream_specs(n_units, R, L)
    return pl.pallas_call(
        functools.partial(_ffn_prompt_kernel, n_tiles=n_tiles, n_steps=n_steps),
        grid=(B, n_tiles),
        in_specs=[tok, const((1, D)), resident(wup.shape), const(cw.shape), const(cb.shape),
                  resident(wdn.shape), const((1, D))] + s_in,
        out_specs=[tok, pl.BlockSpec((None, CONV_W - 1, d_ff), lambda b, i: (b, 0, 0))] + s_out,
        out_shape=[jax.ShapeDtypeStruct((B, S, D), F32),
                   jax.ShapeDtypeStruct((B, CONV_W - 1, d_ff), F32),
                   jax.ShapeDtypeStruct((n_units, 1, R), F32),
                   jax.ShapeDtypeStruct((N, W, L), F32), jax.ShapeDtypeStruct((N, W, L), F32)],
        scratch_shapes=[pltpu.VMEM((8, d_ff), F32)] + s_scratch,
        compiler_params=_params("arbitrary", "arbitrary"),
        name="conv_ffn_prompt_sample_a",
    )(x, g, wup, cw, cb, wdn, gf, sq, skn, svn, _branch_multiplicity(L), kt, vt)


def _ffn_sample_kernel(x_ref, s0_ref, s1_ref, g_ref, wup_ref, cw_ref, cb_ref, wdn_ref, gf_ref,
                       y_ref, gate_ref):
    d_ff = wdn_ref.shape[0]
    x = x_ref[...]
    h = _rms(x, g_ref[...]).astype(BF16)
    y = x
    for c0, cw in FF_CHUNKS:
        prev_rows = lambda gate: (s0_ref[:, c0:c0 + cw], s1_ref[:, c0:c0 + cw])
        gate, contrib = _ffn_chunk(h, wup_ref, wdn_ref, cw_ref, cb_ref, d_ff, c0, cw, prev_rows)
        y = y + contrib
        gate_ref[:, c0:c0 + cw] = gate
    y_ref[...] = _rms(y, gf_ref[...])


def _ffn_sample(x, s0, s1, g, wup, cw, cb, wdn, gf):
    N, D = x.shape
    d_ff = wdn.shape[0]
    return pl.pallas_call(
        _ffn_sample_kernel,
        out_shape=[jax.ShapeDtypeStruct((N, D), F32), jax.ShapeDtypeStruct((N, d_ff), F32)],
        compiler_params=pltpu.CompilerParams(vmem_limit_bytes=VMEM_LIMIT),
        name="conv_ffn_sample",
    )(x, s0, s1, g, wup, cw, cb, wdn, gf)


def _qb_pair_perm():
    cols = []
    for j in range(B_GROUP):
        for hk in range(B_KV_HEADS):
            h = hk * B_GROUP + j
            cols.extend(range(h * HEAD_DIM, (h + 1) * HEAD_DIM))
    return np.asarray(cols, dtype=np.int32)


def _layer_weights(l, g_mix, w_in, g_out_a, g_out_b, sinks, w_out, g_cross, g_mem, w_xq, w_mem_kv, w_xo,
                   g_ffn, w_up, conv_w, conv_b, w_down):
    perm = _qb_pair_perm()
    qb0 = 3 * A_W
    w_in_l = w_in[l]
    w_in_l = jnp.concatenate([w_in_l[:, :qb0], w_in_l[:, qb0:qb0 + B_W][:, perm], w_in_l[:, qb0 + B_W:]], axis=1)
    w_out_l = w_out[l]
    w_out_l = jnp.concatenate([w_out_l[:A_W], w_out_l[A_W:][perm]], axis=0)
    sink = sinks[l].astype(F32)
    pair = jnp.stack([sink[:B_GROUP], sink[B_GROUP:]], axis=1)
    sink_lanes = jnp.broadcast_to(jnp.repeat(pair, HEAD_DIM, axis=1)[:, None, :], (B_GROUP, 8, V7X_LANES))
    sink_rows = pair.reshape(2 * B_GROUP, 1)
    row = lambda v: v.reshape(1, -1).astype(F32)
    return dict(
        g_mix=row(g_mix[l]), w_in=w_in_l.astype(BF16),
        g_out_a=row(g_out_a[l]), g_out_b=row(g_out_b[l][perm]), w_out=w_out_l.astype(BF16),
        sink_lanes=sink_lanes, sink_rows=sink_rows,
        g_cross=row(g_cross[l]), g_mem=row(g_mem[l]), w_xq=w_xq[l].astype(BF16),
        w_mem_kv=w_mem_kv[l].astype(BF16), w_xo=w_xo[l].astype(BF16),
        g_ffn=row(g_ffn[l]), w_up=w_up[l].astype(BF16), conv_w=conv_w[l].astype(F32),
        conv_b=row(conv_b[l]), w_down=w_down[l].astype(BF16))


PROMPT_TM = 512
PROMPT_TM_WIDE = 1024
SAMPLE_NB = 8


def _prompt_layer(x, mem, W, g_final, cos, sin, sample_a):
    B, S, D = x.shape
    M = mem.shape[1]
    tm = min(PROMPT_TM, S)
    tm_wide = min(PROMPT_TM_WIDE, S)
    qa, ka, va, qb, kb, vb, ka_t, va_t, kb_t, vb_t = _inproj(
        x, W["g_mix"], W["w_in"], cos, sin, tm=tm_wide, prompt=True)
    flat = lambda t: t.reshape(B, S, A_W)
    oa = _attn_a(flat(qa), flat(ka), flat(va)).reshape(B, R4, S // R4, A_W)
    ob = _attn_b(qb, kb, vb, W["sink_lanes"])
    mk, mv = _norm_matmul(mem.reshape(B * M, D), W["g_mem"], W["w_mem_kv"], n_out=2, tm=min(512, B * M))
    mk = mk.reshape(B, M, D)
    mv = mv.reshape(B, M, D)
    x2 = _outproj_cross(oa, ob, x, W["g_out_a"], W["g_out_b"], W["w_out"], W["g_cross"], W["w_xq"], mk, mv,
                        W["w_xo"], tm=tm_wide)
    sq, skn, svn, kt, vt = sample_a
    N, W_, L = kt.shape
    unit_rows = lambda t: t.reshape(N * CACHE_HALVES, 1, W_ // CACHE_HALVES)
    y, conv_state, o_units, s_akt, s_avt = _ffn_prompt(
        x2, W["g_ffn"], W["w_up"], W["conv_w"], W["conv_b"], W["w_down"], g_final,
        unit_rows(sq), unit_rows(skn), unit_rows(svn), kt, vt, tm=tm)
    return y, (ka_t, va_t, kb_t, vb_t, mk, mv, conv_state), (o_units.reshape(N, W_), s_akt, s_avt)


def _sample_inproj(x, W, cos, sin):
    N = x.shape[0]
    return [t[0] for t in _inproj(x[None], W["g_mix"], W["w_in"], cos, sin, tm=N, prompt=False)]


def _sample_layer(x, oa, qb, kb, vb, b_kt, b_vt, mem_k, mem_v, conv_state, W, g_final):
    N, D = x.shape
    ob, s_bkt, s_bvt = _sample_b(qb, kb, vb, b_kt, b_vt, W["sink_rows"], nb=SAMPLE_NB)
    x1 = _outproj(oa, ob, x, W["g_out_a"], W["g_out_b"], W["w_out"])
    (q,) = _norm_matmul(x1, W["g_cross"], W["w_xq"], n_out=1, tm=N)
    hd = D // X_HEADS
    halves = hd // V7X_LANES
    q = (q * (hd ** -0.5)).reshape(N, X_HEADS, halves, V7X_LANES).transpose(0, 2, 1, 3)
    o = _sample_cross(q.reshape(N, MEM_TILE_ROWS, V7X_LANES), mem_k, mem_v, nb=SAMPLE_NB)
    o = o.reshape(N, halves, X_HEADS, V7X_LANES).transpose(0, 2, 1, 3).reshape(N, D)
    x2 = _matmul_residual(o, W["w_xo"], x1, tm=N)
    y, gate = _ffn_sample(x2, conv_state[:, 0], conv_state[:, 1], W["g_ffn"], W["w_up"], W["conv_w"],
                          W["conv_b"], W["w_down"], g_final)
    new_conv = jnp.stack([conv_state[:, 1], gate], axis=1)
    return y, (s_bkt, s_bvt, new_conv)


def _time_minor(cache):
    N, L, H, Dh = cache.shape
    return cache.transpose(0, 2, 3, 1).reshape(N, H * Dh, L)


def _time_major(cache_t, H):
    N, W, L = cache_t.shape
    return cache_t.reshape(N, H, W // H, L).transpose(0, 3, 1, 2)


def kernel(x_prompt, x_sample, cache_a_k, cache_a_v, cache_b_k, cache_b_v, cache_mem_k, cache_mem_v, state_conv,
           mem_prompt, g_mix, w_in, g_out_a, g_out_b, sinks, w_out, g_cross, g_mem, w_xq, w_mem_kv, w_xo,
           g_ffn, w_up, conv_w, conv_b, w_down, g_final):
    depth = w_in.shape[0]
    assert depth == 1, "layer stacking is not wired up: the problem has a single layer"
    B, S, D = x_prompt.shape
    N, T, _ = x_sample.shape
    assert T == 1, "the sample group decodes one token per sequence"
    gf = g_final.reshape(1, D).astype(F32)
    cos_p, sin_p = _rope_tables(S, 0, 1)
    cos_s, sin_s = _rope_tables(N, PAST_LEN, 0)

    l = 0
    W = _layer_weights(l, g_mix, w_in, g_out_a, g_out_b, sinks, w_out, g_cross, g_mem, w_xq, w_mem_kv, w_xo,
                       g_ffn, w_up, conv_w, conv_b, w_down)

    xs = x_sample.reshape(N, D)
    qa, ka, va, qb, kb, vb = _sample_inproj(xs, W, cos_s, sin_s)
    sample_a = (qa, ka, va, _time_minor(cache_a_k[l]), _time_minor(cache_a_v[l]))
    yp, (p_ak, p_av, p_bk, p_bv, p_mk, p_mv, p_conv), (oa_s, s_akt, s_avt) = _prompt_layer(
        x_prompt, mem_prompt, W, gf, cos_p, sin_p, sample_a)
    ys, (s_bkt, s_bvt, s_conv) = _sample_layer(
        xs, oa_s, qb, kb, vb, _time_minor(cache_b_k[l]), _time_minor(cache_b_v[l]),
        _mem_tile_rows(cache_mem_k[l]), _mem_tile_rows(cache_mem_v[l]), state_conv[l], W, gf)

    la, lb = p_ak.shape[1], p_bk.shape[1]
    return (yp, ys.reshape(N, 1, D),
            p_ak.reshape(1, B, la, A_HEADS, HEAD_DIM), p_av.reshape(1, B, la, A_HEADS, HEAD_DIM),
            p_bk.reshape(1, B, lb, B_KV_HEADS, HEAD_DIM), p_bv.reshape(1, B, lb, B_KV_HEADS, HEAD_DIM),
            p_mk.reshape(1, B, -1, X_HEADS, D // X_HEADS), p_mv.reshape(1, B, -1, X_HEADS, D // X_HEADS),
            p_conv[None],
            _time_major(s_akt, A_HEADS)[None], _time_major(s_avt, A_HEADS)[None],
            _time_major(s_bkt, B_KV_HEADS)[None], _time_major(s_bvt, B_KV_HEADS)[None],
            s_conv[None])
```

```python
import functools

import jax
import jax.numpy as jnp
import numpy as np
from jax import lax
from jax.experimental import pallas as pl
from jax.experimental.pallas import tpu as pltpu

F32 = jnp.float32
BF16 = jnp.bfloat16

HEAD_DIM = 64
A_HEADS = 8
B_HEADS = 8
B_KV_HEADS = 2
B_GROUP = B_HEADS // B_KV_HEADS
A_W = A_HEADS * HEAD_DIM
B_W = B_HEADS * HEAD_DIM
B_KV_W = B_KV_HEADS * HEAD_DIM
DILATIONS = (1, 4, 16)
A_WINDOW = 2048
B_WINDOW = 128
BLOCK = 128
ROPE_THETA = 10000.0
PAST_LEN = 16384
X_HEADS = 4
CONV_W = 3
EPS = 1e-6
NEG = -1e30
LOG2E = 1.4426950408889634
ATTN_UNROLL = 32
R4 = 4

V7X_LANES = 128
V7X_MXU_DIM = 256
V7X_VMEM_BYTES = 64 * 1024 * 1024
VMEM_LIMIT = V7X_VMEM_BYTES - 8 * 1024 * 1024


def _params(*sem):
    return pltpu.CompilerParams(dimension_semantics=sem, vmem_limit_bytes=VMEM_LIMIT)


def _rms(x, g):
    return (x * lax.rsqrt(jnp.mean(x * x, axis=-1, keepdims=True) + EPS)) * g


def _dot(a, b):
    return jnp.dot(a, b, preferred_element_type=F32)


def _dot_t(a, b):
    return lax.dot_general(a, b, (((1,), (1,)), ((), ())), preferred_element_type=F32)


def _rope_table_kernel(inv_ref, cos_ref, sin_ref, *, pos0, pos_step):
    rows = cos_ref.shape[0]
    row = lax.broadcasted_iota(jnp.int32, (rows, V7X_LANES), 0)
    lane = lax.broadcasted_iota(jnp.int32, (rows, V7X_LANES), 1)
    ang = (pos0 + pos_step * row).astype(F32) * inv_ref[...]
    first_half = (lane % HEAD_DIM) < (HEAD_DIM // 2)
    cos_ref[...] = jnp.cos(ang)
    sin_ref[...] = jnp.where(first_half, -jnp.sin(ang), jnp.sin(ang))


def _rope_tables(rows, pos0, pos_step):
    half = HEAD_DIM // 2
    inv = jnp.power(ROPE_THETA, -jnp.arange(half, dtype=F32) / half)
    inv = jnp.tile(inv, V7X_LANES // half).reshape(1, V7X_LANES)
    return pl.pallas_call(
        functools.partial(_rope_table_kernel, pos0=pos0, pos_step=pos_step),
        out_shape=[jax.ShapeDtypeStruct((rows, V7X_LANES), F32)] * 2,
        name="rope_tables",
    )(inv)


def _rope(slab, cos, sin, first_half):
    partner = jnp.where(first_half, pltpu.roll(slab, 96, 1), pltpu.roll(slab, 32, 1))
    return slab * cos + partner * sin


def _inproj_kernel(x_ref, g_ref, w_ref, cos_ref, sin_ref,
                   qa_ref, ka_ref, va_ref, qb_ref, kb_ref, vb_ref, *rest,
                   tail_skip, n_tiles, prompt):
    tm = x_ref.shape[0]
    hn = _rms(x_ref[...], g_ref[...]).astype(BF16)
    cos = cos_ref[...]
    sin = sin_ref[...]
    lane = lax.broadcasted_iota(jnp.int32, (tm, V7X_LANES), 1)
    first_half = (lane % HEAD_DIM) < (HEAD_DIM // 2)

    def seg(c0, width):
        return _dot(hn, w_ref[:, c0:c0 + width])

    def roped(z):
        return jnp.concatenate(
            [_rope(z[:, c:c + V7X_LANES], cos, sin, first_half)
             for c in range(0, z.shape[1], V7X_LANES)], axis=1)

    if prompt:
        kat_ref, vat_ref, kbt_ref, vbt_ref, shuf_ref = rest

        def put_a(dst_ref, z):
            for c in range(A_W // V7X_LANES):
                shuf_ref[c] = z[:, c * V7X_LANES:(c + 1) * V7X_LANES]
            for c in range(A_W // V7X_LANES):
                for r in range(R4):
                    dst_ref[r, :, c * V7X_LANES:(c + 1) * V7X_LANES] = shuf_ref[c, pl.ds(r, tm // R4, stride=R4), :]
    else:
        def put_a(dst_ref, z):
            dst_ref[...] = z

    put_a(qa_ref, roped(seg(0, A_W)))
    ka = roped(seg(A_W, A_W))
    put_a(ka_ref, ka)
    va = seg(2 * A_W, A_W)
    put_a(va_ref, va)
    qb = roped(seg(3 * A_W, B_W))
    if prompt:
        qb = qb * (HEAD_DIM ** -0.5 * LOG2E)
    qb_ref[...] = qb.astype(qb_ref.dtype)
    kvb = seg(3 * A_W + B_W, 2 * B_KV_W)
    kb = roped(kvb[:, :B_KV_W])
    vb = kvb[:, B_KV_W:]
    kb_ref[...] = kb.astype(kb_ref.dtype)
    vb_ref[...] = vb.astype(vb_ref.dtype)

    if prompt:
        i = pl.program_id(1)

        @pl.when(i >= tail_skip)
        def _():
            kat_ref[...] = ka
            vat_ref[...] = va

        @pl.when(i == n_tiles - 1)
        def _():
            kbt_ref[...] = kb[tm - B_WINDOW:, :]
            vbt_ref[...] = vb[tm - B_WINDOW:, :]


def _inproj(x, g, w, cos, sin, *, tm, prompt):
    B, S, D = x.shape
    n_tiles = S // tm
    la = min(A_WINDOW, S)
    lb = min(B_WINDOW, S)
    tail_skip = (S - la) // tm
    tok = lambda w_: pl.BlockSpec((None, tm, w_), lambda b, i: (b, i, 0))
    const = lambda shape: pl.BlockSpec(shape, lambda b, i: (0,) * len(shape))
    if prompt:
        a_shape = jax.ShapeDtypeStruct((B, R4, S // R4, A_W), F32)
        a_spec = pl.BlockSpec((None, R4, tm // R4, A_W), lambda b, i: (b, 0, i, 0))
    else:
        a_shape = jax.ShapeDtypeStruct((B, S, A_W), F32)
        a_spec = tok(A_W)
    b_dtype = BF16 if prompt else F32
    out_shape = [a_shape] * 3 + [
        jax.ShapeDtypeStruct((B, S, B_W), b_dtype),
        jax.ShapeDtypeStruct((B, S, B_KV_W), b_dtype),
        jax.ShapeDtypeStruct((B, S, B_KV_W), b_dtype)]
    out_specs = [a_spec] * 3 + [tok(B_W), tok(B_KV_W), tok(B_KV_W)]
    scratch = []
    if prompt:
        out_shape += [jax.ShapeDtypeStruct((B, la, A_W), F32)] * 2
        out_shape += [jax.ShapeDtypeStruct((B, lb, B_KV_W), F32)] * 2
        a_tail = pl.BlockSpec((None, tm, A_W), lambda b, i: (b, jnp.maximum(i - tail_skip, 0), 0))
        b_tail = pl.BlockSpec((None, lb, B_KV_W), lambda b, i: (b, 0, 0))
        out_specs += [a_tail, a_tail, b_tail, b_tail]
        scratch = [pltpu.VMEM((A_W // V7X_LANES, tm, V7X_LANES), F32)]
    return pl.pallas_call(
        functools.partial(_inproj_kernel, tail_skip=tail_skip, n_tiles=n_tiles, prompt=prompt),
        grid=(B, n_tiles),
        in_specs=[tok(D), const((1, D)), const(w.shape),
                  pl.BlockSpec((tm, V7X_LANES), lambda b, i: (i, 0)),
                  pl.BlockSpec((tm, V7X_LANES), lambda b, i: (i, 0))],
        out_specs=out_specs,
        out_shape=out_shape,
        scratch_shapes=scratch,
        compiler_params=_params("arbitrary", "arbitrary"),
        name="inproj_rope",
    )(x, g, w, cos, sin)


def _band_bias(first, chunked):
    a = lax.broadcasted_iota(jnp.int32, (2 * BLOCK, 2 * BLOCK), 0) % BLOCK
    b = lax.broadcasted_iota(jnp.int32, (2 * BLOCK, 2 * BLOCK), 1)
    own = b >= BLOCK
    bb = b % BLOCK
    if chunked:
        sub = BLOCK // R4
        a = R4 * (a % sub) + a // sub
        bb = R4 * (bb % sub) + bb // sub
    dist = BLOCK + a - (bb + jnp.where(own, BLOCK, 0))
    valid = (dist >= 0) & (dist <= BLOCK)
    if first:
        valid = valid & own
    return jnp.where(valid, 0.0, NEG).astype(F32)


def _two_head_block(q, kk, vv, bias):
    lane = lax.broadcasted_iota(jnp.int32, (BLOCK, V7X_LANES), 1)
    lo = lane < HEAD_DIM
    if q.dtype == BF16:
        zero = jnp.zeros_like(q)
        qs = jnp.concatenate([jnp.where(lo, q, zero), jnp.where(lo, zero, q)], axis=0)
    else:
        q = q * (HEAD_DIM ** -0.5 * LOG2E)
        qs = jnp.concatenate([jnp.where(lo, q, 0.0), jnp.where(lo, 0.0, q)], axis=0).astype(BF16)
    s = _dot_t(qs, kk) + bias
    m = jnp.max(s, axis=-1, keepdims=True)
    p = jnp.exp2(s - m)
    v1 = jnp.concatenate([vv, jnp.ones_like(vv)], axis=1)
    pv = _dot(p.astype(BF16), v1)
    acc = jnp.where(lo, pv[:BLOCK, :V7X_LANES], pv[BLOCK:, :V7X_LANES])
    l2 = jnp.where(lo, pv[:BLOCK, V7X_LANES:], pv[BLOCK:, V7X_LANES:])
    m2 = jnp.where(lo, m[:BLOCK], m[BLOCK:])
    return acc, m2, l2


def _get(ref, slices):
    return jnp.concatenate([ref[sl, :] for sl in slices], axis=0) if len(slices) > 1 else ref[slices[0], :]


def _put(ref, slices, val):
    off = 0
    for sl in slices:
        ref[sl, :] = val[off:off + sl.size]
        off += sl.size


def _block_slices(layout, d, S, i):
    if layout == "seq":
        start = pl.multiple_of(i * BLOCK, BLOCK)
        prev = pl.multiple_of(jnp.maximum(i - 1, 0) * BLOCK, BLOCK)
        return [pl.ds(start, BLOCK)], [pl.ds(prev, BLOCK)], i == 0
    Sr = S // R4
    if d == 1:
        sub = BLOCK // R4
        pj = jnp.maximum(i - 1, 0)
        own = [pl.ds(pl.multiple_of(r * Sr + i * sub, sub), sub) for r in range(R4)]
        prev = [pl.ds(pl.multiple_of(r * Sr + pj * sub, sub), sub) for r in range(R4)]
        return own, prev, i == 0
    if d == R4:
        per_res = Sr // BLOCK
        jb = i % per_res
        start = pl.multiple_of(i * BLOCK, BLOCK)
        prev = pl.multiple_of(jnp.where(jb == 0, i, i - 1) * BLOCK, BLOCK)
        return [pl.ds(start, BLOCK)], [pl.ds(prev, BLOCK)], jb == 0
    step = d // R4
    per_res = S // (BLOCK * d)
    rd = i // per_res
    jb = i % per_res
    start = (rd % R4) * Sr + rd // R4 + jb * (BLOCK * step)
    prev = jnp.where(jb == 0, start, start - BLOCK * step)
    return [pl.ds(start, BLOCK, stride=step)], [pl.ds(prev, BLOCK, stride=step)], jb == 0


def _branch_blocks(q_ref, k_ref, v_ref, bias_ref, layout, d, visit):
    S = q_ref.shape[0]
    bias_base = 2 if (layout == "r4" and d == 1) else 0

    def body(i, carry):
        own, prev, first = _block_slices(layout, d, S, i)
        kk = jnp.concatenate([_get(k_ref, prev), _get(k_ref, own)], axis=0).astype(BF16)
        vv = jnp.concatenate([_get(v_ref, prev), _get(v_ref, own)], axis=0).astype(BF16)
        bias = bias_ref[bias_base + jnp.where(first, 1, 0)]
        acc, m, l = _two_head_block(_get(q_ref, own), kk, vv, bias)
        visit(own, acc, m, l)
        return carry

    lax.fori_loop(0, S // BLOCK, body, 0, unroll=ATTN_UNROLL)


def _init_bias(bias_ref, chunked_too):
    bias_ref[0] = _band_bias(False, False)
    bias_ref[1] = _band_bias(True, False)
    if chunked_too:
        bias_ref[2] = _band_bias(False, True)
        bias_ref[3] = _band_bias(True, True)


def _attn_a_kernel(q_ref, k_ref, v_ref, o_ref, m_ref, l_ref, bias_ref):
    _init_bias(bias_ref, True)

    def first_visit(rows, acc, m, l):
        _put(o_ref, rows, acc)
        _put(m_ref, rows, m)
        _put(l_ref, rows, l)

    def merged(rows, acc, m, l):
        m_old = _get(m_ref, rows)
        m_new = jnp.maximum(m_old, m)
        a_old = jnp.exp2(m_old - m_new)
        a_new = jnp.exp2(m - m_new)
        return (a_old * _get(o_ref, rows) + a_new * acc, m_new, a_old * _get(l_ref, rows) + a_new * l)

    def mid_visit(rows, acc, m, l):
        acc, m, l = merged(rows, acc, m, l)
        _put(o_ref, rows, acc)
        _put(m_ref, rows, m)
        _put(l_ref, rows, l)

    def last_visit(rows, acc, m, l):
        acc, m, l = merged(rows, acc, m, l)
        _put(o_ref, rows, acc / l)

    order = (R4, 1, 16)
    assert sorted(order) == sorted(DILATIONS)
    visits = [first_visit] + [mid_visit] * (len(order) - 2) + [last_visit]
    for d, visit in zip(order, visits):
        _branch_blocks(q_ref, k_ref, v_ref, bias_ref, "r4", d, visit)


def _attn_b_kernel(q_ref, k_ref, v_ref, sink_ref, o_ref, bias_ref):
    _init_bias(bias_ref, False)
    sink = sink_ref[0:1, :] * LOG2E

    def visit(rows, acc, m, l):
        _put(o_ref, rows, acc / (l + jnp.exp2(sink - m)))

    _branch_blocks(q_ref, k_ref, v_ref, bias_ref, "seq", 1, visit)


def _attn_a(q, k, v):
    B, S, W = q.shape
    slab = pl.BlockSpec((None, S, V7X_LANES), lambda b, j: (b, 0, j))
    return pl.pallas_call(
        _attn_a_kernel,
        grid=(B, W // V7X_LANES),
        in_specs=[slab, slab, slab],
        out_specs=slab,
        out_shape=jax.ShapeDtypeStruct((B, S, W), F32),
        scratch_shapes=[pltpu.VMEM((S, V7X_LANES), F32), pltpu.VMEM((S, V7X_LANES), F32),
                        pltpu.VMEM((4, 2 * BLOCK, 2 * BLOCK), F32)],
        compiler_params=_params("arbitrary", "arbitrary"),
        name="mixer_a_attention",
    )(q, k, v)


def _attn_b(q, k, v, sink_lanes):
    B, S, W = q.shape
    slab = pl.BlockSpec((None, S, V7X_LANES), lambda b, j: (b, 0, j))
    kv = pl.BlockSpec((None, S, V7X_LANES), lambda b, j: (b, 0, 0))
    return pl.pallas_call(
        _attn_b_kernel,
        grid=(B, W // V7X_LANES),
        in_specs=[slab, kv, kv, pl.BlockSpec((None, 8, V7X_LANES), lambda b, j: (j, 0, 0))],
        out_specs=slab,
        out_shape=jax.ShapeDtypeStruct((B, S, W), F32),
        scratch_shapes=[pltpu.VMEM((2, 2 * BLOCK, 2 * BLOCK), F32)],
        compiler_params=_params("arbitrary", "arbitrary"),
        name="mixer_b_attention",
    )(q, k, v, sink_lanes)


def _as_column(row):
    return jnp.broadcast_to(row, (V7X_LANES, row.shape[1])).T


def _roll_in(dst_ref, cache, new_col):
    L = cache.shape[1]
    rolled = pltpu.roll(cache, L - 1, 1)
    lane = lax.broadcasted_iota(jnp.int32, (cache.shape[0], V7X_LANES), 1)
    if L > V7X_LANES:
        dst_ref[:, :L - V7X_LANES] = rolled[:, :L - V7X_LANES]
    dst_ref[:, L - V7X_LANES:] = jnp.where(lane == V7X_LANES - 1, new_col, rolled[:, L - V7X_LANES:])


def _sample_a_unit(q, k_new, v_new, kt, vt, w, okt_ref, ovt_ref):
    R, L = kt.shape
    H = R // HEAD_DIM
    n_br = len(DILATIONS)
    qc = _as_column(q * (HEAD_DIM ** -0.5))
    knc = _as_column(k_new)
    vnc = _as_column(v_new)
    prod = jnp.concatenate([kt[:, j:j + V7X_LANES] * qc for j in range(0, L, V7X_LANES)], axis=1)
    s = jnp.sum(prod.reshape(H, HEAD_DIM, L), axis=1)
    s_new = jnp.sum((qc * knc).reshape(H, HEAD_DIM, V7X_LANES), axis=1)[:, 0:1]
    s = jnp.where(w > 0, s, NEG)
    m = jnp.maximum(jnp.max(s, axis=-1, keepdims=True), s_new)
    p = w * jnp.exp(s - m)
    p_new = n_br * jnp.exp(s_new - m)
    l = jnp.sum(p, axis=-1, keepdims=True) + p_new

    def per_row(a):
        return jnp.broadcast_to(a.reshape(H, 1, a.shape[1]), (H, HEAD_DIM, a.shape[1])).reshape(R, a.shape[1])

    o_col = (jnp.sum(vt * per_row(p), axis=-1, keepdims=True) + per_row(p_new) * vnc[:, 0:1]) / per_row(l)
    _roll_in(okt_ref, kt, knc)
    _roll_in(ovt_ref, vt, vnc)
    return jnp.broadcast_to(o_col, (R, V7X_LANES)).T[0:1, :]


def _branch_multiplicity(L):
    dist = L - np.arange(L)
    w = sum(((dist % d == 0) & (dist <= BLOCK * d)).astype(np.float32) for d in DILATIONS)
    return jnp.asarray(w.reshape(1, L))


CACHE_HALVES = 2


IN_SLOTS = 2
OUT_SLOTS = 2


class _CacheStream:
    def __init__(self, step, n_steps, per_step, unit0, hbm, ring, sem):
        self.step, self.per_step, self.unit0 = step, per_step, unit0
        self.total = n_steps * per_step
        self.n_steps = n_steps
        self.kt_hbm, self.vt_hbm, self.okt_hbm, self.ovt_hbm = hbm
        self.ink, self.inv, self.outk, self.outv = ring
        self.sem = sem
        self.R = self.ink.shape[1]

    def _slot(self, u, depth):
        assert self.per_step % depth == 0
        return u % depth

    def _copies(self, u, shift, inbound):
        depth = IN_SLOTS if inbound else OUT_SLOTS
        assert shift % depth == 0
        g = self.step * self.per_step + u + shift
        unit = self.unit0 + g
        slot = self._slot(u, depth)
        n = unit // CACHE_HALVES
        rows = pl.ds(pl.multiple_of((unit % CACHE_HALVES) * self.R, self.R), self.R)
        if inbound:
            return (pltpu.make_async_copy(self.kt_hbm.at[n, rows, :], self.ink.at[slot], self.sem.at[0, slot]),
                    pltpu.make_async_copy(self.vt_hbm.at[n, rows, :], self.inv.at[slot], self.sem.at[1, slot]))
        return (pltpu.make_async_copy(self.outk.at[slot], self.okt_hbm.at[n, rows, :], self.sem.at[2, slot]),
                pltpu.make_async_copy(self.outv.at[slot], self.ovt_hbm.at[n, rows, :], self.sem.at[3, slot]))

    def _start(self, u, shift, inbound):
        for cp in self._copies(u, shift, inbound):
            cp.start()

    def _wait(self, u, shift, inbound):
        for cp in self._copies(u, shift, inbound):
            cp.wait()

    def prologue(self):
        @pl.when(self.step == 0)
        def _():
            for u in range(IN_SLOTS):
                self._start(u, 0, True)

    def wait_unit(self, u):
        self._wait(u, 0, True)
        if u >= OUT_SLOTS:
            self._wait(u, -OUT_SLOTS, False)
        else:
            @pl.when(self.step > 0)
            def _():
                self._wait(u, -OUT_SLOTS, False)

    def run_unit(self, u, q_ref, kn_ref, vn_ref, w_ref, o_ref):
        g = self.step * self.per_step + u
        islot = self._slot(u, IN_SLOTS)
        oslot = self._slot(u, OUT_SLOTS)
        o_ref[g] = _sample_a_unit(q_ref[g], kn_ref[g], vn_ref[g], self.ink[islot], self.inv[islot], w_ref[...],
                                  self.outk.at[oslot], self.outv.at[oslot])
        self._start(u, 0, False)
        if u + IN_SLOTS < self.per_step:
            self._start(u, IN_SLOTS, True)
        else:
            @pl.when(self.step < self.n_steps - 1)
            def _():
                self._start(u, IN_SLOTS, True)

    def epilogue(self):
        @pl.when(self.step == self.n_steps - 1)
        def _():
            for u in range(self.per_step - OUT_SLOTS, self.per_step):
                self._wait(u, 0, False)


def _stream_specs(n_units, R, L):
    rows = pl.BlockSpec((n_units, 1, R), lambda *_: (0, 0, 0))
    hbm = pl.BlockSpec(memory_space=pl.ANY)
    ring_in = pltpu.VMEM((IN_SLOTS, R, L), F32)
    ring_out = pltpu.VMEM((OUT_SLOTS, R, L), F32)
    in_specs = [rows, rows, rows, pl.BlockSpec((1, L), lambda *_: (0, 0)), hbm, hbm]
    out_specs = [rows, hbm, hbm]
    scratch = [ring_in, ring_in, ring_out, ring_out, pltpu.SemaphoreType.DMA((4, max(IN_SLOTS, OUT_SLOTS)))]
    return in_specs, out_specs, scratch


def _sample_b_kernel(q_ref, kn_ref, vn_ref, kt_ref, vt_ref, sink_ref, o_ref, okt_ref, ovt_ref):
    nb = q_ref.shape[0]
    lane = lax.broadcasted_iota(jnp.int32, (1, V7X_LANES), 1)
    lo = lane < HEAD_DIM
    for n in range(nb):
        q = q_ref[n:n + 1, :] * (HEAD_DIM ** -0.5)
        rows = []
        for j in range(B_W // V7X_LANES):
            pair = q[:, j * V7X_LANES:(j + 1) * V7X_LANES]
            rows += [jnp.where(lo, pair, 0.0), jnp.where(lo, 0.0, pair)]
        qh = jnp.concatenate(rows, axis=0)
        kt = kt_ref[n]
        vt = vt_ref[n]
        s_new = jnp.sum(qh * kn_ref[n:n + 1, :], axis=-1, keepdims=True)
        s = _dot(qh.astype(BF16), kt.astype(BF16))
        m = jnp.maximum(s_new, jnp.max(s, axis=-1, keepdims=True))
        p_new = jnp.exp(s_new - m)
        p = jnp.exp(s - m)
        l = p_new + jnp.sum(p, axis=-1, keepdims=True)
        o = p_new * vn_ref[n:n + 1, :] + _dot_t(p.astype(BF16), vt.astype(BF16))
        o = o / (l + jnp.exp(sink_ref[...] - m))
        o_ref[n:n + 1, :] = jnp.concatenate(
            [jnp.where(lo, o[2 * j:2 * j + 1], o[2 * j + 1:2 * j + 2]) for j in range(B_W // V7X_LANES)], axis=1)
        _roll_in(okt_ref.at[n], kt, _as_column(kn_ref[n:n + 1, :]))
        _roll_in(ovt_ref.at[n], vt, _as_column(vn_ref[n:n + 1, :]))


def _sample_b(q, k_new, v_new, kt, vt, sink_rows, *, nb):
    N, W, L = kt.shape
    assert L == B_WINDOW and W == V7X_LANES, "a cache shorter than the sliding window is unsupported"
    row = lambda w_: pl.BlockSpec((nb, w_), lambda i: (i, 0))
    cache = pl.BlockSpec((nb, W, L), lambda i: (i, 0, 0))
    return pl.pallas_call(
        _sample_b_kernel,
        grid=(N // nb,),
        in_specs=[row(B_W), row(W), row(W), cache, cache, pl.BlockSpec((8, 1), lambda i: (0, 0))],
        out_specs=[row(B_W), cache, cache],
        out_shape=[jax.ShapeDtypeStruct((N, B_W), F32), jax.ShapeDtypeStruct((N, W, L), F32),
                   jax.ShapeDtypeStruct((N, W, L), F32)],
        compiler_params=_params("arbitrary"),
        name="sample_mixer_b",
    )(q, k_new, v_new, kt, vt, sink_rows)


def _mixers_out(oa, ob, x, ga_ref, gb_ref, w_ref):
    ha = _rms(oa, ga_ref[...]).astype(BF16)
    hb = _rms(ob, gb_ref[...]).astype(BF16)
    return x + _dot(ha, w_ref[:A_W, :]) + _dot(hb, w_ref[A_W:, :])


def _outproj_kernel(oa_ref, ob_ref, x_ref, ga_ref, gb_ref, w_ref, y_ref):
    y_ref[...] = _mixers_out(oa_ref[...], ob_ref[...], x_ref[...], ga_ref, gb_ref, w_ref)


def _outproj(oa, ob, x, ga, gb, w):
    return pl.pallas_call(
        _outproj_kernel,
        out_shape=jax.ShapeDtypeStruct(x.shape, F32),
        compiler_params=pltpu.CompilerParams(vmem_limit_bytes=VMEM_LIMIT),
        name="mixers_outproj",
    )(oa, ob, x, ga, gb, w)


def _norm_matmul_kernel(x_ref, g_ref, w_ref, *out_refs):
    h = _rms(x_ref[...], g_ref[...]).astype(BF16)
    width = out_refs[0].shape[1]
    for j, o_ref in enumerate(out_refs):
        o_ref[...] = _dot(h, w_ref[:, j * width:(j + 1) * width])


def _norm_matmul(x, g, w, *, n_out, tm):
    T, D = x.shape
    width = w.shape[1] // n_out
    return pl.pallas_call(
        _norm_matmul_kernel,
        grid=(T // tm,),
        in_specs=[pl.BlockSpec((tm, D), lambda i: (i, 0)), pl.BlockSpec((1, D), lambda i: (0, 0)),
                  pl.BlockSpec(w.shape, lambda i: (0, 0))],
        out_specs=[pl.BlockSpec((tm, width), lambda i: (i, 0))] * n_out,
        out_shape=[jax.ShapeDtypeStruct((T, width), F32)] * n_out,
        compiler_params=_params("arbitrary"),
        name="norm_matmul",
    )(x, g, w)


def _matmul_residual_kernel(a_ref, w_ref, r_ref, y_ref):
    y_ref[...] = r_ref[...] + _dot(a_ref[...].astype(BF16), w_ref[...])


def _matmul_residual(a, w, r, *, tm):
    T, K = a.shape
    D = w.shape[1]
    return pl.pallas_call(
        _matmul_residual_kernel,
        grid=(T // tm,),
        in_specs=[pl.BlockSpec((tm, K), lambda i: (i, 0)), pl.BlockSpec(w.shape, lambda i: (0, 0)),
                  pl.BlockSpec((tm, D), lambda i: (i, 0))],
        out_specs=pl.BlockSpec((tm, D), lambda i: (i, 0)),
        out_shape=jax.ShapeDtypeStruct((T, D), F32),
        compiler_params=_params("arbitrary"),
        name="matmul_residual",
    )(a, w, r)


def _outproj_cross_kernel(oa_ref, ob_ref, x_ref, ga_ref, gb_ref, wout_ref, gc_ref, wq_ref, mk_ref, mv_ref, wo_ref,
                          y_ref, shuf_ref, q_ref):
    tq = oa_ref.shape[1]
    for c in range(A_W // V7X_LANES):
        for r in range(R4):
            shuf_ref[c, pl.ds(r, tq, stride=R4), :] = oa_ref[r, :, c * V7X_LANES:(c + 1) * V7X_LANES]
    oa = jnp.concatenate([shuf_ref[c] for c in range(A_W // V7X_LANES)], axis=1)
    y_ref[...] = _mixers_out(oa, ob_ref[...], x_ref[...], ga_ref, gb_ref, wout_ref)
    D = y_ref.shape[1]
    hd = D // X_HEADS
    q_ref[...] = (_dot(_rms(y_ref[...], gc_ref[...]).astype(BF16), wq_ref[...]) * (hd ** -0.5)).astype(BF16)
    for h in range(X_HEADS):
        cols = slice(h * hd, (h + 1) * hd)
        s = _dot_t(q_ref[:, cols], mk_ref[:, cols].astype(BF16))
        p = jnp.exp(s - jnp.max(s, axis=-1, keepdims=True))
        o = _dot(p.astype(BF16), mv_ref[:, cols].astype(BF16)) / jnp.sum(p, axis=-1, keepdims=True)
        y_ref[...] += _dot(o.astype(BF16), wo_ref[cols, :])


def _outproj_cross(oa, ob, x, ga, gb, wout, gc, wq, mk, mv, wo, *, tm):
    B, S, D = x.shape
    M = mk.shape[1]
    tok = lambda w_: pl.BlockSpec((None, tm, w_), lambda b, i: (b, i, 0))
    mem = pl.BlockSpec((None, M, D), lambda b, i: (b, 0, 0))
    const = lambda shape: pl.BlockSpec(shape, lambda b, i: (0,) * len(shape))
    return pl.pallas_call(
        _outproj_cross_kernel,
        grid=(B, S // tm),
        in_specs=[pl.BlockSpec((None, R4, tm // R4, A_W), lambda b, i: (b, 0, i, 0)), tok(B_W), tok(D),
                  const((1, A_W)), const((1, B_W)), const(wout.shape), const((1, D)), const(wq.shape),
                  mem, mem, const(wo.shape)],
        out_specs=tok(D),
        out_shape=jax.ShapeDtypeStruct((B, S, D), F32),
        scratch_shapes=[pltpu.VMEM((A_W // V7X_LANES, tm, V7X_LANES), F32), pltpu.VMEM((tm, D), BF16)],
        compiler_params=_params("arbitrary", "arbitrary"),
        name="outproj_cross_attention",
    )(oa, ob, x, ga, gb, wout, gc, wq, mk, mv, wo)


MEM_TILE_ROWS = 8


def _sample_cross_kernel(q_ref, mk_ref, mv_ref, o_ref):
    nb = q_ref.shape[0]
    M = mk_ref.shape[1] // MEM_TILE_ROWS
    for n in range(nb):
        q = q_ref[n]
        k3 = mk_ref[n].reshape(M, MEM_TILE_ROWS, V7X_LANES)
        v3 = mv_ref[n].reshape(M, MEM_TILE_ROWS, V7X_LANES)
        part = jnp.sum(k3 * q[None], axis=-1, keepdims=True)
        s = part + pltpu.roll(part, X_HEADS, 1)
        p = jnp.exp(s - jnp.max(s, axis=0, keepdims=True))
        o = jnp.sum(v3 * p, axis=0) / jnp.sum(p, axis=0)
        o_ref[n] = o


def _sample_cross(q, mk, mv, *, nb):
    N, R, _ = mk.shape
    mem = pl.BlockSpec((nb, R, V7X_LANES), lambda i: (i, 0, 0))
    row = pl.BlockSpec((nb, MEM_TILE_ROWS, V7X_LANES), lambda i: (i, 0, 0))
    return pl.pallas_call(
        _sample_cross_kernel,
        grid=(N // nb,),
        in_specs=[row, mem, mem],
        out_specs=row,
        out_shape=jax.ShapeDtypeStruct((N, MEM_TILE_ROWS, V7X_LANES), F32),
        compiler_params=_params("arbitrary"),
        name="sample_cross_attention",
    )(q, mk, mv)


def _mem_tile_rows(cache):
    N, M, H, Dh = cache.shape
    halves = Dh // V7X_LANES
    assert H * halves == MEM_TILE_ROWS
    return cache.reshape(N, M, H, halves, V7X_LANES).transpose(0, 1, 3, 2, 4).reshape(N, M * MEM_TILE_ROWS, V7X_LANES)


FF_CHUNKS = ((0, 1024), (1024, 1024), (2048, 768))


def _ffn_chunk(h, wup_ref, wdn_ref, cw_ref, cb_ref, d_ff, c0, cw, prev_rows):
    gate = _dot(h, wup_ref[:, c0:c0 + cw])
    val = _dot(h, wup_ref[:, d_ff + c0:d_ff + c0 + cw])
    g2, g1 = prev_rows(gate)
    conv = cb_ref[:, c0:c0 + cw] + (g2 * cw_ref[0:1, c0:c0 + cw] + g1 * cw_ref[1:2, c0:c0 + cw]
                                     + gate * cw_ref[2:3, c0:c0 + cw])
    act = (conv * (1.0 / (1.0 + jnp.exp(-conv)))) * val
    return gate, _dot(act.astype(BF16), wdn_ref[c0:c0 + cw, :])


FFN_UNITS = 4
FF_CHUNKS_PROMPT = ((0, 768), (768, 768), (1536, 768), (2304, 512))


def _ffn_prompt_kernel(x_ref, g_ref, wup_ref, cw_ref, cb_ref, wdn_ref, gf_ref,
                       sq_ref, skn_ref, svn_ref, sw_ref, kt_hbm, vt_hbm,
                       y_ref, state_ref, so_ref, okt_hbm, ovt_hbm,
                       halo_ref, ink_ref, inv_ref, outk_ref, outv_ref, sem, *, n_tiles, n_steps):
    i = pl.program_id(1)
    step = pl.program_id(0) * n_tiles + i
    tm = x_ref.shape[0]
    d_ff = wdn_ref.shape[0]
    stream = _CacheStream(step, n_steps, FFN_UNITS, 0, (kt_hbm, vt_hbm, okt_hbm, ovt_hbm),
                          (ink_ref, inv_ref, outk_ref, outv_ref), sem)

    @pl.when(i == 0)
    def _():
        halo_ref[...] = jnp.zeros_like(halo_ref)

    stream.prologue()
    x = x_ref[...]
    h = _rms(x, g_ref[...]).astype(BF16)
    y = x
    for u, (c0, cw) in enumerate(FF_CHUNKS_PROMPT):
        stream.wait_unit(u)
        row = lax.broadcasted_iota(jnp.int32, (tm, cw), 0)
        halo = halo_ref[:, c0:c0 + cw]

        def prev_rows(gate):
            g1 = jnp.where(row == 0, halo[1:2], pltpu.roll(gate, 1, 0))
            g2 = jnp.where(row == 0, halo[0:1], jnp.where(row == 1, halo[1:2], pltpu.roll(gate, 2, 0)))
            return g2, g1

        gate, contrib = _ffn_chunk(h, wup_ref, wdn_ref, cw_ref, cb_ref, d_ff, c0, cw, prev_rows)
        y = y + contrib
        halo_ref[0:2, c0:c0 + cw] = gate[tm - 2:, :]
        state_ref[:, c0:c0 + cw] = gate[tm - 2:, :]
        stream.run_unit(u, sq_ref, skn_ref, svn_ref, sw_ref, so_ref)

    y_ref[...] = _rms(y, gf_ref[...])
    stream.epilogue()


def _ffn_prompt(x, g, wup, cw, cb, wdn, gf, sq, skn, svn, kt, vt, *, tm):
    B, S, D = x.shape
    d_ff = wdn.shape[0]
    N, W, L = kt.shape
    assert FF_CHUNKS_PROMPT[-1][0] + FF_CHUNKS_PROMPT[-1][1] == d_ff and len(FF_CHUNKS_PROMPT) == FFN_UNITS
    assert L == A_WINDOW, "a cache shorter than the largest dilated window is unsupported"
    n_tiles = S // tm
    n_steps = B * n_tiles
    n_units = n_steps * FFN_UNITS
    R = W // CACHE_HALVES
    assert sq.shape == (n_units, 1, R) and n_units == N * CACHE_HALVES, "the cache units are spread evenly over the grid"
    tok = pl.BlockSpec((None, tm, D), lambda b, i: (b, i, 0))
    const = lambda shape: pl.BlockSpec(shape, lambda b, i: (0,) * len(shape))
    resident = lambda shape: pl.BlockSpec(shape, lambda b, i: (0,) * len(shape), pipeline_mode=pl.Buffered(1))
    s_in, s_out, s_scratch = _stream_specs(n_units, R, L)
    return pl.pallas_call(
        functools.partial(_ffn_prompt_kernel, n_tiles=n_tiles, n_steps=n_steps),
        grid=(B, n_tiles),
        in_specs=[tok, const((1, D)), resident(wup.shape), const(cw.shape), const(cb.shape),
                  resident(wdn.shape), const((1, D))] + s_in,
        out_specs=[tok, pl.BlockSpec((None, CONV_W - 1, d_ff), lambda b, i: (b, 0, 0))] + s_out,
        out_shape=[jax.ShapeDtypeStruct((B, S, D), F32),
                   jax.ShapeDtypeStruct((B, CONV_W - 1, d_ff), F32),
                   jax.ShapeDtypeStruct((n_units, 1, R), F32),
                   jax.ShapeDtypeStruct((N, W, L), F32), jax.ShapeDtypeStruct((N, W, L), F32)],
        scratch_shapes=[pltpu.VMEM((8, d_ff), F32)] + s_scratch,
        compiler_params=_params("arbitrary", "arbitrary"),
        name="conv_ffn_prompt_sample_a",
    )(x, g, wup, cw, cb, wdn, gf, sq, skn, svn, _branch_multiplicity(L), kt, vt)


def _ffn_sample_kernel(x_ref, s0_ref, s1_ref, g_ref, wup_ref, cw_ref, cb_ref, wdn_ref, gf_ref,
                       y_ref, gate_ref):
    d_ff = wdn_ref.shape[0]
    x = x_ref[...]
    h = _rms(x, g_ref[...]).astype(BF16)
    y = x
    for c0, cw in FF_CHUNKS:
        prev_rows = lambda gate: (s0_ref[:, c0:c0 + cw], s1_ref[:, c0:c0 + cw])
        gate, contrib = _ffn_chunk(h, wup_ref, wdn_ref, cw_ref, cb_ref, d_ff, c0, cw, prev_rows)
        y = y + contrib
        gate_ref[:, c0:c0 + cw] = gate
    y_ref[...] = _rms(y, gf_ref[...])


def _ffn_sample(x, s0, s1, g, wup, cw, cb, wdn, gf):
    N, D = x.shape
    d_ff = wdn.shape[0]
    return pl.pallas_call(
        _ffn_sample_kernel,
        out_shape=[jax.ShapeDtypeStruct((N, D), F32), jax.ShapeDtypeStruct((N, d_ff), F32)],
        compiler_params=pltpu.CompilerParams(vmem_limit_bytes=VMEM_LIMIT),
        name="conv_ffn_sample",
    )(x, s0, s1, g, wup, cw, cb, wdn, gf)


def _qb_pair_perm():
    cols = []
    for j in range(B_GROUP):
        for hk in range(B_KV_HEADS):
            h = hk * B_GROUP + j
            cols.extend(range(h * HEAD_DIM, (h + 1) * HEAD_DIM))
    return np.asarray(cols, dtype=np.int32)


def _layer_weights(l, g_mix, w_in, g_out_a, g_out_b, sinks, w_out, g_cross, g_mem, w_xq, w_mem_kv, w_xo,
                   g_ffn, w_up, conv_w, conv_b, w_down):
    perm = _qb_pair_perm()
    qb0 = 3 * A_W
    w_in_l = w_in[l]
    w_in_l = jnp.concatenate([w_in_l[:, :qb0], w_in_l[:, qb0:qb0 + B_W][:, perm], w_in_l[:, qb0 + B_W:]], axis=1)
    w_out_l = w_out[l]
    w_out_l = jnp.concatenate([w_out_l[:A_W], w_out_l[A_W:][perm]], axis=0)
    sink = sinks[l].astype(F32)
    pair = jnp.stack([sink[:B_GROUP], sink[B_GROUP:]], axis=1)
    sink_lanes = jnp.broadcast_to(jnp.repeat(pair, HEAD_DIM, axis=1)[:, None, :], (B_GROUP, 8, V7X_LANES))
    sink_rows = pair.reshape(2 * B_GROUP, 1)
    row = lambda v: v.reshape(1, -1).astype(F32)
    return dict(
        g_mix=row(g_mix[l]), w_in=w_in_l.astype(BF16),
        g_out_a=row(g_out_a[l]), g_out_b=row(g_out_b[l][perm]), w_out=w_out_l.astype(BF16),
        sink_lanes=sink_lanes, sink_rows=sink_rows,
        g_cross=row(g_cross[l]), g_mem=row(g_mem[l]), w_xq=w_xq[l].astype(BF16),
        w_mem_kv=w_mem_kv[l].astype(BF16), w_xo=w_xo[l].astype(BF16),
        g_ffn=row(g_ffn[l]), w_up=w_up[l].astype(BF16), conv_w=conv_w[l].astype(F32),
        conv_b=row(conv_b[l]), w_down=w_down[l].astype(BF16))


PROMPT_TM = 512
PROMPT_TM_WIDE = 1024
SAMPLE_NB = 8


def _prompt_layer(x, mem, W, g_final, cos, sin, sample_a):
    B, S, D = x.shape
    M = mem.shape[1]
    tm = min(PROMPT_TM, S)
    tm_wide = min(PROMPT_TM_WIDE, S)
    qa, ka, va, qb, kb, vb, ka_t, va_t, kb_t, vb_t = _inproj(
        x, W["g_mix"], W["w_in"], cos, sin, tm=tm_wide, prompt=True)
    flat = lambda t: t.reshape(B, S, A_W)
    oa = _attn_a(flat(qa), flat(ka), flat(va)).reshape(B, R4, S // R4, A_W)
    ob = _attn_b(qb, kb, vb, W["sink_lanes"])
    mk, mv = _norm_matmul(mem.reshape(B * M, D), W["g_mem"], W["w_mem_kv"], n_out=2, tm=min(512, B * M))
    mk = mk.reshape(B, M, D)
    mv = mv.reshape(B, M, D)
    x2 = _outproj_cross(oa, ob, x, W["g_out_a"], W["g_out_b"], W["w_out"], W["g_cross"], W["w_xq"], mk, mv,
                        W["w_xo"], tm=tm_wide)
    sq, skn, svn, kt, vt = sample_a
    N, W_, L = kt.shape
    unit_rows = lambda t: t.reshape(N * CACHE_HALVES, 1, W_ // CACHE_HALVES)
    y, conv_state, o_units, s_akt, s_avt = _ffn_prompt(
        x2, W["g_ffn"], W["w_up"], W["conv_w"], W["conv_b"], W["w_down"], g_final,
        unit_rows(sq), unit_rows(skn), unit_rows(svn), kt, vt, tm=tm)
    return y, (ka_t, va_t, kb_t, vb_t, mk, mv, conv_state), (o_units.reshape(N, W_), s_akt, s_avt)


def _sample_inproj(x, W, cos, sin):
    N = x.shape[0]
    return [t[0] for t in _inproj(x[None], W["g_mix"], W["w_in"], cos, sin, tm=N, prompt=False)]


def _sample_layer(x, oa, qb, kb, vb, b_kt, b_vt, mem_k, mem_v, conv_state, W, g_final):
    N, D = x.shape
    ob, s_bkt, s_bvt = _sample_b(qb, kb, vb, b_kt, b_vt, W["sink_rows"], nb=SAMPLE_NB)
    x1 = _outproj(oa, ob, x, W["g_out_a"], W["g_out_b"], W["w_out"])
    (q,) = _norm_matmul(x1, W["g_cross"], W["w_xq"], n_out=1, tm=N)
    hd = D // X_HEADS
    halves = hd // V7X_LANES
    q = (q * (hd ** -0.5)).reshape(N, X_HEADS, halves, V7X_LANES).transpose(0, 2, 1, 3)
    o = _sample_cross(q.reshape(N, MEM_TILE_ROWS, V7X_LANES), mem_k, mem_v, nb=SAMPLE_NB)
    o = o.reshape(N, halves, X_HEADS, V7X_LANES).transpose(0, 2, 1, 3).reshape(N, D)
    x2 = _matmul_residual(o, W["w_xo"], x1, tm=N)
    y, gate = _ffn_sample(x2, conv_state[:, 0], conv_state[:, 1], W["g_ffn"], W["w_up"], W["conv_w"],
                          W["conv_b"], W["w_down"], g_final)
    new_conv = jnp.stack([conv_state[:, 1], gate], axis=1)
    return y, (s_bkt, s_bvt, new_conv)


def _time_minor(cache):
    N, L, H, Dh = cache.shape
    return cache.transpose(0, 2, 3, 1).reshape(N, H * Dh, L)


def _time_major(cache_t, H):
    N, W, L = cache_t.shape
    return cache_t.reshape(N, H, W // H, L).transpose(0, 3, 1, 2)


def kernel(x_prompt, x_sample, cache_a_k, cache_a_v, cache_b_k, cache_b_v, cache_mem_k, cache_mem_v, state_conv,
           mem_prompt, g_mix, w_in, g_out_a, g_out_b, sinks, w_out, g_cross, g_mem, w_xq, w_mem_kv, w_xo,
           g_ffn, w_up, conv_w, conv_b, w_down, g_final):
    depth = w_in.shape[0]
    assert depth == 1, "layer stacking is not wired up: the problem has a single layer"
    B, S, D = x_prompt.shape
    N, T, _ = x_sample.shape
    assert T == 1, "the sample group decodes one token per sequence"
    gf = g_final.reshape(1, D).astype(F32)
    cos_p, sin_p = _rope_tables(S, 0, 1)
    cos_s, sin_s = _rope_tables(N, PAST_LEN, 0)

    l = 0
    W = _layer_weights(l, g_mix, w_in, g_out_a, g_out_b, sinks, w_out, g_cross, g_mem, w_xq, w_mem_kv, w_xo,
                       g_ffn, w_up, conv_w, conv_b, w_down)

    xs = x_sample.reshape(N, D)
    qa, ka, va, qb, kb, vb = _sample_inproj(xs, W, cos_s, sin_s)
    sample_a = (qa, ka, va, _time_minor(cache_a_k[l]), _time_minor(cache_a_v[l]))
    yp, (p_ak, p_av, p_bk, p_bv, p_mk, p_mv, p_conv), (oa_s, s_akt, s_avt) = _prompt_layer(
        x_prompt, mem_prompt, W, gf, cos_p, sin_p, sample_a)
    ys, (s_bkt, s_bvt, s_conv) = _sample_layer(
        xs, oa_s, qb, kb, vb, _time_minor(cache_b_k[l]), _time_minor(cache_b_v[l]),
        _mem_tile_rows(cache_mem_k[l]), _mem_tile_rows(cache_mem_v[l]), state_conv[l], W, gf)

    la, lb = p_ak.shape[1], p_bk.shape[1]
    return (yp, ys.reshape(N, 1, D),
            p_ak.reshape(1, B, la, A_HEADS, HEAD_DIM), p_av.reshape(1, B, la, A_HEADS, HEAD_DIM),
            p_bk.reshape(1, B, lb, B_KV_HEADS, HEAD_DIM), p_bv.reshape(1, B, lb, B_KV_HEADS, HEAD_DIM),
            p_mk.reshape(1, B, -1, X_HEADS, D // X_HEADS), p_mv.reshape(1, B, -1, X_HEADS, D // X_HEADS),
            p_conv[None],
            _time_major(s_akt, A_HEADS)[None], _time_major(s_avt, A_HEADS)[None],
            _time_major(s_bkt, B_KV_HEADS)[None], _time_major(s_bvt, B_KV_HEADS)[None],
            s_conv[None])
```
